```python
import jax, jax.numpy as jnp
from jax import lax
import numpy as np

D_MODEL = 1024
BATCH = 4
SEQ = 4096
DEPTH = 2

N_META = 16
N_A_LAYERS = DEPTH // 2
N_B_LAYERS = DEPTH - N_A_LAYERS
POOL_WINDOWS = (2, 4, 8, 16)
N_POOL_GROUPS = len(POOL_WINDOWS)
POOL_GROUP_DIM = D_MODEL // N_POOL_GROUPS
N_HEADS = 16
HEAD_DIM = D_MODEL // N_HEADS
Q_BLOCK = 128
D_FF = ((8 * D_MODEL // 3 + 127) // 128) * 128
CONV_WIDTH = 3
RMS_EPS = 1e-6

kernel_name = "yoco_pool_stickbreak_convffn"


def rms_norm(x, g):
    xf = x.astype(jnp.float32)
    y = xf * lax.rsqrt(jnp.mean(xf * xf, axis=-1, keepdims=True) + RMS_EPS)
    return (y * g.astype(jnp.float32)).astype(x.dtype)


def multiscale_pool(h, w_groups, scale):
    b, l, d = h.shape
    hf = h.astype(jnp.float32)
    csum = jnp.concatenate([jnp.zeros((b, 1, d), jnp.float32), jnp.cumsum(hf, axis=1)], axis=1)
    hg = hf.reshape(b, l, N_POOL_GROUPS, POOL_GROUP_DIM)
    cg = csum.reshape(b, l + 1, N_POOL_GROUPS, POOL_GROUP_DIM)
    t = jnp.arange(l)
    diffs = []
    for g, w in enumerate(POOL_WINDOWS):
        lo = jnp.maximum(t + 1 - w, 0)
        count = (t + 1 - lo).astype(jnp.float32)
        cgg = cg[:, :, g]
        window_sum = cgg[:, 1:] - cgg[:, lo]
        diffs.append(window_sum / count[None, :, None] - hg[:, :, g])
    diff = jnp.stack(diffs, axis=2).astype(h.dtype)
    y = jnp.einsum('blgc,gcd->blgd', diff, w_groups).reshape(b, l, d)
    return y * scale


def causal_dwconv(u, w, bias):
    l = u.shape[1]
    up = jnp.pad(u, ((0, 0), (CONV_WIDTH - 1, 0), (0, 0)))
    out = bias + w[0] * up[:, 0:l]
    for k in range(1, CONV_WIDTH):
        out = out + w[k] * up[:, k:k + l]
    return out


def conv_ffn(h, w_up, conv_w, conv_b, w_down):
    u = causal_dwconv(h @ w_up, conv_w, conv_b)
    gate, val = jnp.split(u, 2, axis=-1)
    return (jax.nn.silu(gate) * val) @ w_down


def shared_kv(h, kv_norm, w_kv):
    b, l, _ = h.shape
    kv = rms_norm(h, kv_norm) @ w_kv
    k, v = jnp.split(kv, 2, axis=-1)
    k = k.reshape(b, l, N_HEADS, HEAD_DIM).transpose(0, 2, 1, 3)
    v = v.reshape(b, l, N_HEADS, HEAD_DIM).transpose(0, 2, 1, 3)
    return k, v


def stick_breaking_block(q_blk, pos_q, k, v):
    z = jnp.einsum('bhqd,bhsd->bhqs', q_blk, k).astype(jnp.float32) * (HEAD_DIM ** -0.5)
    pos_k = jnp.arange(k.shape[2])
    mask = pos_k[None, :] < pos_q[:, None]
    log_beta = jax.nn.log_sigmoid(z)
    log_1m_beta = jnp.where(mask, jax.nn.log_sigmoid(-z), 0.0)
    later = lax.cumsum(log_1m_beta, axis=3, reverse=True) - log_1m_beta
    a = jnp.where(mask, jnp.exp(log_beta + later), 0.0)
    return jnp.einsum('bhqs,bhsd->bhqd', a.astype(v.dtype), v)


def stick_breaking_attention(h, w_q, k, v, w_o):
    b, l, d = h.shape
    n_real = l - N_META
    n_blk = n_real // Q_BLOCK
    q = (h @ w_q).reshape(b, l, N_HEADS, HEAD_DIM).transpose(0, 2, 1, 3)
    o_meta = stick_breaking_block(q[:, :, :N_META], jnp.arange(N_META),
                                  k[:, :, :N_META], v[:, :, :N_META])
    q_real = q[:, :, N_META:].reshape(b, N_HEADS, n_blk, Q_BLOCK, HEAD_DIM).transpose(2, 0, 1, 3, 4)
    pos_real = (N_META + jnp.arange(n_real)).reshape(n_blk, Q_BLOCK)
    o_real = lax.map(lambda args: stick_breaking_block(args[0], args[1], k, v), (q_real, pos_real))
    o_real = o_real.transpose(1, 2, 0, 3, 4).reshape(b, N_HEADS, n_real, HEAD_DIM)
    o = jnp.concatenate([o_meta, o_real], axis=2).transpose(0, 2, 1, 3).reshape(b, l, d)
    return o @ w_o


def setup_inputs(seed: int = 0) -> dict:
    key = jax.random.key(seed)
    ks = jax.random.split(key, 16)
    f32 = jnp.float32
    nrm = lambda k, shape, s: jax.random.normal(k, shape, f32) * s
    return {
        "x": nrm(ks[0], (BATCH, SEQ, D_MODEL), 1.0),
        "meta_tokens": nrm(ks[1], (N_META, D_MODEL), 1.0),
        "mix_norm": 1.0 + nrm(ks[2], (DEPTH, D_MODEL), 0.05),
        "ffn_norm": 1.0 + nrm(ks[3], (DEPTH, D_MODEL), 0.05),
        "pool_w": nrm(ks[4], (N_A_LAYERS, N_POOL_GROUPS, POOL_GROUP_DIM, POOL_GROUP_DIM), POOL_GROUP_DIM ** -0.5),
        "pool_scale": 1.0 + nrm(ks[5], (N_A_LAYERS, D_MODEL), 0.1),
        "kv_norm": 1.0 + nrm(ks[6], (D_MODEL,), 0.05),
        "w_kv": nrm(ks[7], (D_MODEL, 2 * D_MODEL), D_MODEL ** -0.5),
        "w_q": nrm(ks[8], (N_B_LAYERS, D_MODEL, D_MODEL), D_MODEL ** -0.5),
        "w_o": nrm(ks[9], (N_B_LAYERS, D_MODEL, D_MODEL), D_MODEL ** -0.5),
        "ffn_w_up": nrm(ks[10], (DEPTH, D_MODEL, 2 * D_FF), D_MODEL ** -0.5),
        "ffn_conv_w": nrm(ks[11], (DEPTH, CONV_WIDTH, 2 * D_FF), CONV_WIDTH ** -0.5),
        "ffn_conv_b": nrm(ks[12], (DEPTH, 2 * D_FF), 0.01),
        "ffn_w_down": nrm(ks[13], (DEPTH, D_FF, D_MODEL), D_FF ** -0.5),
        "final_norm": 1.0 + nrm(ks[14], (D_MODEL,), 0.05),
    }


def reference(x, meta_tokens, mix_norm, ffn_norm, pool_w, pool_scale, kv_norm, w_kv,
              w_q, w_o, ffn_w_up, ffn_conv_w, ffn_conv_b, ffn_w_down, final_norm):
    b = x.shape[0]
    meta = jnp.broadcast_to(meta_tokens[None].astype(x.dtype), (b, N_META, D_MODEL))
    h = jnp.concatenate([meta, x], axis=1)
    k = v = None
    for layer in range(DEPTH):
        if layer < N_A_LAYERS:
            h = h + multiscale_pool(rms_norm(h, mix_norm[layer]), pool_w[layer], pool_scale[layer])
        else:
            if layer == N_A_LAYERS:
                k, v = shared_kv(h, kv_norm, w_kv)
            j = layer - N_A_LAYERS
            h = h + stick_breaking_attention(rms_norm(h, mix_norm[layer]), w_q[j], k, v, w_o[j])
        h = h + conv_ffn(rms_norm(h, ffn_norm[layer]), ffn_w_up[layer], ffn_conv_w[layer],
                         ffn_conv_b[layer], ffn_w_down[layer])
    return rms_norm(h, final_norm)[:, N_META:]
```

```python
import functools

import numpy as np
import jax
import jax.numpy as jnp
from jax import lax
from jax.experimental import pallas as pl
from jax.experimental.pallas import tpu as pltpu

D_MODEL = 1024
BATCH = 4
SEQ = 4096
N_META = 16
POOL_WINDOWS = (2, 4, 8, 16)
POOL_GROUP_DIM = D_MODEL // len(POOL_WINDOWS)
N_HEADS = 16
HEAD_DIM = D_MODEL // N_HEADS
D_FF = 2816
CONV_WIDTH = 3
RMS_EPS = 1e-6

SUBLANES = 8
LANES = 128
MXU_DIM = 256

BLK = MXU_DIM
ROW_PAD = BLK - N_META
LP = SEQ + BLK
N_BLK = LP // BLK
ROWS = BATCH * LP
MAX_WINDOW = max(POOL_WINDOWS)

TM_POOL = BLK
TM_FFN = 512
FFN_CHUNK = MXU_DIM
N_FFN_CHUNKS = D_FF // FFN_CHUNK
VMEM_LIMIT = 56 * 1024 * 1024

F32 = jnp.float32
BF16 = jnp.bfloat16


def _rms_scale(x):
    return lax.rsqrt(jnp.mean(x * x, axis=-1, keepdims=True) + RMS_EPS)


def _pool_kernel(h_ref, g_ref, w_ref, sc_ref, o_ref, buf_ref):
    j = pl.program_id(1)
    tm = TM_POOL

    @pl.when(j == 0)
    def _():
        buf_ref[0:MAX_WINDOW, :] = jnp.zeros((MAX_WINDOW, D_MODEL), F32)

    x = h_ref[0]
    xn = (x * _rms_scale(x)) * g_ref[...]
    buf_ref[MAX_WINDOW:MAX_WINDOW + tm, :] = xn
    pos = j * tm + lax.broadcasted_iota(jnp.int32, (tm, 1), 0) - ROW_PAD
    ys = []
    for g, w in enumerate(POOL_WINDOWS):
        c0 = g * POOL_GROUP_DIM
        cur = xn[:, c0:c0 + POOL_GROUP_DIM]
        s = cur
        for k in range(1, w):
            s = s + buf_ref[MAX_WINDOW - k:MAX_WINDOW - k + tm, c0:c0 + POOL_GROUP_DIM]
        count = jnp.clip(pos + 1, 1, w).astype(F32)
        diff = s / count - cur
        ys.append(jnp.dot(diff.astype(BF16), w_ref[g], preferred_element_type=F32))
    y = jnp.concatenate(ys, axis=1) * sc_ref[...]
    o_ref[0] = jnp.where(pos >= 0, x + y, 0.0)
    buf_ref[0:MAX_WINDOW, :] = buf_ref[tm:tm + MAX_WINDOW, :]


def _pool_layer(h, gain, w, scale):
    return pl.pallas_call(
        _pool_kernel,
        out_shape=jax.ShapeDtypeStruct((BATCH, LP, D_MODEL), F32),
        grid=(BATCH, LP // TM_POOL),
        in_specs=[
            pl.BlockSpec((1, TM_POOL, D_MODEL), lambda b, j: (b, j, 0)),
            pl.BlockSpec((1, D_MODEL), lambda b, j: (0, 0)),
            pl.BlockSpec((len(POOL_WINDOWS), POOL_GROUP_DIM, POOL_GROUP_DIM), lambda b, j: (0, 0, 0)),
            pl.BlockSpec((1, D_MODEL), lambda b, j: (0, 0)),
        ],
        out_specs=pl.BlockSpec((1, TM_POOL, D_MODEL), lambda b, j: (b, j, 0)),
        scratch_shapes=[pltpu.VMEM((MAX_WINDOW + TM_POOL, D_MODEL), F32)],
        compiler_params=pltpu.CompilerParams(dimension_semantics=("arbitrary", "arbitrary")),
        name="pool_mixer",
    )(h, gain, w, scale)


def _pad_row_mask(row0, tm):
    row = row0 + lax.broadcasted_iota(jnp.int32, (tm, 1), 0)
    is_pad = jnp.zeros((tm, 1), jnp.bool_)
    for b in range(BATCH):
        is_pad = is_pad | ((row >= b * LP) & (row < b * LP + ROW_PAD))
    return jnp.logical_not(is_pad)


def _ffn_kernel(h_ref, g_ref, wup_ref, cw_ref, cb_ref, wdn_ref, fg_ref, o_ref, carry_ref, stage_ref,
                *, final_norm):
    i = pl.program_id(0)
    tm = TM_FFN

    @pl.when(i == 0)
    def _():
        carry_ref[...] = jnp.zeros(carry_ref.shape, F32)

    x = h_ref[...]
    xn = ((x * _rms_scale(x)) * g_ref[...]).astype(BF16)
    acc = jnp.zeros((tm, D_MODEL), F32)
    for c in range(N_FFN_CHUNKS):
        halves = []
        for half in range(2):
            col0 = half * D_FF + c * FFN_CHUNK
            slot = 2 * (c % 2) + half
            idx = half * N_FFN_CHUNKS + c
            u = jnp.dot(xn, wup_ref[:, col0:col0 + FFN_CHUNK], preferred_element_type=F32)
            stage_ref[slot, 0:SUBLANES, :] = carry_ref[idx]
            stage_ref[slot, SUBLANES:SUBLANES + tm, :] = u
            carry_ref[idx] = u[tm - SUBLANES:tm, :]
            w = cw_ref[:, col0:col0 + FFN_CHUNK]
            cv = (cb_ref[:, col0:col0 + FFN_CHUNK]
                  + w[0:1] * stage_ref[slot, SUBLANES - 2:SUBLANES - 2 + tm, :]
                  + w[1:2] * stage_ref[slot, SUBLANES - 1:SUBLANES - 1 + tm, :]
                  + w[2:3] * u)
            halves.append(cv)
        gate, val = halves
        act = (gate / (1.0 + jnp.exp(-gate))) * val
        acc = acc + jnp.dot(act.astype(BF16), wdn_ref[c * FFN_CHUNK:(c + 1) * FFN_CHUNK, :],
                            preferred_element_type=F32)
    out = x + acc
    if final_norm:
        out = (out * _rms_scale(out)) * fg_ref[...]
    o_ref[...] = jnp.where(_pad_row_mask(i * tm, tm), out, 0.0)


def _ffn_layer(h, gain, w_up, conv_w, conv_b, w_down, final_gain, final_norm):
    const = lambda i: (0, 0)
    resident = functools.partial(pl.BlockSpec, index_map=const, pipeline_mode=pl.Buffered(1))
    return pl.pallas_call(
        functools.partial(_ffn_kernel, final_norm=final_norm),
        out_shape=jax.ShapeDtypeStruct((ROWS, D_MODEL), F32),
        grid=(ROWS // TM_FFN,),
        in_specs=[
            pl.BlockSpec((TM_FFN, D_MODEL), lambda i: (i, 0)),
            pl.BlockSpec((1, D_MODEL), const),
            resident((D_MODEL, 2 * D_FF)),
            pl.BlockSpec((CONV_WIDTH, 2 * D_FF), const),
            pl.BlockSpec((1, 2 * D_FF), const),
            resident((D_FF, D_MODEL)),
            pl.BlockSpec((1, D_MODEL), const),
        ],
        out_specs=pl.BlockSpec((TM_FFN, D_MODEL), lambda i: (i, 0)),
        scratch_shapes=[
            pltpu.VMEM((2 * N_FFN_CHUNKS, SUBLANES, FFN_CHUNK), F32),
            pltpu.VMEM((4, SUBLANES + TM_FFN, FFN_CHUNK), F32),
        ],
        compiler_params=pltpu.CompilerParams(dimension_semantics=("arbitrary",),
                                             vmem_limit_bytes=VMEM_LIMIT),
        name="conv_ffn_final" if final_norm else "conv_ffn",
    )(h, gain, w_up, conv_w, conv_b, w_down, final_gain)


_NT_DIMS = (((1,), (1,)), ((), ()))
_TN_DIMS = (((0,), (0,)), ((), ()))


def _proj_kernel(h_ref, gkv_ref, gq_ref, wk_ref, wvt_ref, wqt_ref, k_ref, vt_ref, qt_ref):
    x = h_ref[0]
    xr = x * _rms_scale(x)
    xk = (xr * gkv_ref[...]).astype(BF16)
    xq = (xr * gq_ref[...]).astype(BF16)
    k_ref[0] = jnp.dot(xk, wk_ref[...], preferred_element_type=F32).astype(BF16)
    vt_ref[0] = lax.dot_general(wvt_ref[...], xk, _NT_DIMS, preferred_element_type=F32).astype(BF16)
    qt_ref[0] = lax.dot_general(wqt_ref[...], xq, _NT_DIMS, preferred_element_type=F32).astype(BF16)


def _projections(h, g_kv, g_q, w_k, w_vt, w_qt):
    const = lambda b, j: (0, 0)
    tm = TM_POOL
    row_spec = pl.BlockSpec((1, tm, D_MODEL), lambda b, j: (b, j, 0))
    col_spec = pl.BlockSpec((1, D_MODEL, tm), lambda b, j: (b, 0, j))
    w_spec = pl.BlockSpec((D_MODEL, D_MODEL), const)
    return pl.pallas_call(
        _proj_kernel,
        out_shape=(jax.ShapeDtypeStruct((BATCH, LP, D_MODEL), BF16),
                   jax.ShapeDtypeStruct((BATCH, D_MODEL, LP), BF16),
                   jax.ShapeDtypeStruct((BATCH, D_MODEL, LP), BF16)),
        grid=(BATCH, LP // tm),
        in_specs=[row_spec, pl.BlockSpec((1, D_MODEL), const), pl.BlockSpec((1, D_MODEL), const),
                  w_spec, w_spec, w_spec],
        out_specs=(row_spec, col_spec, col_spec),
        compiler_params=pltpu.CompilerParams(dimension_semantics=("arbitrary", "arbitrary"),
                                             vmem_limit_bytes=VMEM_LIMIT),
        name="qkv_proj",
    )(h, g_kv, g_q, w_k, w_vt, w_qt)


def _suffix_sum_matrix():
    s = np.arange(BLK + SUBLANES)[:, None]
    j = np.arange(BLK)[None, :]
    return jnp.asarray(np.where(s < BLK, j > s, True), BF16)


def _attn_kernel(qt_ref, k_ref, vt_ref, u_ref, ot_ref, acc_ref, carry_ref):
    head = pl.program_id(1)
    i = pl.program_id(2)
    half = lax.rem(head, 2)
    pair_row = lax.broadcasted_iota(jnp.int32, (2 * HEAD_DIM, 1), 0)
    mine = (pair_row >= half * HEAD_DIM) & (pair_row < (half + 1) * HEAD_DIM)
    q = jnp.where(mine, qt_ref[0], jnp.zeros((), BF16))

    acc_ref[...] = jnp.zeros(acc_ref.shape, F32)
    carry_ref[...] = jnp.zeros(carry_ref.shape, F32)

    def block(j, masked):
        start = pl.multiple_of(j * BLK, BLK)
        kb = k_ref[0, pl.ds(start, BLK), :]
        z = jnp.dot(kb, q, preferred_element_type=F32)
        soft = jnp.log(1.0 + jnp.exp(-jnp.abs(z)))
        log_beta = jnp.minimum(z, 0.0) - soft
        log_1m = log_beta - z
        if masked:
            key = start + lax.broadcasted_iota(jnp.int32, (BLK, BLK), 0)
            qry = i * BLK + lax.broadcasted_iota(jnp.int32, (BLK, BLK), 1)
            valid = (key < qry) & (key >= ROW_PAD)
            log_1m = jnp.where(valid, log_1m, 0.0)
        hi = log_1m.astype(BF16)
        lo = (log_1m - hi.astype(F32)).astype(BF16)
        sums = (jnp.dot(u_ref[...], hi, preferred_element_type=F32)
                + jnp.dot(u_ref[...], lo, preferred_element_type=F32))
        carry = carry_ref[0:1, :]
        a = jnp.exp(log_beta + sums[0:BLK] + carry)
        if masked:
            a = jnp.where(valid, a, 0.0)
        vb = vt_ref[0, :, pl.ds(start, BLK)]
        acc_ref[...] += jnp.dot(vb, a.astype(BF16), preferred_element_type=F32)
        carry_ref[...] += sums[BLK:BLK + SUBLANES]

    block(i, True)

    def interior(t, c):
        block(i - 1 - t, False)
        return c

    lax.fori_loop(0, i - 1, interior, 0)

    @pl.when(i > 0)
    def _():
        block(0, True)

    ot_ref[0] = acc_ref[...].astype(BF16)


def _attention(q_t, k, v_t):
    u = _suffix_sum_matrix()
    return pl.pallas_call(
        _attn_kernel,
        out_shape=jax.ShapeDtypeStruct((BATCH, D_MODEL, LP), BF16),
        grid=(BATCH, N_HEADS, N_BLK),
        in_specs=[
            pl.BlockSpec((1, 2 * HEAD_DIM, BLK), lambda b, h, i: (b, h // 2, i)),
            pl.BlockSpec((1, LP, 2 * HEAD_DIM), lambda b, h, i: (b, 0, h // 2)),
            pl.BlockSpec((1, HEAD_DIM, LP), lambda b, h, i: (b, h, 0)),
            pl.BlockSpec((BLK + SUBLANES, BLK), lambda b, h, i: (0, 0)),
        ],
        out_specs=pl.BlockSpec((1, HEAD_DIM, BLK), lambda b, h, i: (b, h, i)),
        scratch_shapes=[pltpu.VMEM((HEAD_DIM, BLK), F32), pltpu.VMEM((SUBLANES, BLK), F32)],
        compiler_params=pltpu.CompilerParams(
            dimension_semantics=("arbitrary", "arbitrary", "arbitrary"), vmem_limit_bytes=VMEM_LIMIT),
        name="stickbreak_attn",
    )(q_t, k, v_t, u)


def _oproj_kernel(h_ref, ot_ref, wo_ref, o_ref):
    j = pl.program_id(1)
    y = lax.dot_general(ot_ref[0], wo_ref[...], _TN_DIMS, preferred_element_type=F32)
    pos = j * TM_POOL + lax.broadcasted_iota(jnp.int32, (TM_POOL, 1), 0) - ROW_PAD
    o_ref[0] = jnp.where(pos >= 0, h_ref[0] + y, 0.0)


def _out_projection(h, o_t, w_o):
    tm = TM_POOL
    row_spec = pl.BlockSpec((1, tm, D_MODEL), lambda b, j: (b, j, 0))
    return pl.pallas_call(
        _oproj_kernel,
        out_shape=jax.ShapeDtypeStruct((BATCH, LP, D_MODEL), F32),
        grid=(BATCH, LP // tm),
        in_specs=[row_spec,
                  pl.BlockSpec((1, D_MODEL, tm), lambda b, j: (b, 0, j)),
                  pl.BlockSpec((D_MODEL, D_MODEL), lambda b, j: (0, 0))],
        out_specs=row_spec,
        compiler_params=pltpu.CompilerParams(dimension_semantics=("arbitrary", "arbitrary")),
        name="out_proj",
    )(h, o_t, w_o)


def kernel(x, meta_tokens, mix_norm, ffn_norm, pool_w, pool_scale, kv_norm, w_kv, w_q, w_o,
           ffn_w_up, ffn_conv_w, ffn_conv_b, ffn_w_down, final_norm):
    row = lambda v: v.reshape(1, -1)
    meta = jnp.broadcast_to(meta_tokens[None].astype(x.dtype), (BATCH, N_META, D_MODEL))
    h = jnp.concatenate([jnp.zeros((BATCH, ROW_PAD, D_MODEL), x.dtype), meta, x], axis=1)

    def ffn(h, layer, final):
        out = _ffn_layer(h.reshape(ROWS, D_MODEL), row(ffn_norm[layer]), ffn_w_up[layer].astype(BF16),
                         ffn_conv_w[layer], row(ffn_conv_b[layer]), ffn_w_down[layer].astype(BF16),
                         row(final_norm), final)
        return out.reshape(BATCH, LP, D_MODEL)

    h = _pool_layer(h, row(mix_norm[0]), pool_w[0].astype(BF16), row(pool_scale[0]))
    h = ffn(h, 0, False)

    w_k, w_v = w_kv[:, :D_MODEL], w_kv[:, D_MODEL:]
    k, v_t, q_t = _projections(h, row(kv_norm), row(mix_norm[1]), w_k.astype(BF16),
                               w_v.T.astype(BF16), (w_q[0].T * (HEAD_DIM ** -0.5)).astype(BF16))
    o_t = _attention(q_t, k, v_t)
    h = _out_projection(h, o_t, w_o[0].astype(BF16))
    h = ffn(h, 1, True)
    return h[:, BLK:]
```

```python
import functools

import numpy as np
import jax
import jax.numpy as jnp
from jax import lax
from jax.experimental import pallas as pl
from jax.experimental.pallas import tpu as pltpu

D_MODEL = 1024
BATCH = 4
SEQ = 4096
N_META = 16
POOL_WINDOWS = (2, 4, 8, 16)
POOL_GROUP_DIM = D_MODEL // len(POOL_WINDOWS)
N_HEADS = 16
HEAD_DIM = D_MODEL // N_HEADS
D_FF = 2816
CONV_WIDTH = 3
RMS_EPS = 1e-6

SUBLANES = 8
LANES = 128
MXU_DIM = 256

BLK = MXU_DIM
ROW_PAD = BLK - N_META
LP = SEQ + BLK
N_BLK = LP // BLK
ROWS = BATCH * LP
MAX_WINDOW = max(POOL_WINDOWS)
HEAD_GROUP = 8
PAIR = 2 * HEAD_DIM
UNDERFLOW_LOG = -104.0

TM_POOL = BLK
TM_FFN = 512
FFN_CHUNK = MXU_DIM
N_FFN_CHUNKS = D_FF // FFN_CHUNK
VMEM_LIMIT = 56 * 1024 * 1024

F32 = jnp.float32
BF16 = jnp.bfloat16


def _rms_scale(x):
    return lax.rsqrt(jnp.mean(x * x, axis=-1, keepdims=True) + RMS_EPS)


def _pool_kernel(h_ref, g_ref, w_ref, sc_ref, o_ref, buf_ref):
    j = pl.program_id(1)
    tm = TM_POOL

    @pl.when(j == 0)
    def _():
        buf_ref[0:MAX_WINDOW, :] = jnp.zeros((MAX_WINDOW, D_MODEL), F32)

    x = h_ref[0]
    xn = (x * _rms_scale(x)) * g_ref[...]
    buf_ref[MAX_WINDOW:MAX_WINDOW + tm, :] = xn
    pos = j * tm + lax.broadcasted_iota(jnp.int32, (tm, 1), 0) - ROW_PAD
    ys = []
    for g, w in enumerate(POOL_WINDOWS):
        c0 = g * POOL_GROUP_DIM
        cur = xn[:, c0:c0 + POOL_GROUP_DIM]
        s = cur
        for k in range(1, w):
            s = s + buf_ref[MAX_WINDOW - k:MAX_WINDOW - k + tm, c0:c0 + POOL_GROUP_DIM]
        count = jnp.clip(pos + 1, 1, w).astype(F32)
        diff = s / count - cur
        ys.append(jnp.dot(diff.astype(BF16), w_ref[g], preferred_element_type=F32))
    y = jnp.concatenate(ys, axis=1) * sc_ref[...]
    o_ref[0] = jnp.where(pos >= 0, x + y, 0.0)
    buf_ref[0:MAX_WINDOW, :] = buf_ref[tm:tm + MAX_WINDOW, :]


def _pool_layer(h, gain, w, scale):
    return pl.pallas_call(
        _pool_kernel,
        out_shape=jax.ShapeDtypeStruct((BATCH, LP, D_MODEL), F32),
        grid=(BATCH, LP // TM_POOL),
        in_specs=[
            pl.BlockSpec((1, TM_POOL, D_MODEL), lambda b, j: (b, j, 0)),
            pl.BlockSpec((1, D_MODEL), lambda b, j: (0, 0)),
            pl.BlockSpec((len(POOL_WINDOWS), POOL_GROUP_DIM, POOL_GROUP_DIM), lambda b, j: (0, 0, 0)),
            pl.BlockSpec((1, D_MODEL), lambda b, j: (0, 0)),
        ],
        out_specs=pl.BlockSpec((1, TM_POOL, D_MODEL), lambda b, j: (b, j, 0)),
        scratch_shapes=[pltpu.VMEM((MAX_WINDOW + TM_POOL, D_MODEL), F32)],
        compiler_params=pltpu.CompilerParams(dimension_semantics=("arbitrary", "arbitrary")),
        name="pool_mixer",
    )(h, gain, w, scale)


def _pad_row_mask(row0, tm):
    row = row0 + lax.broadcasted_iota(jnp.int32, (tm, 1), 0)
    is_pad = jnp.zeros((tm, 1), jnp.bool_)
    for b in range(BATCH):
        is_pad = is_pad | ((row >= b * LP) & (row < b * LP + ROW_PAD))
    return jnp.logical_not(is_pad)


def _ffn_kernel(h_ref, g_ref, wup_ref, cw_ref, cb_ref, wdn_ref, fg_ref, o_ref, carry_ref, stage_ref,
                *, final_norm):
    i = pl.program_id(0)
    tm = TM_FFN

    @pl.when(i == 0)
    def _():
        carry_ref[...] = jnp.zeros(carry_ref.shape, F32)

    x = h_ref[...]
    xn = ((x * _rms_scale(x)) * g_ref[...]).astype(BF16)
    acc = jnp.zeros((tm, D_MODEL), F32)
    for c in range(N_FFN_CHUNKS):
        halves = []
        for half in range(2):
            col0 = half * D_FF + c * FFN_CHUNK
            slot = 2 * (c % 2) + half
            idx = half * N_FFN_CHUNKS + c
            u = jnp.dot(xn, wup_ref[:, col0:col0 + FFN_CHUNK], preferred_element_type=F32)
            stage_ref[slot, 0:SUBLANES, :] = carry_ref[idx]
            stage_ref[slot, SUBLANES:SUBLANES + tm, :] = u
            carry_ref[idx] = u[tm - SUBLANES:tm, :]
            w = cw_ref[:, col0:col0 + FFN_CHUNK]
            cv = (cb_ref[:, col0:col0 + FFN_CHUNK]
                  + w[0:1] * stage_ref[slot, SUBLANES - 2:SUBLANES - 2 + tm, :]
                  + w[1:2] * stage_ref[slot, SUBLANES - 1:SUBLANES - 1 + tm, :]
                  + w[2:3] * u)
            halves.append(cv)
        gate, val = halves
        act = (gate / (1.0 + jnp.exp(-gate))) * val
        acc = acc + jnp.dot(act.astype(BF16), wdn_ref[c * FFN_CHUNK:(c + 1) * FFN_CHUNK, :],
                            preferred_element_type=F32)
    out = x + acc
    if final_norm:
        out = (out * _rms_scale(out)) * fg_ref[...]
    o_ref[...] = jnp.where(_pad_row_mask(i * tm, tm), out, 0.0)


def _ffn_layer(h, gain, w_up, conv_w, conv_b, w_down, final_gain, final_norm):
    const = lambda i: (0, 0)
    resident = functools.partial(pl.BlockSpec, index_map=const, pipeline_mode=pl.Buffered(1))
    return pl.pallas_call(
        functools.partial(_ffn_kernel, final_norm=final_norm),
        out_shape=jax.ShapeDtypeStruct((ROWS, D_MODEL), F32),
        grid=(ROWS // TM_FFN,),
        in_specs=[
            pl.BlockSpec((TM_FFN, D_MODEL), lambda i: (i, 0)),
            pl.BlockSpec((1, D_MODEL), const),
            resident((D_MODEL, 2 * D_FF)),
            pl.BlockSpec((CONV_WIDTH, 2 * D_FF), const),
            pl.BlockSpec((1, 2 * D_FF), const),
            resident((D_FF, D_MODEL)),
            pl.BlockSpec((1, D_MODEL), const),
        ],
        out_specs=pl.BlockSpec((TM_FFN, D_MODEL), lambda i: (i, 0)),
        scratch_shapes=[
            pltpu.VMEM((2 * N_FFN_CHUNKS, SUBLANES, FFN_CHUNK), F32),
            pltpu.VMEM((4, SUBLANES + TM_FFN, FFN_CHUNK), F32),
        ],
        compiler_params=pltpu.CompilerParams(dimension_semantics=("arbitrary",),
                                             vmem_limit_bytes=VMEM_LIMIT),
        name="conv_ffn_final" if final_norm else "conv_ffn",
    )(h, gain, w_up, conv_w, conv_b, w_down, final_gain)


_NT_DIMS = (((1,), (1,)), ((), ()))
_TN_DIMS = (((0,), (0,)), ((), ()))


def _proj_kernel(h_ref, gkv_ref, gq_ref, wk_ref, wvt_ref, wqt_ref, k_ref, vt_ref, qt_ref):
    x = h_ref[0]
    xr = x * _rms_scale(x)
    xk = (xr * gkv_ref[...]).astype(BF16)
    xq = (xr * gq_ref[...]).astype(BF16)
    k_ref[0] = jnp.dot(xk, wk_ref[...], preferred_element_type=F32).astype(BF16)
    vt_ref[0] = lax.dot_general(wvt_ref[...], xk, _NT_DIMS, preferred_element_type=F32).astype(BF16)
    qt_ref[0] = lax.dot_general(wqt_ref[...], xq, _NT_DIMS, preferred_element_type=F32).astype(BF16)


def _projections(h, g_kv, g_q, w_k, w_vt, w_qt):
    const = lambda b, j: (0, 0)
    tm = TM_POOL
    row_spec = pl.BlockSpec((1, tm, D_MODEL), lambda b, j: (b, j, 0))
    col_spec = pl.BlockSpec((1, D_MODEL, tm), lambda b, j: (b, 0, j))
    w_spec = pl.BlockSpec((D_MODEL, D_MODEL), const)
    return pl.pallas_call(
        _proj_kernel,
        out_shape=(jax.ShapeDtypeStruct((BATCH, LP, D_MODEL), BF16),
                   jax.ShapeDtypeStruct((BATCH, D_MODEL, LP), BF16),
                   jax.ShapeDtypeStruct((BATCH, D_MODEL, LP), BF16)),
        grid=(BATCH, LP // tm),
        in_specs=[row_spec, pl.BlockSpec((1, D_MODEL), const), pl.BlockSpec((1, D_MODEL), const),
                  w_spec, w_spec, w_spec],
        out_specs=(row_spec, col_spec, col_spec),
        compiler_params=pltpu.CompilerParams(dimension_semantics=("arbitrary", "arbitrary"),
                                             vmem_limit_bytes=VMEM_LIMIT),
        name="qkv_proj",
    )(h, g_kv, g_q, w_k, w_vt, w_qt)


def _suffix_sum_matrix():
    s = np.arange(BLK + SUBLANES)[:, None]
    j = np.arange(BLK)[None, :]
    return jnp.asarray(np.where(s < BLK, j > s, True), BF16)


def _neg_abs(x):
    bits = lax.bitcast_convert_type(x, jnp.uint32) | jnp.uint32(0x80000000)
    return lax.bitcast_convert_type(bits, F32)


def _attn_kernel(qt_ref, k_ref, vt_ref, u_ref, ot_ref, qm_ref, acc_ref, carry_ref):
    i = pl.program_id(2)
    pair_row = lax.broadcasted_iota(jnp.int32, (PAIR, 1), 0)
    for h in range(HEAD_GROUP):
        first = (h % 2) * HEAD_DIM
        mine = (pair_row >= first) & (pair_row < first + HEAD_DIM)
        qp = qt_ref[0, (h // 2) * PAIR:(h // 2 + 1) * PAIR, :]
        qm_ref[h] = jnp.where(mine, qp, jnp.zeros((), BF16))
    acc_ref[...] = jnp.zeros(acc_ref.shape, F32)
    carry_ref[...] = jnp.zeros(carry_ref.shape, F32)

    def block(j, masked):
        start = pl.multiple_of(j * BLK, BLK)
        if masked:
            key = start + lax.broadcasted_iota(jnp.int32, (BLK, BLK), 0)
            qry = i * BLK + lax.broadcasted_iota(jnp.int32, (BLK, BLK), 1)
            valid = (key < qry) & (key >= ROW_PAD)
        heads = range(HEAD_GROUP)
        zs = [jnp.dot(k_ref[0, pl.ds(start, BLK), (h // 2) * PAIR:(h // 2 + 1) * PAIR], qm_ref[h],
                      preferred_element_type=F32) for h in heads]
        log_betas, sums = [], []
        for h in heads:
            z = zs[h]
            soft = jnp.log(1.0 + jnp.exp(_neg_abs(z)))
            log_beta = jnp.minimum(z, 0.0) - soft
            log_1m = log_beta - z
            if masked:
                log_1m = jnp.where(valid, log_1m, 0.0)
            log_betas.append(log_beta)
            sums.append(jnp.dot(u_ref[...], log_1m.astype(BF16), preferred_element_type=F32))
        for h in heads:
            a = jnp.exp(log_betas[h] + sums[h][0:BLK] + carry_ref[h, 0:1, :])
            if masked:
                a = jnp.where(valid, a, 0.0)
            vb = vt_ref[0, h * HEAD_DIM:(h + 1) * HEAD_DIM, pl.ds(start, BLK)]
            acc_ref[h] += jnp.dot(vb, a.astype(BF16), preferred_element_type=F32)
            carry_ref[h] += sums[h][BLK:BLK + SUBLANES]

    def any_weight_left():
        m = carry_ref[0]
        for h in range(1, HEAD_GROUP):
            m = jnp.maximum(m, carry_ref[h])
        return jnp.max(m) >= UNDERFLOW_LOG

    block(i, True)

    def interior(state):
        j, _ = state
        block(j, False)
        return j - 1, any_weight_left()

    j_end, go = lax.while_loop(lambda s: (s[0] >= 1) & s[1], interior, (i - 1, any_weight_left()))

    @pl.when((j_end == 0) & go)
    def _():
        block(0, True)

    for h in range(HEAD_GROUP):
        ot_ref[0, h * HEAD_DIM:(h + 1) * HEAD_DIM, :] = acc_ref[h].astype(BF16)


def _attention(q_t, k, v_t):
    u = _suffix_sum_matrix()
    gw = HEAD_GROUP * HEAD_DIM
    return pl.pallas_call(
        _attn_kernel,
        out_shape=jax.ShapeDtypeStruct((BATCH, D_MODEL, LP), BF16),
        grid=(BATCH, N_HEADS // HEAD_GROUP, N_BLK),
        in_specs=[
            pl.BlockSpec((1, gw, BLK), lambda b, g, i: (b, g, i)),
            pl.BlockSpec((1, LP, gw), lambda b, g, i: (b, 0, g)),
            pl.BlockSpec((1, gw, LP), lambda b, g, i: (b, g, 0)),
            pl.BlockSpec((BLK + SUBLANES, BLK), lambda b, g, i: (0, 0)),
        ],
        out_specs=pl.BlockSpec((1, gw, BLK), lambda b, g, i: (b, g, i)),
        scratch_shapes=[pltpu.VMEM((HEAD_GROUP, PAIR, BLK), BF16),
                        pltpu.VMEM((HEAD_GROUP, HEAD_DIM, BLK), F32),
                        pltpu.VMEM((HEAD_GROUP, SUBLANES, BLK), F32)],
        compiler_params=pltpu.CompilerParams(
            dimension_semantics=("arbitrary", "arbitrary", "arbitrary"), vmem_limit_bytes=VMEM_LIMIT),
        name="stickbreak_attn",
    )(q_t, k, v_t, u)


def _oproj_kernel(h_ref, ot_ref, wo_ref, o_ref):
    j = pl.program_id(1)
    y = lax.dot_general(ot_ref[0], wo_ref[...], _TN_DIMS, preferred_element_type=F32)
    pos = j * TM_POOL + lax.broadcasted_iota(jnp.int32, (TM_POOL, 1), 0) - ROW_PAD
    o_ref[0] = jnp.where(pos >= 0, h_ref[0] + y, 0.0)


def _out_projection(h, o_t, w_o):
    tm = TM_POOL
    row_spec = pl.BlockSpec((1, tm, D_MODEL), lambda b, j: (b, j, 0))
    return pl.pallas_call(
        _oproj_kernel,
        out_shape=jax.ShapeDtypeStruct((BATCH, LP, D_MODEL), F32),
        grid=(BATCH, LP // tm),
        in_specs=[row_spec,
                  pl.BlockSpec((1, D_MODEL, tm), lambda b, j: (b, 0, j)),
                  pl.BlockSpec((D_MODEL, D_MODEL), lambda b, j: (0, 0))],
        out_specs=row_spec,
        compiler_params=pltpu.CompilerParams(dimension_semantics=("arbitrary", "arbitrary")),
        name="out_proj",
    )(h, o_t, w_o)


def kernel(x, meta_tokens, mix_norm, ffn_norm, pool_w, pool_scale, kv_norm, w_kv, w_q, w_o,
           ffn_w_up, ffn_conv_w, ffn_conv_b, ffn_w_down, final_norm):
    row = lambda v: v.reshape(1, -1)
    meta = jnp.broadcast_to(meta_tokens[None].astype(x.dtype), (BATCH, N_META, D_MODEL))
    h = jnp.concatenate([jnp.zeros((BATCH, ROW_PAD, D_MODEL), x.dtype), meta, x], axis=1)

    def ffn(h, layer, final):
        out = _ffn_layer(h.reshape(ROWS, D_MODEL), row(ffn_norm[layer]), ffn_w_up[layer].astype(BF16),
                         ffn_conv_w[layer], row(ffn_conv_b[layer]), ffn_w_down[layer].astype(BF16),
                         row(final_norm), final)
        return out.reshape(BATCH, LP, D_MODEL)

    h = _pool_layer(h, row(mix_norm[0]), pool_w[0].astype(BF16), row(pool_scale[0]))
    h = ffn(h, 0, False)

    w_k, w_v = w_kv[:, :D_MODEL], w_kv[:, D_MODEL:]
    k, v_t, q_t = _projections(h, row(kv_norm), row(mix_norm[1]), w_k.astype(BF16),
                               w_v.T.astype(BF16), (w_q[0].T * (HEAD_DIM ** -0.5)).astype(BF16))
    o_t = _attention(q_t, k, v_t)
    h = _out_projection(h, o_t, w_o[0].astype(BF16))
    h = ffn(h, 1, True)
    return h[:, BLK:]
```

```python
import functools

import numpy as np
import jax
import jax.numpy as jnp
from jax import lax
from jax.experimental import pallas as pl
from jax.experimental.pallas import tpu as pltpu

D_MODEL = 1024
BATCH = 4
SEQ = 4096
N_META = 16
POOL_WINDOWS = (2, 4, 8, 16)
POOL_GROUP_DIM = D_MODEL // len(POOL_WINDOWS)
N_HEADS = 16
HEAD_DIM = D_MODEL // N_HEADS
D_FF = 2816
CONV_WIDTH = 3
RMS_EPS = 1e-6

SUBLANES = 8
LANES = 128
MXU_DIM = 256

BLK = MXU_DIM
ROW_PAD = BLK - N_META
LP = SEQ + BLK
N_BLK = LP // BLK
ROWS = BATCH * LP
MAX_WINDOW = max(POOL_WINDOWS)
HEAD_GROUP = 8
PAIR = 2 * HEAD_DIM
UNDERFLOW_LOG = -104.0
LOG2_E = 1.4426950408889634

TM_POOL = BLK
TM_FFN = 512
FFN_CHUNK = MXU_DIM
N_FFN_CHUNKS = D_FF // FFN_CHUNK
VMEM_LIMIT = 56 * 1024 * 1024

F32 = jnp.float32
BF16 = jnp.bfloat16


def _rms_scale(x):
    return lax.rsqrt(jnp.mean(x * x, axis=-1, keepdims=True) + RMS_EPS)


def _pool_kernel(h_ref, g_ref, w_ref, sc_ref, o_ref, buf_ref):
    j = pl.program_id(1)
    tm = TM_POOL

    @pl.when(j == 0)
    def _():
        buf_ref[0:MAX_WINDOW, :] = jnp.zeros((MAX_WINDOW, D_MODEL), F32)

    x = h_ref[0]
    xn = (x * _rms_scale(x)) * g_ref[...]
    buf_ref[MAX_WINDOW:MAX_WINDOW + tm, :] = xn
    pos = j * tm + lax.broadcasted_iota(jnp.int32, (tm, 1), 0) - ROW_PAD
    ys = []
    for g, w in enumerate(POOL_WINDOWS):
        c0 = g * POOL_GROUP_DIM
        cur = xn[:, c0:c0 + POOL_GROUP_DIM]
        s = cur
        for k in range(1, w):
            s = s + buf_ref[MAX_WINDOW - k:MAX_WINDOW - k + tm, c0:c0 + POOL_GROUP_DIM]
        count = jnp.clip(pos + 1, 1, w).astype(F32)
        diff = s / count - cur
        ys.append(jnp.dot(diff.astype(BF16), w_ref[g], preferred_element_type=F32))
    y = jnp.concatenate(ys, axis=1) * sc_ref[...]
    o_ref[0] = jnp.where(pos >= 0, x + y, 0.0)
    buf_ref[0:MAX_WINDOW, :] = buf_ref[tm:tm + MAX_WINDOW, :]


def _pool_layer(h, gain, w, scale):
    return pl.pallas_call(
        _pool_kernel,
        out_shape=jax.ShapeDtypeStruct((BATCH, LP, D_MODEL), F32),
        grid=(BATCH, LP // TM_POOL),
        in_specs=[
            pl.BlockSpec((1, TM_POOL, D_MODEL), lambda b, j: (b, j, 0)),
            pl.BlockSpec((1, D_MODEL), lambda b, j: (0, 0)),
            pl.BlockSpec((len(POOL_WINDOWS), POOL_GROUP_DIM, POOL_GROUP_DIM), lambda b, j: (0, 0, 0)),
            pl.BlockSpec((1, D_MODEL), lambda b, j: (0, 0)),
        ],
        out_specs=pl.BlockSpec((1, TM_POOL, D_MODEL), lambda b, j: (b, j, 0)),
        scratch_shapes=[pltpu.VMEM((MAX_WINDOW + TM_POOL, D_MODEL), F32)],
        compiler_params=pltpu.CompilerParams(dimension_semantics=("arbitrary", "arbitrary")),
        name="pool_mixer",
    )(h, gain, w, scale)


def _pad_row_mask(row0, tm):
    row = row0 + lax.broadcasted_iota(jnp.int32, (tm, 1), 0)
    is_pad = jnp.zeros((tm, 1), jnp.bool_)
    for b in range(BATCH):
        is_pad = is_pad | ((row >= b * LP) & (row < b * LP + ROW_PAD))
    return jnp.logical_not(is_pad)


def _ffn_kernel(h_ref, g_ref, wup_ref, cw_ref, cb_ref, wdn_ref, fg_ref, o_ref, carry_ref, stage_ref,
                xn_ref, acc_ref, *, final_norm):
    i = pl.program_id(0)
    tm = TM_FFN

    @pl.when(i == 0)
    def _():
        carry_ref[...] = jnp.zeros(carry_ref.shape, F32)

    x = h_ref[...]
    xn_ref[...] = ((x * _rms_scale(x)) * g_ref[...]).astype(BF16)
    acc_ref[...] = x

    def up_project_conv(c):
        for half in range(2):
            col0 = half * D_FF + c * FFN_CHUNK
            idx = half * N_FFN_CHUNKS + c
            u = jnp.dot(xn_ref[...], wup_ref[:, col0:col0 + FFN_CHUNK], preferred_element_type=F32)
            ext = jnp.concatenate([carry_ref[idx], u], axis=0)
            carry_ref[idx] = u[tm - SUBLANES:tm, :]
            w = cw_ref[:, col0:col0 + FFN_CHUNK]
            out = cb_ref[:, col0:col0 + FFN_CHUNK] + w[CONV_WIDTH - 1:CONV_WIDTH] * u
            for k in range(CONV_WIDTH - 1):
                first = SUBLANES - (CONV_WIDTH - 1) + k
                out = out + w[k:k + 1] * ext[first:first + tm, :]
            stage_ref[2 * (c % 2) + half] = out

    up_project_conv(0)
    for c in range(N_FFN_CHUNKS):
        if c + 1 < N_FFN_CHUNKS:
            up_project_conv(c + 1)
        half_gate = 0.5 * stage_ref[2 * (c % 2)]
        act = (half_gate + half_gate * jnp.tanh(half_gate)) * stage_ref[2 * (c % 2) + 1]
        acc_ref[...] += jnp.dot(act.astype(BF16), wdn_ref[c * FFN_CHUNK:(c + 1) * FFN_CHUNK, :],
                                preferred_element_type=F32)
    out = acc_ref[...]
    if final_norm:
        out = (out * _rms_scale(out)) * fg_ref[...]
    o_ref[...] = jnp.where(_pad_row_mask(i * tm, tm), out, 0.0)


def _ffn_layer(h, gain, w_up, conv_w, conv_b, w_down, final_gain, final_norm):
    const = lambda i: (0, 0)
    resident = functools.partial(pl.BlockSpec, index_map=const, pipeline_mode=pl.Buffered(1))
    return pl.pallas_call(
        functools.partial(_ffn_kernel, final_norm=final_norm),
        out_shape=jax.ShapeDtypeStruct((ROWS, D_MODEL), F32),
        grid=(ROWS // TM_FFN,),
        in_specs=[
            pl.BlockSpec((TM_FFN, D_MODEL), lambda i: (i, 0)),
            pl.BlockSpec((1, D_MODEL), const),
            resident((D_MODEL, 2 * D_FF)),
            pl.BlockSpec((CONV_WIDTH, 2 * D_FF), const),
            pl.BlockSpec((1, 2 * D_FF), const),
            resident((D_FF, D_MODEL)),
            pl.BlockSpec((1, D_MODEL), const),
        ],
        out_specs=pl.BlockSpec((TM_FFN, D_MODEL), lambda i: (i, 0)),
        scratch_shapes=[
            pltpu.VMEM((2 * N_FFN_CHUNKS, SUBLANES, FFN_CHUNK), F32),
            pltpu.VMEM((4, TM_FFN, FFN_CHUNK), F32),
            pltpu.VMEM((TM_FFN, D_MODEL), BF16),
            pltpu.VMEM((TM_FFN, D_MODEL), F32),
        ],
        compiler_params=pltpu.CompilerParams(dimension_semantics=("arbitrary",),
                                             vmem_limit_bytes=VMEM_LIMIT),
        name="conv_ffn_final" if final_norm else "conv_ffn",
    )(h, gain, w_up, conv_w, conv_b, w_down, final_gain)


_NT_DIMS = (((1,), (1,)), ((), ()))
_TN_DIMS = (((0,), (0,)), ((), ()))


def _proj_kernel(h_ref, gkv_ref, gq_ref, wk_ref, wvt_ref, wqt_ref, k_ref, vt_ref, qt_ref):
    x = h_ref[0]
    xr = x * _rms_scale(x)
    xk = (xr * gkv_ref[...]).astype(BF16)
    xq = (xr * gq_ref[...]).astype(BF16)
    k_ref[0] = jnp.dot(xk, wk_ref[...], preferred_element_type=F32).astype(BF16)
    vt_ref[0] = lax.dot_general(wvt_ref[...], xk, _NT_DIMS, preferred_element_type=F32).astype(BF16)
    qt_ref[0] = lax.dot_general(wqt_ref[...], xq, _NT_DIMS, preferred_element_type=F32).astype(BF16)


def _projections(h, g_kv, g_q, w_k, w_vt, w_qt):
    const = lambda b, j: (0, 0)
    tm = TM_POOL
    row_spec = pl.BlockSpec((1, tm, D_MODEL), lambda b, j: (b, j, 0))
    col_spec = pl.BlockSpec((1, D_MODEL, tm), lambda b, j: (b, 0, j))
    w_spec = pl.BlockSpec((D_MODEL, D_MODEL), const)
    return pl.pallas_call(
        _proj_kernel,
        out_shape=(jax.ShapeDtypeStruct((BATCH, LP, D_MODEL), BF16),
                   jax.ShapeDtypeStruct((BATCH, D_MODEL, LP), BF16),
                   jax.ShapeDtypeStruct((BATCH, D_MODEL, LP), BF16)),
        grid=(BATCH, LP // tm),
        in_specs=[row_spec, pl.BlockSpec((1, D_MODEL), const), pl.BlockSpec((1, D_MODEL), const),
                  w_spec, w_spec, w_spec],
        out_specs=(row_spec, col_spec, col_spec),
        compiler_params=pltpu.CompilerParams(dimension_semantics=("arbitrary", "arbitrary"),
                                             vmem_limit_bytes=VMEM_LIMIT),
        name="qkv_proj",
    )(h, g_kv, g_q, w_k, w_vt, w_qt)


def _suffix_sum_matrix():
    s = np.arange(BLK + SUBLANES)[:, None]
    j = np.arange(BLK)[None, :]
    return jnp.asarray(np.where(s < BLK, j > s, True), BF16)


def _attn_kernel(qt_ref, k_ref, vt_ref, u_ref, ot_ref, qm_ref, acc_ref, carry_ref):
    i = pl.program_id(2)
    pair_row = lax.broadcasted_iota(jnp.int32, (PAIR, 1), 0)
    for h in range(HEAD_GROUP):
        first = (h % 2) * HEAD_DIM
        mine = (pair_row >= first) & (pair_row < first + HEAD_DIM)
        qp = qt_ref[0, (h // 2) * PAIR:(h // 2 + 1) * PAIR, :]
        qm_ref[h] = jnp.where(mine, qp, jnp.zeros((), BF16))
    acc_ref[...] = jnp.zeros(acc_ref.shape, F32)
    carry_ref[...] = jnp.zeros(carry_ref.shape, F32)

    def block(j, masked):
        start = pl.multiple_of(j * BLK, BLK)
        if masked:
            key = start + lax.broadcasted_iota(jnp.int32, (BLK, BLK), 0)
            qry = i * BLK + lax.broadcasted_iota(jnp.int32, (BLK, BLK), 1)
            valid = (key < qry) & (key >= ROW_PAD)
        heads = range(HEAD_GROUP)
        zs = [jnp.dot(k_ref[0, pl.ds(start, BLK), (h // 2) * PAIR:(h // 2 + 1) * PAIR], qm_ref[h],
                      preferred_element_type=F32) for h in heads]
        log_betas, sums = [], []
        for h in heads:
            z = zs[h]
            soft = jnp.log(1.0 + jnp.exp2(jnp.abs(z) * -LOG2_E))
            log_beta = jnp.minimum(z, 0.0) - soft
            log_1m = log_beta - z
            if masked:
                log_1m = jnp.where(valid, log_1m, 0.0)
            log_betas.append(log_beta)
            sums.append(jnp.dot(u_ref[...], log_1m.astype(BF16), preferred_element_type=F32))
        for h in heads:
            a = jnp.exp(log_betas[h] + sums[h][0:BLK] + carry_ref[h, 0:1, :])
            if masked:
                a = jnp.where(valid, a, 0.0)
            vb = vt_ref[0, h * HEAD_DIM:(h + 1) * HEAD_DIM, pl.ds(start, BLK)]
            acc_ref[h] += jnp.dot(vb, a.astype(BF16), preferred_element_type=F32)
            carry_ref[h] += sums[h][BLK:BLK + SUBLANES]

    def any_weight_left():
        m = carry_ref[0]
        for h in range(1, HEAD_GROUP):
            m = jnp.maximum(m, carry_ref[h])
        return jnp.max(m) >= UNDERFLOW_LOG

    block(i, True)

    def interior(state):
        j, _ = state
        block(j, False)
        return j - 1, any_weight_left()

    j_end, go = lax.while_loop(lambda s: (s[0] >= 1) & s[1], interior, (i - 1, any_weight_left()))

    @pl.when((j_end == 0) & go)
    def _():
        block(0, True)

    for h in range(HEAD_GROUP):
        ot_ref[0, h * HEAD_DIM:(h + 1) * HEAD_DIM, :] = acc_ref[h].astype(BF16)


def _attention(q_t, k, v_t):
    u = _suffix_sum_matrix()
    gw = HEAD_GROUP * HEAD_DIM
    return pl.pallas_call(
        _attn_kernel,
        out_shape=jax.ShapeDtypeStruct((BATCH, D_MODEL, LP), BF16),
        grid=(BATCH, N_HEADS // HEAD_GROUP, N_BLK),
        in_specs=[
            pl.BlockSpec((1, gw, BLK), lambda b, g, i: (b, g, i)),
            pl.BlockSpec((1, LP, gw), lambda b, g, i: (b, 0, g)),
            pl.BlockSpec((1, gw, LP), lambda b, g, i: (b, g, 0)),
            pl.BlockSpec((BLK + SUBLANES, BLK), lambda b, g, i: (0, 0)),
        ],
        out_specs=pl.BlockSpec((1, gw, BLK), lambda b, g, i: (b, g, i)),
        scratch_shapes=[pltpu.VMEM((HEAD_GROUP, PAIR, BLK), BF16),
                        pltpu.VMEM((HEAD_GROUP, HEAD_DIM, BLK), F32),
                        pltpu.VMEM((HEAD_GROUP, SUBLANES, BLK), F32)],
        compiler_params=pltpu.CompilerParams(
            dimension_semantics=("arbitrary", "arbitrary", "arbitrary"), vmem_limit_bytes=VMEM_LIMIT),
        name="stickbreak_attn",
    )(q_t, k, v_t, u)


def _oproj_kernel(h_ref, ot_ref, wo_ref, o_ref):
    j = pl.program_id(1)
    y = lax.dot_general(ot_ref[0], wo_ref[...], _TN_DIMS, preferred_element_type=F32)
    pos = j * TM_POOL + lax.broadcasted_iota(jnp.int32, (TM_POOL, 1), 0) - ROW_PAD
    o_ref[0] = jnp.where(pos >= 0, h_ref[0] + y, 0.0)


def _out_projection(h, o_t, w_o):
    tm = TM_POOL
    row_spec = pl.BlockSpec((1, tm, D_MODEL), lambda b, j: (b, j, 0))
    return pl.pallas_call(
        _oproj_kernel,
        out_shape=jax.ShapeDtypeStruct((BATCH, LP, D_MODEL), F32),
        grid=(BATCH, LP // tm),
        in_specs=[row_spec,
                  pl.BlockSpec((1, D_MODEL, tm), lambda b, j: (b, 0, j)),
                  pl.BlockSpec((D_MODEL, D_MODEL), lambda b, j: (0, 0))],
        out_specs=row_spec,
        compiler_params=pltpu.CompilerParams(dimension_semantics=("arbitrary", "arbitrary")),
        name="out_proj",
    )(h, o_t, w_o)


def kernel(x, meta_tokens, mix_norm, ffn_norm, pool_w, pool_scale, kv_norm, w_kv, w_q, w_o,
           ffn_w_up, ffn_conv_w, ffn_conv_b, ffn_w_down, final_norm):
    row = lambda v: v.reshape(1, -1)
    meta = jnp.broadcast_to(meta_tokens[None].astype(x.dtype), (BATCH, N_META, D_MODEL))
    h = jnp.concatenate([jnp.zeros((BATCH, ROW_PAD, D_MODEL), x.dtype), meta, x], axis=1)

    def ffn(h, layer, final):
        out = _ffn_layer(h.reshape(ROWS, D_MODEL), row(ffn_norm[layer]), ffn_w_up[layer].astype(BF16),
                         ffn_conv_w[layer], row(ffn_conv_b[layer]), ffn_w_down[layer].astype(BF16),
                         row(final_norm), final)
        return out.reshape(BATCH, LP, D_MODEL)

    h = _pool_layer(h, row(mix_norm[0]), pool_w[0].astype(BF16), row(pool_scale[0]))
    h = ffn(h, 0, False)

    w_k, w_v = w_kv[:, :D_MODEL], w_kv[:, D_MODEL:]
    k, v_t, q_t = _projections(h, row(kv_norm), row(mix_norm[1]), w_k.astype(BF16),
                               w_v.T.astype(BF16), (w_q[0].T * (HEAD_DIM ** -0.5)).astype(BF16))
    o_t = _attention(q_t, k, v_t)
    h = _out_projection(h, o_t, w_o[0].astype(BF16))
    h = ffn(h, 1, True)
    return h[:, BLK:]
```

```python
import functools

import numpy as np
import jax
import jax.numpy as jnp
from jax import lax
from jax.experimental import pallas as pl
from jax.experimental.pallas import tpu as pltpu

D_MODEL = 1024
BATCH = 4
SEQ = 4096
N_META = 16
POOL_WINDOWS = (2, 4, 8, 16)
POOL_GROUP_DIM = D_MODEL // len(POOL_WINDOWS)
N_HEADS = 16
HEAD_DIM = D_MODEL // N_HEADS
D_FF = 2816
CONV_WIDTH = 3
RMS_EPS = 1e-6

SUBLANES = 8
LANES = 128
MXU_DIM = 256

BLK = MXU_DIM
ROW_PAD = BLK - N_META
LP = SEQ + BLK
N_BLK = LP // BLK
ROWS = BATCH * LP
MAX_WINDOW = max(POOL_WINDOWS)
HEAD_GROUP = 8
PAIR = 2 * HEAD_DIM
UNDERFLOW_LOG = -104.0
LOG2_E = 1.4426950408889634

TM_POOL = BLK
TM_FFN = 512
FFN_CHUNK = MXU_DIM
N_FFN_CHUNKS = D_FF // FFN_CHUNK
VMEM_LIMIT = 56 * 1024 * 1024

F32 = jnp.float32
BF16 = jnp.bfloat16


def _rms_scale(x):
    return lax.rsqrt(jnp.mean(x * x, axis=-1, keepdims=True) + RMS_EPS)


def _pool_kernel(h_ref, g_ref, w_ref, sc_ref, o_ref, buf_ref):
    j = pl.program_id(1)
    tm = TM_POOL

    @pl.when(j == 0)
    def _():
        buf_ref[0:MAX_WINDOW, :] = jnp.zeros((MAX_WINDOW, D_MODEL), F32)

    x = h_ref[0]
    xn = (x * _rms_scale(x)) * g_ref[...]
    buf_ref[MAX_WINDOW:MAX_WINDOW + tm, :] = xn
    pos = j * tm + lax.broadcasted_iota(jnp.int32, (tm, 1), 0) - ROW_PAD
    ys = []
    for g, w in enumerate(POOL_WINDOWS):
        c0 = g * POOL_GROUP_DIM
        cur = xn[:, c0:c0 + POOL_GROUP_DIM]
        s = cur
        for k in range(1, w):
            s = s + buf_ref[MAX_WINDOW - k:MAX_WINDOW - k + tm, c0:c0 + POOL_GROUP_DIM]
        count = jnp.clip(pos + 1, 1, w).astype(F32)
        diff = s / count - cur
        ys.append(jnp.dot(diff.astype(BF16), w_ref[g], preferred_element_type=F32))
    y = jnp.concatenate(ys, axis=1) * sc_ref[...]
    o_ref[0] = jnp.where(pos >= 0, x + y, 0.0)
    buf_ref[0:MAX_WINDOW, :] = buf_ref[tm:tm + MAX_WINDOW, :]


def _pool_layer(h, gain, w, scale):
    return pl.pallas_call(
        _pool_kernel,
        out_shape=jax.ShapeDtypeStruct((BATCH, LP, D_MODEL), F32),
        grid=(BATCH, LP // TM_POOL),
        in_specs=[
            pl.BlockSpec((1, TM_POOL, D_MODEL), lambda b, j: (b, j, 0)),
            pl.BlockSpec((1, D_MODEL), lambda b, j: (0, 0)),
            pl.BlockSpec((len(POOL_WINDOWS), POOL_GROUP_DIM, POOL_GROUP_DIM), lambda b, j: (0, 0, 0)),
            pl.BlockSpec((1, D_MODEL), lambda b, j: (0, 0)),
        ],
        out_specs=pl.BlockSpec((1, TM_POOL, D_MODEL), lambda b, j: (b, j, 0)),
        scratch_shapes=[pltpu.VMEM((MAX_WINDOW + TM_POOL, D_MODEL), F32)],
        compiler_params=pltpu.CompilerParams(dimension_semantics=("arbitrary", "arbitrary")),
        name="pool_mixer",
    )(h, gain, w, scale)


def _pad_row_mask(row0, tm):
    row = row0 + lax.broadcasted_iota(jnp.int32, (tm, 1), 0)
    is_pad = jnp.zeros((tm, 1), jnp.bool_)
    for b in range(BATCH):
        is_pad = is_pad | ((row >= b * LP) & (row < b * LP + ROW_PAD))
    return jnp.logical_not(is_pad)


def _ffn_kernel(h_ref, g_ref, wup_ref, cw_ref, cb_ref, wdn_ref, fg_ref, o_ref, carry_ref, xn_ref, act_ref,
                *, final_norm):
    i = pl.program_id(0)
    tm = TM_FFN
    cw = 2 * FFN_CHUNK

    @pl.when(i == 0)
    def _():
        carry_ref[...] = jnp.zeros(carry_ref.shape, F32)

    x = h_ref[...]
    xn_ref[...] = ((x * _rms_scale(x)) * g_ref[...]).astype(BF16)

    for c in range(N_FFN_CHUNKS):
        u = jnp.dot(xn_ref[...], wup_ref[:, c * cw:(c + 1) * cw], preferred_element_type=F32)
        ext = jnp.concatenate([carry_ref[c], u], axis=0)
        carry_ref[c] = u[tm - SUBLANES:tm, :]
        w = cw_ref[:, c * cw:(c + 1) * cw]
        cv = cb_ref[:, c * cw:(c + 1) * cw] + w[CONV_WIDTH - 1:CONV_WIDTH] * u
        for k in range(CONV_WIDTH - 1):
            first = SUBLANES - (CONV_WIDTH - 1) + k
            cv = cv + w[k:k + 1] * ext[first:first + tm, :]
        half_gate = 0.5 * cv[:, 0:FFN_CHUNK]
        act = (half_gate + half_gate * jnp.tanh(half_gate)) * cv[:, FFN_CHUNK:cw]
        act_ref[:, c * FFN_CHUNK:(c + 1) * FFN_CHUNK] = act.astype(BF16)

    out = x + jnp.dot(act_ref[...], wdn_ref[...], preferred_element_type=F32)
    if final_norm:
        out = (out * _rms_scale(out)) * fg_ref[...]
    o_ref[...] = jnp.where(_pad_row_mask(i * tm, tm), out, 0.0)


def _ffn_layer(h, gain, w_up, conv_w, conv_b, w_down, final_gain, final_norm):
    const = lambda i: (0, 0)
    resident = functools.partial(pl.BlockSpec, index_map=const, pipeline_mode=pl.Buffered(1))
    return pl.pallas_call(
        functools.partial(_ffn_kernel, final_norm=final_norm),
        out_shape=jax.ShapeDtypeStruct((ROWS, D_MODEL), F32),
        grid=(ROWS // TM_FFN,),
        in_specs=[
            pl.BlockSpec((TM_FFN, D_MODEL), lambda i: (i, 0)),
            pl.BlockSpec((1, D_MODEL), const),
            resident((D_MODEL, 2 * D_FF)),
            pl.BlockSpec((CONV_WIDTH, 2 * D_FF), const),
            pl.BlockSpec((1, 2 * D_FF), const),
            resident((D_FF, D_MODEL)),
            pl.BlockSpec((1, D_MODEL), const),
        ],
        out_specs=pl.BlockSpec((TM_FFN, D_MODEL), lambda i: (i, 0)),
        scratch_shapes=[
            pltpu.VMEM((N_FFN_CHUNKS, SUBLANES, 2 * FFN_CHUNK), F32),
            pltpu.VMEM((TM_FFN, D_MODEL), BF16),
            pltpu.VMEM((TM_FFN, D_FF), BF16),
        ],
        compiler_params=pltpu.CompilerParams(dimension_semantics=("arbitrary",),
                                             vmem_limit_bytes=VMEM_LIMIT),
        name="conv_ffn_final" if final_norm else "conv_ffn",
    )(h, gain, w_up, conv_w, conv_b, w_down, final_gain)


_NT_DIMS = (((1,), (1,)), ((), ()))
_TN_DIMS = (((0,), (0,)), ((), ()))


def _proj_kernel(h_ref, gkv_ref, gq_ref, wk_ref, wvt_ref, wqt_ref, k_ref, vt_ref, qt_ref):
    x = h_ref[0]
    xr = x * _rms_scale(x)
    xk = (xr * gkv_ref[...]).astype(BF16)
    xq = (xr * gq_ref[...]).astype(BF16)
    k_ref[0] = jnp.dot(xk, wk_ref[...], preferred_element_type=F32).astype(BF16)
    vt_ref[0] = lax.dot_general(wvt_ref[...], xk, _NT_DIMS, preferred_element_type=F32).astype(BF16)
    qt_ref[0] = lax.dot_general(wqt_ref[...], xq, _NT_DIMS, preferred_element_type=F32).astype(BF16)


def _projections(h, g_kv, g_q, w_k, w_vt, w_qt):
    const = lambda b, j: (0, 0)
    tm = TM_POOL
    row_spec = pl.BlockSpec((1, tm, D_MODEL), lambda b, j: (b, j, 0))
    col_spec = pl.BlockSpec((1, D_MODEL, tm), lambda b, j: (b, 0, j))
    w_spec = pl.BlockSpec((D_MODEL, D_MODEL), const)
    return pl.pallas_call(
        _proj_kernel,
        out_shape=(jax.ShapeDtypeStruct((BATCH, LP, D_MODEL), BF16),
                   jax.ShapeDtypeStruct((BATCH, D_MODEL, LP), BF16),
                   jax.ShapeDtypeStruct((BATCH, D_MODEL, LP), BF16)),
        grid=(BATCH, LP // tm),
        in_specs=[row_spec, pl.BlockSpec((1, D_MODEL), const), pl.BlockSpec((1, D_MODEL), const),
                  w_spec, w_spec, w_spec],
        out_specs=(row_spec, col_spec, col_spec),
        compiler_params=pltpu.CompilerParams(dimension_semantics=("arbitrary", "arbitrary"),
                                             vmem_limit_bytes=VMEM_LIMIT),
        name="qkv_proj",
    )(h, g_kv, g_q, w_k, w_vt, w_qt)


def _suffix_sum_matrix():
    s = np.arange(BLK + SUBLANES)[:, None]
    j = np.arange(BLK)[None, :]
    return jnp.asarray(np.where(s < BLK, j > s, True), BF16)


def _attn_kernel(qt_ref, k_ref, vt_ref, u_ref, ot_ref, qm_ref, acc_ref, carry_ref):
    i = pl.program_id(2)
    pair_row = lax.broadcasted_iota(jnp.int32, (PAIR, 1), 0)
    for h in range(HEAD_GROUP):
        first = (h % 2) * HEAD_DIM
        mine = (pair_row >= first) & (pair_row < first + HEAD_DIM)
        qp = qt_ref[0, (h // 2) * PAIR:(h // 2 + 1) * PAIR, :]
        qm_ref[h] = jnp.where(mine, qp, jnp.zeros((), BF16))
    acc_ref[...] = jnp.zeros(acc_ref.shape, F32)
    carry_ref[...] = jnp.zeros(carry_ref.shape, F32)

    def block(j, masked):
        start = pl.multiple_of(j * BLK, BLK)
        if masked:
            key = start + lax.broadcasted_iota(jnp.int32, (BLK, BLK), 0)
            qry = i * BLK + lax.broadcasted_iota(jnp.int32, (BLK, BLK), 1)
            valid = (key < qry) & (key >= ROW_PAD)
        heads = range(HEAD_GROUP)
        zs = [jnp.dot(k_ref[0, pl.ds(start, BLK), (h // 2) * PAIR:(h // 2 + 1) * PAIR], qm_ref[h],
                      preferred_element_type=F32) for h in heads]
        log_betas, sums = [], []
        for h in heads:
            z = zs[h]
            soft = jnp.log(1.0 + jnp.exp2(jnp.abs(z) * -LOG2_E))
            log_beta = jnp.minimum(z, 0.0) - soft
            log_1m = log_beta - z
            if masked:
                log_1m = jnp.where(valid, log_1m, 0.0)
            log_betas.append(log_beta)
            sums.append(jnp.dot(u_ref[...], log_1m.astype(BF16), preferred_element_type=F32))
        for h in heads:
            a = jnp.exp(log_betas[h] + sums[h][0:BLK] + carry_ref[h, 0:1, :])
            if masked:
                a = jnp.where(valid, a, 0.0)
            vb = vt_ref[0, h * HEAD_DIM:(h + 1) * HEAD_DIM, pl.ds(start, BLK)]
            acc_ref[h] += jnp.dot(vb, a.astype(BF16), preferred_element_type=F32)
            carry_ref[h] += sums[h][BLK:BLK + SUBLANES]

    def any_weight_left():
        m = carry_ref[0]
        for h in range(1, HEAD_GROUP):
            m = jnp.maximum(m, carry_ref[h])
        return jnp.max(m) >= UNDERFLOW_LOG

    block(i, True)

    def interior(state):
        j, _ = state
        block(j, False)
        return j - 1, any_weight_left()

    j_end, go = lax.while_loop(lambda s: (s[0] >= 1) & s[1], interior, (i - 1, any_weight_left()))

    @pl.when((j_end == 0) & go)
    def _():
        block(0, True)

    for h in range(HEAD_GROUP):
        ot_ref[0, h * HEAD_DIM:(h + 1) * HEAD_DIM, :] = acc_ref[h].astype(BF16)


def _attention(q_t, k, v_t):
    u = _suffix_sum_matrix()
    gw = HEAD_GROUP * HEAD_DIM
    return pl.pallas_call(
        _attn_kernel,
        out_shape=jax.ShapeDtypeStruct((BATCH, D_MODEL, LP), BF16),
        grid=(BATCH, N_HEADS // HEAD_GROUP, N_BLK),
        in_specs=[
            pl.BlockSpec((1, gw, BLK), lambda b, g, i: (b, g, i)),
            pl.BlockSpec((1, LP, gw), lambda b, g, i: (b, 0, g)),
            pl.BlockSpec((1, gw, LP), lambda b, g, i: (b, g, 0)),
            pl.BlockSpec((BLK + SUBLANES, BLK), lambda b, g, i: (0, 0)),
        ],
        out_specs=pl.BlockSpec((1, gw, BLK), lambda b, g, i: (b, g, i)),
        scratch_shapes=[pltpu.VMEM((HEAD_GROUP, PAIR, BLK), BF16),
                        pltpu.VMEM((HEAD_GROUP, HEAD_DIM, BLK), F32),
                        pltpu.VMEM((HEAD_GROUP, SUBLANES, BLK), F32)],
        compiler_params=pltpu.CompilerParams(
            dimension_semantics=("arbitrary", "arbitrary", "arbitrary"), vmem_limit_bytes=VMEM_LIMIT),
        name="stickbreak_attn",
    )(q_t, k, v_t, u)


def _oproj_kernel(h_ref, ot_ref, wo_ref, o_ref):
    j = pl.program_id(1)
    y = lax.dot_general(ot_ref[0], wo_ref[...], _TN_DIMS, preferred_element_type=F32)
    pos = j * TM_POOL + lax.broadcasted_iota(jnp.int32, (TM_POOL, 1), 0) - ROW_PAD
    o_ref[0] = jnp.where(pos >= 0, h_ref[0] + y, 0.0)


def _out_projection(h, o_t, w_o):
    tm = TM_POOL
    row_spec = pl.BlockSpec((1, tm, D_MODEL), lambda b, j: (b, j, 0))
    return pl.pallas_call(
        _oproj_kernel,
        out_shape=jax.ShapeDtypeStruct((BATCH, LP, D_MODEL), F32),
        grid=(BATCH, LP // tm),
        in_specs=[row_spec,
                  pl.BlockSpec((1, D_MODEL, tm), lambda b, j: (b, 0, j)),
                  pl.BlockSpec((D_MODEL, D_MODEL), lambda b, j: (0, 0))],
        out_specs=row_spec,
        compiler_params=pltpu.CompilerParams(dimension_semantics=("arbitrary", "arbitrary")),
        name="out_proj",
    )(h, o_t, w_o)


def kernel(x, meta_tokens, mix_norm, ffn_norm, pool_w, pool_scale, kv_norm, w_kv, w_q, w_o,
           ffn_w_up, ffn_conv_w, ffn_conv_b, ffn_w_down, final_norm):
    row = lambda v: v.reshape(1, -1)
    meta = jnp.broadcast_to(meta_tokens[None].astype(x.dtype), (BATCH, N_META, D_MODEL))
    h = jnp.concatenate([jnp.zeros((BATCH, ROW_PAD, D_MODEL), x.dtype), meta, x], axis=1)

    def by_chunk(a):
        lead = a.shape[:-1]
        a = a.reshape(*lead, 2, N_FFN_CHUNKS, FFN_CHUNK)
        return jnp.swapaxes(a, -3, -2).reshape(*lead, 2 * D_FF)

    def ffn(h, layer, final):
        out = _ffn_layer(h.reshape(ROWS, D_MODEL), row(ffn_norm[layer]),
                         by_chunk(ffn_w_up[layer]).astype(BF16), by_chunk(ffn_conv_w[layer]),
                         by_chunk(row(ffn_conv_b[layer])), ffn_w_down[layer].astype(BF16),
                         row(final_norm), final)
        return out.reshape(BATCH, LP, D_MODEL)

    h = _pool_layer(h, row(mix_norm[0]), pool_w[0].astype(BF16), row(pool_scale[0]))
    h = ffn(h, 0, False)

    w_k, w_v = w_kv[:, :D_MODEL], w_kv[:, D_MODEL:]
    k, v_t, q_t = _projections(h, row(kv_norm), row(mix_norm[1]), w_k.astype(BF16),
                               w_v.T.astype(BF16), (w_q[0].T * (HEAD_DIM ** -0.5)).astype(BF16))
    o_t = _attention(q_t, k, v_t)
    h = _out_projection(h, o_t, w_o[0].astype(BF16))
    h = ffn(h, 1, True)
    return h[:, BLK:]
```

```python
import functools

import numpy as np
import jax
import jax.numpy as jnp
from jax import lax
from jax.experimental import pallas as pl
from jax.experimental.pallas import tpu as pltpu

D_MODEL = 1024
BATCH = 4
SEQ = 4096
N_META = 16
POOL_WINDOWS = (2, 4, 8, 16)
POOL_GROUP_DIM = D_MODEL // len(POOL_WINDOWS)
N_HEADS = 16
HEAD_DIM = D_MODEL // N_HEADS
D_FF = 2816
CONV_WIDTH = 3
RMS_EPS = 1e-6

SUBLANES = 8
LANES = 128
MXU_DIM = 256

BLK = MXU_DIM
ROW_PAD = BLK - N_META
LP = SEQ + BLK
N_BLK = LP // BLK
ROWS = BATCH * LP
MAX_WINDOW = max(POOL_WINDOWS)
HEAD_GROUP = 8
PAIR = 2 * HEAD_DIM
UNDERFLOW_LOG = -104.0
LOG2_E = 1.4426950408889634

TM_POOL = BLK
TM_FFN = 512
FFN_CHUNK = MXU_DIM
N_FFN_CHUNKS = D_FF // FFN_CHUNK
VMEM_LIMIT = 56 * 1024 * 1024

F32 = jnp.float32
BF16 = jnp.bfloat16


def _rms_scale(x):
    return lax.rsqrt(jnp.mean(x * x, axis=-1, keepdims=True) + RMS_EPS)


def _pool_kernel(x_ref, meta_ref, g_ref, w_ref, sc_ref, o_ref, buf_ref):
    j = pl.program_id(1)
    tm = TM_POOL

    @pl.when(j == 0)
    def _():
        buf_ref[0:MAX_WINDOW, :] = jnp.zeros((MAX_WINDOW, D_MODEL), F32)

    head_tile = jnp.concatenate([jnp.zeros((ROW_PAD, D_MODEL), F32), meta_ref[...]], axis=0)
    x = jnp.where(j == 0, head_tile, x_ref[0])
    xn = (x * _rms_scale(x)) * g_ref[...]
    buf_ref[MAX_WINDOW:MAX_WINDOW + tm, :] = xn
    pos = j * tm + lax.broadcasted_iota(jnp.int32, (tm, 1), 0) - ROW_PAD
    ys = []
    for g, w in enumerate(POOL_WINDOWS):
        c0 = g * POOL_GROUP_DIM
        cur = xn[:, c0:c0 + POOL_GROUP_DIM]
        s = cur
        for k in range(1, w):
            s = s + buf_ref[MAX_WINDOW - k:MAX_WINDOW - k + tm, c0:c0 + POOL_GROUP_DIM]
        count = jnp.clip(pos + 1, 1, w).astype(F32)
        diff = s / count - cur
        ys.append(jnp.dot(diff.astype(BF16), w_ref[g], preferred_element_type=F32))
    y = jnp.concatenate(ys, axis=1) * sc_ref[...]
    o_ref[0] = jnp.where(pos >= 0, x + y, 0.0)
    buf_ref[0:MAX_WINDOW, :] = buf_ref[tm:tm + MAX_WINDOW, :]


def _pool_layer(x, meta, gain, w, scale):
    return pl.pallas_call(
        _pool_kernel,
        out_shape=jax.ShapeDtypeStruct((BATCH, LP, D_MODEL), F32),
        grid=(BATCH, LP // TM_POOL),
        in_specs=[
            pl.BlockSpec((1, TM_POOL, D_MODEL), lambda b, j: (b, jnp.maximum(j - 1, 0), 0)),
            pl.BlockSpec((N_META, D_MODEL), lambda b, j: (0, 0)),
            pl.BlockSpec((1, D_MODEL), lambda b, j: (0, 0)),
            pl.BlockSpec((len(POOL_WINDOWS), POOL_GROUP_DIM, POOL_GROUP_DIM), lambda b, j: (0, 0, 0)),
            pl.BlockSpec((1, D_MODEL), lambda b, j: (0, 0)),
        ],
        out_specs=pl.BlockSpec((1, TM_POOL, D_MODEL), lambda b, j: (b, j, 0)),
        scratch_shapes=[pltpu.VMEM((MAX_WINDOW + TM_POOL, D_MODEL), F32)],
        compiler_params=pltpu.CompilerParams(dimension_semantics=("arbitrary", "arbitrary")),
        name="pool_mixer",
    )(x, meta, gain, w, scale)


def _pad_row_mask(row0, tm):
    row = row0 + lax.broadcasted_iota(jnp.int32, (tm, 1), 0)
    is_pad = jnp.zeros((tm, 1), jnp.bool_)
    for b in range(BATCH):
        is_pad = is_pad | ((row >= b * LP) & (row < b * LP + ROW_PAD))
    return jnp.logical_not(is_pad)


def _ffn_kernel(*refs, attn_tail):
    if attn_tail:
        h_ref, attn_ref, wo_ref, fg_ref, *refs = refs
    else:
        h_ref, *refs = refs
    g_ref, wup_ref, cw_ref, cb_ref, wdn_ref, o_ref, carry_ref, xn_ref, act_ref = refs
    i = pl.program_id(0)
    tm = TM_FFN
    cw = 2 * FFN_CHUNK

    @pl.when(i == 0)
    def _():
        carry_ref[...] = jnp.zeros(carry_ref.shape, F32)

    x = h_ref[...]
    if attn_tail:
        x = x + jnp.dot(attn_ref[...], wo_ref[...], preferred_element_type=F32)
    xn_ref[...] = ((x * _rms_scale(x)) * g_ref[...]).astype(BF16)

    def chunk_cols(ref, c):
        g0 = c * FFN_CHUNK
        return jnp.concatenate([ref[:, g0:g0 + FFN_CHUNK], ref[:, D_FF + g0:D_FF + g0 + FFN_CHUNK]], axis=1)

    for c in range(N_FFN_CHUNKS):
        u = jnp.dot(xn_ref[...], chunk_cols(wup_ref, c), preferred_element_type=F32)
        ext = jnp.concatenate([carry_ref[c], u], axis=0)
        carry_ref[c] = u[tm - SUBLANES:tm, :]
        w = chunk_cols(cw_ref, c)
        cv = chunk_cols(cb_ref, c) + w[CONV_WIDTH - 1:CONV_WIDTH] * u
        for k in range(CONV_WIDTH - 1):
            first = SUBLANES - (CONV_WIDTH - 1) + k
            cv = cv + w[k:k + 1] * ext[first:first + tm, :]
        half_gate = 0.5 * cv[:, 0:FFN_CHUNK]
        act = (half_gate + half_gate * jnp.tanh(half_gate)) * cv[:, FFN_CHUNK:cw]
        act_ref[:, c * FFN_CHUNK:(c + 1) * FFN_CHUNK] = act.astype(BF16)

    out = x + jnp.dot(act_ref[...], wdn_ref[...], preferred_element_type=F32)
    if attn_tail:
        out = (out * _rms_scale(out)) * fg_ref[...]
    o_ref[...] = jnp.where(_pad_row_mask(i * tm, tm), out, 0.0)


def _ffn_layer(h, gain, w_up, conv_w, conv_b, w_down, attn_tail=None):
    const = lambda i: (0, 0)
    resident = functools.partial(pl.BlockSpec, index_map=const, pipeline_mode=pl.Buffered(1))
    row_tile = pl.BlockSpec((TM_FFN, D_MODEL), lambda i: (i, 0))
    operands, in_specs = [h], [row_tile]
    if attn_tail is not None:
        operands += list(attn_tail)
        in_specs += [row_tile, resident((D_MODEL, D_MODEL)), pl.BlockSpec((1, D_MODEL), const)]
    operands += [gain, w_up, conv_w, conv_b, w_down]
    in_specs += [
        pl.BlockSpec((1, D_MODEL), const),
        resident((D_MODEL, 2 * D_FF)),
        pl.BlockSpec((CONV_WIDTH, 2 * D_FF), const),
        pl.BlockSpec((1, 2 * D_FF), const),
        resident((D_FF, D_MODEL)),
    ]
    return pl.pallas_call(
        functools.partial(_ffn_kernel, attn_tail=attn_tail is not None),
        out_shape=jax.ShapeDtypeStruct((ROWS, D_MODEL), F32),
        grid=(ROWS // TM_FFN,),
        in_specs=in_specs,
        out_specs=row_tile,
        scratch_shapes=[
            pltpu.VMEM((N_FFN_CHUNKS, SUBLANES, 2 * FFN_CHUNK), F32),
            pltpu.VMEM((TM_FFN, D_MODEL), BF16),
            pltpu.VMEM((TM_FFN, D_FF), BF16),
        ],
        compiler_params=pltpu.CompilerParams(dimension_semantics=("arbitrary",),
                                             vmem_limit_bytes=VMEM_LIMIT),
        name="conv_ffn" if attn_tail is None else "attn_out_conv_ffn_norm",
    )(*operands)


_NT_DIMS = (((1,), (1,)), ((), ()))
_TN_DIMS = (((0,), (0,)), ((), ()))


def _proj_kernel(h_ref, gkv_ref, gq_ref, wk_ref, wvt_ref, wqt_ref, k_ref, vt_ref, qt_ref):
    x = h_ref[0]
    xr = x * _rms_scale(x)
    xk = (xr * gkv_ref[...]).astype(BF16)
    xq = (xr * gq_ref[...]).astype(BF16)
    k_ref[0] = jnp.dot(xk, wk_ref[...], preferred_element_type=F32).astype(BF16)
    vt_ref[0] = lax.dot_general(wvt_ref[...], xk, _NT_DIMS, preferred_element_type=F32).astype(BF16)
    qt_ref[0] = lax.dot_general(wqt_ref[...], xq, _NT_DIMS, preferred_element_type=F32).astype(BF16)


def _projections(h, g_kv, g_q, w_k, w_vt, w_qt):
    const = lambda b, j: (0, 0)
    tm = TM_POOL
    row_spec = pl.BlockSpec((1, tm, D_MODEL), lambda b, j: (b, j, 0))
    col_spec = pl.BlockSpec((1, D_MODEL, tm), lambda b, j: (b, 0, j))
    w_spec = pl.BlockSpec((D_MODEL, D_MODEL), const)
    return pl.pallas_call(
        _proj_kernel,
        out_shape=(jax.ShapeDtypeStruct((BATCH, LP, D_MODEL), BF16),
                   jax.ShapeDtypeStruct((BATCH, D_MODEL, LP), BF16),
                   jax.ShapeDtypeStruct((BATCH, D_MODEL, LP), BF16)),
        grid=(BATCH, LP // tm),
        in_specs=[row_spec, pl.BlockSpec((1, D_MODEL), const), pl.BlockSpec((1, D_MODEL), const),
                  w_spec, w_spec, w_spec],
        out_specs=(row_spec, col_spec, col_spec),
        compiler_params=pltpu.CompilerParams(dimension_semantics=("arbitrary", "arbitrary"),
                                             vmem_limit_bytes=VMEM_LIMIT),
        name="qkv_proj",
    )(h, g_kv, g_q, w_k, w_vt, w_qt)


def _suffix_sum_matrix():
    s = np.arange(BLK + SUBLANES)[:, None]
    j = np.arange(BLK)[None, :]
    return jnp.asarray(np.where(s < BLK, j > s, True), BF16)


def _attn_kernel(qt_ref, k_ref, vt_ref, u_ref, o_ref, qm_ref, acc_ref, carry_ref):
    i = pl.program_id(2)
    pair_row = lax.broadcasted_iota(jnp.int32, (PAIR, 1), 0)
    for h in range(HEAD_GROUP):
        first = (h % 2) * HEAD_DIM
        mine = (pair_row >= first) & (pair_row < first + HEAD_DIM)
        qp = qt_ref[0, (h // 2) * PAIR:(h // 2 + 1) * PAIR, :]
        qm_ref[h] = jnp.where(mine, qp, jnp.zeros((), BF16))
    acc_ref[...] = jnp.zeros(acc_ref.shape, F32)
    carry_ref[...] = jnp.zeros(carry_ref.shape, F32)

    def block(j, masked):
        start = pl.multiple_of(j * BLK, BLK)
        if masked:
            key = start + lax.broadcasted_iota(jnp.int32, (BLK, BLK), 0)
            qry = i * BLK + lax.broadcasted_iota(jnp.int32, (BLK, BLK), 1)
            valid = (key < qry) & (key >= ROW_PAD)
        heads = range(HEAD_GROUP)
        zs = [jnp.dot(k_ref[0, pl.ds(start, BLK), (h // 2) * PAIR:(h // 2 + 1) * PAIR], qm_ref[h],
                      preferred_element_type=F32) for h in heads]
        log_betas, sums = [], []
        for h in heads:
            z = zs[h]
            soft = jnp.log(1.0 + jnp.exp2(jnp.abs(z) * -LOG2_E))
            log_beta = jnp.minimum(z, 0.0) - soft
            log_1m = log_beta - z
            if masked:
                log_1m = jnp.where(valid, log_1m, 0.0)
            log_betas.append(log_beta)
            sums.append(jnp.dot(u_ref[...], log_1m.astype(BF16), preferred_element_type=F32))
        for h in heads:
            a = jnp.exp(log_betas[h] + sums[h][0:BLK] + carry_ref[h, 0:1, :])
            if masked:
                a = jnp.where(valid, a, 0.0)
            vb = vt_ref[0, h * HEAD_DIM:(h + 1) * HEAD_DIM, pl.ds(start, BLK)]
            rows = slice((h % 2) * HEAD_DIM, (h % 2 + 1) * HEAD_DIM)
            acc_ref[h // 2, rows, :] += jnp.dot(vb, a.astype(BF16), preferred_element_type=F32)
            carry_ref[h] += sums[h][BLK:BLK + SUBLANES]

    def any_weight_left():
        m = carry_ref[0]
        for h in range(1, HEAD_GROUP):
            m = jnp.maximum(m, carry_ref[h])
        return jnp.max(m) >= UNDERFLOW_LOG

    block(i, True)

    def interior(state):
        j, _ = state
        block(j, False)
        return j - 1, any_weight_left()

    j_end, go = lax.while_loop(lambda s: (s[0] >= 1) & s[1], interior, (i - 1, any_weight_left()))

    @pl.when((j_end == 0) & go)
    def _():
        block(0, True)

    for p in range(HEAD_GROUP // 2):
        o_ref[0, :, p * PAIR:(p + 1) * PAIR] = acc_ref[p].T.astype(BF16)


def _attention(q_t, k, v_t):
    u = _suffix_sum_matrix()
    gw = HEAD_GROUP * HEAD_DIM
    return pl.pallas_call(
        _attn_kernel,
        out_shape=jax.ShapeDtypeStruct((BATCH, LP, D_MODEL), BF16),
        grid=(BATCH, N_HEADS // HEAD_GROUP, N_BLK),
        in_specs=[
            pl.BlockSpec((1, gw, BLK), lambda b, g, i: (b, g, i)),
            pl.BlockSpec((1, LP, gw), lambda b, g, i: (b, 0, g)),
            pl.BlockSpec((1, gw, LP), lambda b, g, i: (b, g, 0)),
            pl.BlockSpec((BLK + SUBLANES, BLK), lambda b, g, i: (0, 0)),
        ],
        out_specs=pl.BlockSpec((1, BLK, gw), lambda b, g, i: (b, i, g)),
        scratch_shapes=[pltpu.VMEM((HEAD_GROUP, PAIR, BLK), BF16),
                        pltpu.VMEM((HEAD_GROUP // 2, PAIR, BLK), F32),
                        pltpu.VMEM((HEAD_GROUP, SUBLANES, BLK), F32)],
        compiler_params=pltpu.CompilerParams(
            dimension_semantics=("arbitrary", "arbitrary", "arbitrary"), vmem_limit_bytes=VMEM_LIMIT),
        name="stickbreak_attn",
    )(q_t, k, v_t, u)


def kernel(x, meta_tokens, mix_norm, ffn_norm, pool_w, pool_scale, kv_norm, w_kv, w_q, w_o,
           ffn_w_up, ffn_conv_w, ffn_conv_b, ffn_w_down, final_norm):
    row = lambda v: v.reshape(1, -1)

    def ffn(h, layer, attn_tail=None):
        out = _ffn_layer(h.reshape(ROWS, D_MODEL), row(ffn_norm[layer]), ffn_w_up[layer].astype(BF16),
                         ffn_conv_w[layer], row(ffn_conv_b[layer]), ffn_w_down[layer].astype(BF16),
                         attn_tail)
        return out.reshape(BATCH, LP, D_MODEL)

    h = _pool_layer(x, meta_tokens.astype(x.dtype), row(mix_norm[0]), pool_w[0].astype(BF16),
                    row(pool_scale[0]))
    h = ffn(h, 0)

    w_k, w_v = w_kv[:, :D_MODEL], w_kv[:, D_MODEL:]
    k, v_t, q_t = _projections(h, row(kv_norm), row(mix_norm[1]), w_k.astype(BF16),
                               w_v.T.astype(BF16), (w_q[0].T * (HEAD_DIM ** -0.5)).astype(BF16))
    attn = _attention(q_t, k, v_t)
    h = ffn(h, 1, (attn.reshape(ROWS, D_MODEL), w_o[0].astype(BF16), row(final_norm)))
    return h[:, BLK:]
```

```python
import functools

import numpy as np
import jax
import jax.numpy as jnp
from jax import lax
from jax.experimental import pallas as pl
from jax.experimental.pallas import tpu as pltpu

D_MODEL = 1024
BATCH = 4
SEQ = 4096
N_META = 16
POOL_WINDOWS = (2, 4, 8, 16)
POOL_GROUP_DIM = D_MODEL // len(POOL_WINDOWS)
N_HEADS = 16
HEAD_DIM = D_MODEL // N_HEADS
D_FF = 2816
CONV_WIDTH = 3
RMS_EPS = 1e-6

SUBLANES = 8
LANES = 128
MXU_DIM = 256

BLK = MXU_DIM
ROW_PAD = BLK - N_META
LP = SEQ + BLK
N_BLK = LP // BLK
ROWS = BATCH * LP
MAX_WINDOW = max(POOL_WINDOWS)
HEAD_GROUP = 8
PAIR = 2 * HEAD_DIM
UNDERFLOW_LOG = -104.0
LOG2_E = 1.4426950408889634

TM_POOL = BLK
TM_FFN = 512
FFN_CHUNK = MXU_DIM
N_FFN_CHUNKS = D_FF // FFN_CHUNK
VMEM_LIMIT = 56 * 1024 * 1024

F32 = jnp.float32
BF16 = jnp.bfloat16


def _rms_scale(x):
    return lax.rsqrt(jnp.mean(x * x, axis=-1, keepdims=True) + RMS_EPS)


def _pool_kernel(x_ref, meta_ref, g_ref, w_ref, sc_ref, o_ref, buf_ref):
    j = pl.program_id(1)
    tm = TM_POOL

    @pl.when(j == 0)
    def _():
        buf_ref[0:MAX_WINDOW, :] = jnp.zeros((MAX_WINDOW, D_MODEL), F32)

    head_tile = jnp.concatenate([jnp.zeros((ROW_PAD, D_MODEL), F32), meta_ref[...]], axis=0)
    x = jnp.where(j == 0, head_tile, x_ref[0])
    xn = (x * _rms_scale(x)) * g_ref[...]
    buf_ref[MAX_WINDOW:MAX_WINDOW + tm, :] = xn
    pos = j * tm + lax.broadcasted_iota(jnp.int32, (tm, 1), 0) - ROW_PAD
    sums = []
    win = buf_ref[...]
    shift = 1
    for g, w in enumerate(POOL_WINDOWS):
        assert w == 2 * shift
        win = win[:, (POOL_GROUP_DIM if g else 0):]
        win = win + pltpu.roll(win, shift, axis=0)
        sums.append(win[MAX_WINDOW:, 0:POOL_GROUP_DIM])
        shift = w
    ys = []
    for g, w in enumerate(POOL_WINDOWS):
        c0 = g * POOL_GROUP_DIM
        count = jnp.clip(pos + 1, 1, w).astype(F32)
        diff = sums[g] / count - xn[:, c0:c0 + POOL_GROUP_DIM]
        ys.append(jnp.dot(diff.astype(BF16), w_ref[g], preferred_element_type=F32))
    y = jnp.concatenate(ys, axis=1) * sc_ref[...]
    o_ref[0] = jnp.where(pos >= 0, x + y, 0.0)
    buf_ref[0:MAX_WINDOW, :] = buf_ref[tm:tm + MAX_WINDOW, :]


def _pool_layer(x, meta, gain, w, scale):
    return pl.pallas_call(
        _pool_kernel,
        out_shape=jax.ShapeDtypeStruct((BATCH, LP, D_MODEL), F32),
        grid=(BATCH, LP // TM_POOL),
        in_specs=[
            pl.BlockSpec((1, TM_POOL, D_MODEL), lambda b, j: (b, jnp.maximum(j - 1, 0), 0)),
            pl.BlockSpec((N_META, D_MODEL), lambda b, j: (0, 0)),
            pl.BlockSpec((1, D_MODEL), lambda b, j: (0, 0)),
            pl.BlockSpec((len(POOL_WINDOWS), POOL_GROUP_DIM, POOL_GROUP_DIM), lambda b, j: (0, 0, 0)),
            pl.BlockSpec((1, D_MODEL), lambda b, j: (0, 0)),
        ],
        out_specs=pl.BlockSpec((1, TM_POOL, D_MODEL), lambda b, j: (b, j, 0)),
        scratch_shapes=[pltpu.VMEM((MAX_WINDOW + TM_POOL, D_MODEL), F32)],
        compiler_params=pltpu.CompilerParams(dimension_semantics=("arbitrary", "arbitrary")),
        name="pool_mixer",
    )(x, meta, gain, w, scale)


def _pad_row_mask(row0, tm):
    row = row0 + lax.broadcasted_iota(jnp.int32, (tm, 1), 0)
    is_pad = jnp.zeros((tm, 1), jnp.bool_)
    for b in range(BATCH):
        is_pad = is_pad | ((row >= b * LP) & (row < b * LP + ROW_PAD))
    return jnp.logical_not(is_pad)


def _ffn_kernel(*refs, attn_tail):
    if attn_tail:
        h_ref, attn_ref, wo_ref, fg_ref, *refs = refs
    else:
        h_ref, *refs = refs
    g_ref, wup_ref, cw_ref, cb_ref, wdn_ref, o_ref, carry_ref, xn_ref, act_ref = refs
    i = pl.program_id(0)
    tm = TM_FFN
    cw = 2 * FFN_CHUNK

    @pl.when(i == 0)
    def _():
        carry_ref[...] = jnp.zeros(carry_ref.shape, F32)

    x = h_ref[...]
    if attn_tail:
        x = x + jnp.dot(attn_ref[...], wo_ref[...], preferred_element_type=F32)
    xn_ref[...] = ((x * _rms_scale(x)) * g_ref[...]).astype(BF16)

    def chunk_cols(ref, c):
        g0 = c * FFN_CHUNK
        return jnp.concatenate([ref[:, g0:g0 + FFN_CHUNK], ref[:, D_FF + g0:D_FF + g0 + FFN_CHUNK]], axis=1)

    for c in range(N_FFN_CHUNKS):
        u = jnp.dot(xn_ref[...], chunk_cols(wup_ref, c), preferred_element_type=F32)
        ext = jnp.concatenate([carry_ref[c], u], axis=0)
        carry_ref[c] = u[tm - SUBLANES:tm, :]
        w = chunk_cols(cw_ref, c)
        cv = chunk_cols(cb_ref, c) + w[CONV_WIDTH - 1:CONV_WIDTH] * u
        for k in range(CONV_WIDTH - 1):
            first = SUBLANES - (CONV_WIDTH - 1) + k
            cv = cv + w[k:k + 1] * ext[first:first + tm, :]
        half_gate = 0.5 * cv[:, 0:FFN_CHUNK]
        act = (half_gate + half_gate * jnp.tanh(half_gate)) * cv[:, FFN_CHUNK:cw]
        act_ref[:, c * FFN_CHUNK:(c + 1) * FFN_CHUNK] = act.astype(BF16)

    out = x + jnp.dot(act_ref[...], wdn_ref[...], preferred_element_type=F32)
    if attn_tail:
        out = (out * _rms_scale(out)) * fg_ref[...]
    o_ref[...] = jnp.where(_pad_row_mask(i * tm, tm), out, 0.0)


def _ffn_layer(h, layer, gain, w_up, conv_w, conv_b, w_down, attn_tail=None):
    const = lambda i: (0, 0)
    of_layer = lambda i: (layer, 0, 0)
    row_tile = pl.BlockSpec((TM_FFN, D_MODEL), lambda i: (i, 0))
    operands, in_specs = [h], [row_tile]
    if attn_tail is not None:
        operands += list(attn_tail)
        in_specs += [row_tile,
                     pl.BlockSpec((D_MODEL, D_MODEL), const, pipeline_mode=pl.Buffered(1)),
                     pl.BlockSpec((1, D_MODEL), const)]
    operands += [gain, w_up, conv_w, conv_b, w_down]
    in_specs += [
        pl.BlockSpec((None, 1, D_MODEL), of_layer),
        pl.BlockSpec((None, D_MODEL, 2 * D_FF), of_layer, pipeline_mode=pl.Buffered(1)),
        pl.BlockSpec((None, CONV_WIDTH, 2 * D_FF), of_layer),
        pl.BlockSpec((None, 1, 2 * D_FF), of_layer),
        pl.BlockSpec((None, D_FF, D_MODEL), of_layer, pipeline_mode=pl.Buffered(1)),
    ]
    return pl.pallas_call(
        functools.partial(_ffn_kernel, attn_tail=attn_tail is not None),
        out_shape=jax.ShapeDtypeStruct((ROWS, D_MODEL), F32),
        grid=(ROWS // TM_FFN,),
        in_specs=in_specs,
        out_specs=row_tile,
        scratch_shapes=[
            pltpu.VMEM((N_FFN_CHUNKS, SUBLANES, 2 * FFN_CHUNK), F32),
            pltpu.VMEM((TM_FFN, D_MODEL), BF16),
            pltpu.VMEM((TM_FFN, D_FF), BF16),
        ],
        compiler_params=pltpu.CompilerParams(dimension_semantics=("arbitrary",),
                                             vmem_limit_bytes=VMEM_LIMIT),
        name="conv_ffn" if attn_tail is None else "attn_out_conv_ffn_norm",
    )(*operands)


_NT_DIMS = (((1,), (1,)), ((), ()))


def _proj_kernel(h_ref, gkv_ref, gq_ref, wk_ref, wvt_ref, wqt_ref, k_ref, vt_ref, qt_ref):
    x = h_ref[0]
    xr = x * _rms_scale(x)
    xk = (xr * gkv_ref[...]).astype(BF16)
    xq = (xr * gq_ref[...]).astype(BF16)
    k_ref[0] = jnp.dot(xk, wk_ref[...], preferred_element_type=F32).astype(BF16)
    vt_ref[0] = lax.dot_general(wvt_ref[...], xk, _NT_DIMS, preferred_element_type=F32).astype(BF16)
    qt_ref[0] = lax.dot_general(wqt_ref[...], xq, _NT_DIMS, preferred_element_type=F32).astype(BF16)


def _projections(h, g_kv, g_q, w_k, w_vt, w_qt):
    const = lambda b, j: (0, 0)
    tm = TM_POOL
    row_spec = pl.BlockSpec((1, tm, D_MODEL), lambda b, j: (b, j, 0))
    col_spec = pl.BlockSpec((1, D_MODEL, tm), lambda b, j: (b, 0, j))
    w_spec = pl.BlockSpec((D_MODEL, D_MODEL), const)
    return pl.pallas_call(
        _proj_kernel,
        out_shape=(jax.ShapeDtypeStruct((BATCH, LP, D_MODEL), BF16),
                   jax.ShapeDtypeStruct((BATCH, D_MODEL, LP), BF16),
                   jax.ShapeDtypeStruct((BATCH, D_MODEL, LP), BF16)),
        grid=(BATCH, LP // tm),
        in_specs=[row_spec, pl.BlockSpec((1, D_MODEL), const), pl.BlockSpec((1, D_MODEL), const),
                  w_spec, w_spec, w_spec],
        out_specs=(row_spec, col_spec, col_spec),
        compiler_params=pltpu.CompilerParams(dimension_semantics=("arbitrary", "arbitrary"),
                                             vmem_limit_bytes=VMEM_LIMIT),
        name="qkv_proj",
    )(h, g_kv, g_q, w_k, w_vt, w_qt)


def _suffix_sum_matrix():
    s = np.arange(BLK + SUBLANES)[:, None]
    j = np.arange(BLK)[None, :]
    return jnp.asarray(np.where(s < BLK, j > s, True), BF16)


def _attn_kernel(qt_ref, k_ref, vt_ref, u_ref, o_ref, qm_ref, acc_ref, carry_ref):
    i = pl.program_id(2)
    pair_row = lax.broadcasted_iota(jnp.int32, (PAIR, 1), 0)
    for h in range(HEAD_GROUP):
        first = (h % 2) * HEAD_DIM
        mine = (pair_row >= first) & (pair_row < first + HEAD_DIM)
        qp = qt_ref[0, (h // 2) * PAIR:(h // 2 + 1) * PAIR, :]
        qm_ref[h] = jnp.where(mine, qp, jnp.zeros((), BF16))
    acc_ref[...] = jnp.zeros(acc_ref.shape, F32)
    carry_ref[...] = jnp.zeros(carry_ref.shape, F32)

    def sweep(blocks):
        row = lax.broadcasted_iota(jnp.int32, (BLK, BLK), 0)
        col = lax.broadcasted_iota(jnp.int32, (BLK, BLK), 1)
        valid = {None: None}
        for j, kind in blocks:
            if kind == "causal":
                valid[kind] = row < col
            elif kind == "edge":
                key = j * BLK + row
                valid[kind] = (key < i * BLK + col) & (key >= ROW_PAD)
        chains = [(pl.multiple_of(j * BLK, BLK), kind, h) for j, kind in blocks for h in range(HEAD_GROUP)]
        zs = [jnp.dot(k_ref[0, pl.ds(start, BLK), (h // 2) * PAIR:(h // 2 + 1) * PAIR], qm_ref[h],
                      preferred_element_type=F32) for start, _, h in chains]
        log_betas, sums = [], []
        for (_, kind, h), z in zip(chains, zs):
            soft = jnp.log(1.0 + jnp.exp2(jnp.abs(z) * -LOG2_E))
            log_beta = jnp.minimum(z, 0.0) - soft
            log_1m = log_beta - z
            if kind is not None:
                log_1m = jnp.where(valid[kind], log_1m, 0.0)
            log_betas.append(log_beta)
            sums.append(jnp.dot(u_ref[...], log_1m.astype(BF16), preferred_element_type=F32))
        for (start, kind, h), log_beta, s in zip(chains, log_betas, sums):
            a = jnp.exp(log_beta + s[0:BLK] + carry_ref[h, 0:1, :])
            if kind is not None:
                a = jnp.where(valid[kind], a, 0.0)
            vb = vt_ref[0, h * HEAD_DIM:(h + 1) * HEAD_DIM, pl.ds(start, BLK)]
            rows = slice((h % 2) * HEAD_DIM, (h % 2 + 1) * HEAD_DIM)
            acc_ref[h // 2, rows, :] += jnp.dot(vb, a.astype(BF16), preferred_element_type=F32)
            carry_ref[h] += s[BLK:BLK + SUBLANES]

    def any_weight_left():
        m = carry_ref[0]
        for h in range(1, HEAD_GROUP):
            m = jnp.maximum(m, carry_ref[h])
        return jnp.max(m) >= UNDERFLOW_LOG

    @pl.when(i >= 2)
    def _():
        sweep([(i, "causal"), (i - 1, None)])

    @pl.when(i < 2)
    def _():
        sweep([(i, "edge")])

    def interior(state):
        j, _ = state
        sweep([(j, None)])
        return j - 1, any_weight_left()

    j_first = jnp.where(i >= 2, i - 2, i - 1)
    j_end, go = lax.while_loop(lambda s: (s[0] >= 1) & s[1], interior, (j_first, any_weight_left()))

    @pl.when((j_end == 0) & go)
    def _():
        sweep([(0, "edge")])

    for p in range(HEAD_GROUP // 2):
        o_ref[0, :, p * PAIR:(p + 1) * PAIR] = acc_ref[p].T.astype(BF16)


def _attention(q_t, k, v_t):
    u = _suffix_sum_matrix()
    gw = HEAD_GROUP * HEAD_DIM
    return pl.pallas_call(
        _attn_kernel,
        out_shape=jax.ShapeDtypeStruct((BATCH, LP, D_MODEL), BF16),
        grid=(BATCH, N_HEADS // HEAD_GROUP, N_BLK),
        in_specs=[
            pl.BlockSpec((1, gw, BLK), lambda b, g, i: (b, g, i)),
            pl.BlockSpec((1, LP, gw), lambda b, g, i: (b, 0, g)),
            pl.BlockSpec((1, gw, LP), lambda b, g, i: (b, g, 0)),
            pl.BlockSpec((BLK + SUBLANES, BLK), lambda b, g, i: (0, 0)),
        ],
        out_specs=pl.BlockSpec((1, BLK, gw), lambda b, g, i: (b, i, g)),
        scratch_shapes=[pltpu.VMEM((HEAD_GROUP, PAIR, BLK), BF16),
                        pltpu.VMEM((HEAD_GROUP // 2, PAIR, BLK), F32),
                        pltpu.VMEM((HEAD_GROUP, SUBLANES, BLK), F32)],
        compiler_params=pltpu.CompilerParams(
            dimension_semantics=("arbitrary", "arbitrary", "arbitrary"), vmem_limit_bytes=VMEM_LIMIT),
        name="stickbreak_attn",
    )(q_t, k, v_t, u)


def kernel(x, meta_tokens, mix_norm, ffn_norm, pool_w, pool_scale, kv_norm, w_kv, w_q, w_o,
           ffn_w_up, ffn_conv_w, ffn_conv_b, ffn_w_down, final_norm):
    row = lambda v: v.reshape(1, -1)

    w_up, w_down = ffn_w_up.astype(BF16), ffn_w_down.astype(BF16)

    def ffn(h, layer, attn_tail=None):
        out = _ffn_layer(h.reshape(ROWS, D_MODEL), layer, ffn_norm[:, None, :], w_up, ffn_conv_w,
                         ffn_conv_b[:, None, :], w_down, attn_tail)
        return out.reshape(BATCH, LP, D_MODEL)

    h = _pool_layer(x, meta_tokens.astype(x.dtype), row(mix_norm[0]), pool_w[0].astype(BF16),
                    row(pool_scale[0]))
    h = ffn(h, 0)

    w_k, w_v = w_kv[:, :D_MODEL], w_kv[:, D_MODEL:]
    k, v_t, q_t = _projections(h, row(kv_norm), row(mix_norm[1]), w_k.astype(BF16),
                               w_v.T.astype(BF16), (w_q[0].T * (HEAD_DIM ** -0.5)).astype(BF16))
    attn = _attention(q_t, k, v_t)
    h = ffn(h, 1, (attn.reshape(ROWS, D_MODEL), w_o[0].astype(BF16), row(final_norm)))
    return h[:, BLK:]
```

```python
import functools

import numpy as np
import jax
import jax.numpy as jnp
from jax import lax
from jax.experimental import pallas as pl
from jax.experimental.pallas import tpu as pltpu

D_MODEL = 1024
BATCH = 4
SEQ = 4096
N_META = 16
POOL_WINDOWS = (2, 4, 8, 16)
POOL_GROUP_DIM = D_MODEL // len(POOL_WINDOWS)
N_HEADS = 16
HEAD_DIM = D_MODEL // N_HEADS
D_FF = 2816
CONV_WIDTH = 3
RMS_EPS = 1e-6

SUBLANES = 8
LANES = 128
MXU_DIM = 256

BLK = MXU_DIM
ROW_PAD = BLK - N_META
LP = SEQ + BLK
N_BLK = LP // BLK
ROWS = BATCH * LP
MAX_WINDOW = max(POOL_WINDOWS)
HEAD_GROUP = 8
PAIR = 2 * HEAD_DIM
UNDERFLOW_LOG = -104.0
LOG2_E = 1.4426950408889634

TM_POOL = BLK
TM_FFN = 512
FFN_CHUNK = MXU_DIM
N_FFN_CHUNKS = D_FF // FFN_CHUNK
VMEM_LIMIT = 56 * 1024 * 1024

F32 = jnp.float32
BF16 = jnp.bfloat16


def _rms_scale(x):
    return lax.rsqrt(jnp.mean(x * x, axis=-1, keepdims=True) + RMS_EPS)


def _pool_kernel(x_ref, meta_ref, g_ref, w_ref, sc_ref, o_ref, buf_ref):
    j = pl.program_id(0)
    tm = TM_POOL

    @pl.when(j == 0)
    def _():
        buf_ref[:, 0:MAX_WINDOW, :] = jnp.zeros((BATCH, MAX_WINDOW, D_MODEL), F32)

    head_tile = jnp.concatenate([jnp.zeros((ROW_PAD, D_MODEL), F32), meta_ref[...]], axis=0)
    pos = j * tm + lax.broadcasted_iota(jnp.int32, (tm, 1), 0) - ROW_PAD
    xs, diffs = [], [[] for _ in POOL_WINDOWS]
    for b in range(BATCH):
        x = jnp.where(j == 0, head_tile, x_ref[b])
        xn = (x * _rms_scale(x)) * g_ref[...]
        buf_ref[b, MAX_WINDOW:MAX_WINDOW + tm, :] = xn
        win = buf_ref[b]
        shift = 1
        for g, w in enumerate(POOL_WINDOWS):
            assert w == 2 * shift
            win = win[:, (POOL_GROUP_DIM if g else 0):]
            win = win + pltpu.roll(win, shift, axis=0)
            count = jnp.clip(pos + 1, 1, w).astype(F32)
            mean = win[MAX_WINDOW:, 0:POOL_GROUP_DIM] / count
            diffs[g].append((mean - xn[:, g * POOL_GROUP_DIM:(g + 1) * POOL_GROUP_DIM]).astype(BF16))
            shift = w
        buf_ref[b, 0:MAX_WINDOW, :] = buf_ref[b, tm:tm + MAX_WINDOW, :]
        xs.append(x)
    y = jnp.concatenate([jnp.dot(jnp.concatenate(diffs[g], axis=0), w_ref[g], preferred_element_type=F32)
                         for g in range(len(POOL_WINDOWS))], axis=1) * sc_ref[...]
    for b in range(BATCH):
        o_ref[b] = jnp.where(pos >= 0, xs[b] + y[b * tm:(b + 1) * tm], 0.0)


def _pool_layer(x, meta, gain, w, scale):
    const2 = lambda j: (0, 0)
    return pl.pallas_call(
        _pool_kernel,
        out_shape=jax.ShapeDtypeStruct((BATCH, LP, D_MODEL), F32),
        grid=(LP // TM_POOL,),
        in_specs=[
            pl.BlockSpec((BATCH, TM_POOL, D_MODEL), lambda j: (0, jnp.maximum(j - 1, 0), 0)),
            pl.BlockSpec((N_META, D_MODEL), const2),
            pl.BlockSpec((1, D_MODEL), const2),
            pl.BlockSpec((len(POOL_WINDOWS), POOL_GROUP_DIM, POOL_GROUP_DIM), lambda j: (0, 0, 0)),
            pl.BlockSpec((1, D_MODEL), const2),
        ],
        out_specs=pl.BlockSpec((BATCH, TM_POOL, D_MODEL), lambda j: (0, j, 0)),
        scratch_shapes=[pltpu.VMEM((BATCH, MAX_WINDOW + TM_POOL, D_MODEL), F32)],
        compiler_params=pltpu.CompilerParams(dimension_semantics=("arbitrary",),
                                             vmem_limit_bytes=VMEM_LIMIT),
        name="pool_mixer",
    )(x, meta, gain, w, scale)


def _pad_row_mask(row0, tm):
    row = row0 + lax.broadcasted_iota(jnp.int32, (tm, 1), 0)
    is_pad = jnp.zeros((tm, 1), jnp.bool_)
    for b in range(BATCH):
        is_pad = is_pad | ((row >= b * LP) & (row < b * LP + ROW_PAD))
    return jnp.logical_not(is_pad)


def _ffn_kernel(*refs, attn_tail):
    if attn_tail:
        h_ref, attn_ref, wo_ref, fg_ref, *refs = refs
    else:
        h_ref, *refs = refs
    g_ref, wup_ref, cw_ref, cb_ref, wdn_ref, o_ref, carry_ref, xn_ref, act_ref = refs
    i = pl.program_id(0)
    tm = TM_FFN
    cw = 2 * FFN_CHUNK

    @pl.when(i == 0)
    def _():
        carry_ref[...] = jnp.zeros(carry_ref.shape, F32)

    x = h_ref[...]
    if attn_tail:
        x = x + jnp.dot(attn_ref[...], wo_ref[...], preferred_element_type=F32)
    xn_ref[...] = ((x * _rms_scale(x)) * g_ref[...]).astype(BF16)

    def chunk_cols(ref, c):
        g0 = c * FFN_CHUNK
        return jnp.concatenate([ref[:, g0:g0 + FFN_CHUNK], ref[:, D_FF + g0:D_FF + g0 + FFN_CHUNK]], axis=1)

    for c in range(N_FFN_CHUNKS):
        u = jnp.dot(xn_ref[...], chunk_cols(wup_ref, c), preferred_element_type=F32)
        ext = jnp.concatenate([carry_ref[c], u], axis=0)
        carry_ref[c] = u[tm - SUBLANES:tm, :]
        w = chunk_cols(cw_ref, c)
        cv = chunk_cols(cb_ref, c) + w[CONV_WIDTH - 1:CONV_WIDTH] * u
        for k in range(CONV_WIDTH - 1):
            first = SUBLANES - (CONV_WIDTH - 1) + k
            cv = cv + w[k:k + 1] * ext[first:first + tm, :]
        half_gate = 0.5 * cv[:, 0:FFN_CHUNK]
        act = (half_gate + half_gate * jnp.tanh(half_gate)) * cv[:, FFN_CHUNK:cw]
        act_ref[:, c * FFN_CHUNK:(c + 1) * FFN_CHUNK] = act.astype(BF16)

    out = x + jnp.dot(act_ref[...], wdn_ref[...], preferred_element_type=F32)
    if attn_tail:
        out = (out * _rms_scale(out)) * fg_ref[...]
    o_ref[...] = jnp.where(_pad_row_mask(i * tm, tm), out, 0.0)


def _ffn_layer(h, layer, gain, w_up, conv_w, conv_b, w_down, attn_tail=None):
    const = lambda i: (0, 0)
    of_layer = lambda i: (layer, 0, 0)
    row_tile = pl.BlockSpec((TM_FFN, D_MODEL), lambda i: (i, 0))
    operands, in_specs = [h], [row_tile]
    if attn_tail is not None:
        operands += list(attn_tail)
        in_specs += [row_tile,
                     pl.BlockSpec((D_MODEL, D_MODEL), const, pipeline_mode=pl.Buffered(1)),
                     pl.BlockSpec((1, D_MODEL), const)]
    operands += [gain, w_up, conv_w, conv_b, w_down]
    in_specs += [
        pl.BlockSpec((None, 1, D_MODEL), of_layer),
        pl.BlockSpec((None, D_MODEL, 2 * D_FF), of_layer, pipeline_mode=pl.Buffered(1)),
        pl.BlockSpec((None, CONV_WIDTH, 2 * D_FF), of_layer),
        pl.BlockSpec((None, 1, 2 * D_FF), of_layer),
        pl.BlockSpec((None, D_FF, D_MODEL), of_layer, pipeline_mode=pl.Buffered(1)),
    ]
    return pl.pallas_call(
        functools.partial(_ffn_kernel, attn_tail=attn_tail is not None),
        out_shape=jax.ShapeDtypeStruct((ROWS, D_MODEL), F32),
        grid=(ROWS // TM_FFN,),
        in_specs=in_specs,
        out_specs=row_tile,
        scratch_shapes=[
            pltpu.VMEM((N_FFN_CHUNKS, SUBLANES, 2 * FFN_CHUNK), F32),
            pltpu.VMEM((TM_FFN, D_MODEL), BF16),
            pltpu.VMEM((TM_FFN, D_FF), BF16),
        ],
        compiler_params=pltpu.CompilerParams(dimension_semantics=("arbitrary",),
                                             vmem_limit_bytes=VMEM_LIMIT),
        name="conv_ffn" if attn_tail is None else "attn_out_conv_ffn_norm",
    )(*operands)


_NT_DIMS = (((1,), (1,)), ((), ()))


def _proj_kernel(h_ref, gkv_ref, gq_ref, wk_ref, wvt_ref, wqt_ref, k_ref, vt_ref, qt_ref):
    tm = TM_POOL
    x = h_ref[...].reshape(BATCH * tm, D_MODEL)
    xr = x * _rms_scale(x)
    xk = (xr * gkv_ref[...]).astype(BF16)
    xq = (xr * gq_ref[...]).astype(BF16)
    k = jnp.dot(xk, wk_ref[...], preferred_element_type=F32).astype(BF16)
    vt = lax.dot_general(wvt_ref[...], xk, _NT_DIMS, preferred_element_type=F32).astype(BF16)
    qt = lax.dot_general(wqt_ref[...], xq, _NT_DIMS, preferred_element_type=F32).astype(BF16)
    for b in range(BATCH):
        k_ref[b] = k[b * tm:(b + 1) * tm]
        vt_ref[b] = vt[:, b * tm:(b + 1) * tm]
        qt_ref[b] = qt[:, b * tm:(b + 1) * tm]


def _projections(h, g_kv, g_q, w_k, w_vt, w_qt):
    const = lambda j: (0, 0)
    tm = TM_POOL
    row_spec = pl.BlockSpec((BATCH, tm, D_MODEL), lambda j: (0, j, 0))
    col_spec = pl.BlockSpec((BATCH, D_MODEL, tm), lambda j: (0, 0, j))
    w_spec = pl.BlockSpec((D_MODEL, D_MODEL), const, pipeline_mode=pl.Buffered(1))
    return pl.pallas_call(
        _proj_kernel,
        out_shape=(jax.ShapeDtypeStruct((BATCH, LP, D_MODEL), BF16),
                   jax.ShapeDtypeStruct((BATCH, D_MODEL, LP), BF16),
                   jax.ShapeDtypeStruct((BATCH, D_MODEL, LP), BF16)),
        grid=(LP // tm,),
        in_specs=[row_spec, pl.BlockSpec((1, D_MODEL), const), pl.BlockSpec((1, D_MODEL), const),
                  w_spec, w_spec, w_spec],
        out_specs=(row_spec, col_spec, col_spec),
        compiler_params=pltpu.CompilerParams(dimension_semantics=("arbitrary",),
                                             vmem_limit_bytes=VMEM_LIMIT),
        name="qkv_proj",
    )(h, g_kv, g_q, w_k, w_vt, w_qt)


def _suffix_sum_matrix():
    s = np.arange(BLK + SUBLANES)[:, None]
    j = np.arange(BLK)[None, :]
    return jnp.asarray(np.where(s < BLK, j > s, True), BF16)


def _attn_kernel(qt_ref, k_ref, vt_ref, u_ref, o_ref, qm_ref, acc_ref, carry_ref):
    i = pl.program_id(2)
    pair_row = lax.broadcasted_iota(jnp.int32, (PAIR, 1), 0)
    for h in range(HEAD_GROUP):
        first = (h % 2) * HEAD_DIM
        mine = (pair_row >= first) & (pair_row < first + HEAD_DIM)
        qp = qt_ref[0, (h // 2) * PAIR:(h // 2 + 1) * PAIR, :]
        qm_ref[h] = jnp.where(mine, qp, jnp.zeros((), BF16))
    acc_ref[...] = jnp.zeros(acc_ref.shape, F32)
    carry_ref[...] = jnp.zeros(carry_ref.shape, F32)

    def sweep(blocks):
        row = lax.broadcasted_iota(jnp.int32, (BLK, BLK), 0)
        col = lax.broadcasted_iota(jnp.int32, (BLK, BLK), 1)
        valid = {None: None}
        for j, kind in blocks:
            if kind == "causal":
                valid[kind] = row < col
            elif kind == "edge":
                key = j * BLK + row
                valid[kind] = (key < i * BLK + col) & (key >= ROW_PAD)
        chains = [(pl.multiple_of(j * BLK, BLK), kind, h) for j, kind in blocks for h in range(HEAD_GROUP)]
        zs = [jnp.dot(k_ref[0, pl.ds(start, BLK), (h // 2) * PAIR:(h // 2 + 1) * PAIR], qm_ref[h],
                      preferred_element_type=F32) for start, _, h in chains]
        log_betas, sums = [], []
        for (_, kind, h), z in zip(chains, zs):
            soft = jnp.log(1.0 + jnp.exp2(jnp.abs(z) * -LOG2_E))
            log_beta = jnp.minimum(z, 0.0) - soft
            log_1m = log_beta - z
            if kind is not None:
                log_1m = jnp.where(valid[kind], log_1m, 0.0)
            log_betas.append(log_beta)
            sums.append(jnp.dot(u_ref[...], log_1m.astype(BF16), preferred_element_type=F32))
        for (start, kind, h), log_beta, s in zip(chains, log_betas, sums):
            a = jnp.exp(log_beta + s[0:BLK] + carry_ref[h, 0:1, :])
            if kind is not None:
                a = jnp.where(valid[kind], a, 0.0)
            vb = vt_ref[0, h * HEAD_DIM:(h + 1) * HEAD_DIM, pl.ds(start, BLK)]
            rows = slice((h % 2) * HEAD_DIM, (h % 2 + 1) * HEAD_DIM)
            acc_ref[h // 2, rows, :] += jnp.dot(vb, a.astype(BF16), preferred_element_type=F32)
            carry_ref[h] += s[BLK:BLK + SUBLANES]

    def any_weight_left():
        m = carry_ref[0]
        for h in range(1, HEAD_GROUP):
            m = jnp.maximum(m, carry_ref[h])
        return jnp.max(m) >= UNDERFLOW_LOG

    @pl.when(i >= 2)
    def _():
        sweep([(i, "causal"), (i - 1, None)])

    @pl.when(i < 2)
    def _():
        sweep([(i, "edge")])

    def interior(state):
        j, _ = state
        sweep([(j, None)])
        return j - 1, any_weight_left()

    j_first = jnp.where(i >= 2, i - 2, i - 1)
    j_end, go = lax.while_loop(lambda s: (s[0] >= 1) & s[1], interior, (j_first, any_weight_left()))

    @pl.when((j_end == 0) & go)
    def _():
        sweep([(0, "edge")])

    for p in range(HEAD_GROUP // 2):
        o_ref[0, :, p * PAIR:(p + 1) * PAIR] = acc_ref[p].T.astype(BF16)


def _attention(q_t, k, v_t):
    u = _suffix_sum_matrix()
    gw = HEAD_GROUP * HEAD_DIM
    return pl.pallas_call(
        _attn_kernel,
        out_shape=jax.ShapeDtypeStruct((BATCH, LP, D_MODEL), BF16),
        grid=(BATCH, N_HEADS // HEAD_GROUP, N_BLK),
        in_specs=[
            pl.BlockSpec((1, gw, BLK), lambda b, g, i: (b, g, i)),
            pl.BlockSpec((1, LP, gw), lambda b, g, i: (b, 0, g)),
            pl.BlockSpec((1, gw, LP), lambda b, g, i: (b, g, 0)),
            pl.BlockSpec((BLK + SUBLANES, BLK), lambda b, g, i: (0, 0)),
        ],
        out_specs=pl.BlockSpec((1, BLK, gw), lambda b, g, i: (b, i, g)),
        scratch_shapes=[pltpu.VMEM((HEAD_GROUP, PAIR, BLK), BF16),
                        pltpu.VMEM((HEAD_GROUP // 2, PAIR, BLK), F32),
                        pltpu.VMEM((HEAD_GROUP, SUBLANES, BLK), F32)],
        compiler_params=pltpu.CompilerParams(
            dimension_semantics=("arbitrary", "arbitrary", "arbitrary"), vmem_limit_bytes=VMEM_LIMIT),
        name="stickbreak_attn",
    )(q_t, k, v_t, u)


def kernel(x, meta_tokens, mix_norm, ffn_norm, pool_w, pool_scale, kv_norm, w_kv, w_q, w_o,
           ffn_w_up, ffn_conv_w, ffn_conv_b, ffn_w_down, final_norm):
    row = lambda v: v.reshape(1, -1)

    w_up, w_down = ffn_w_up.astype(BF16), ffn_w_down.astype(BF16)

    def ffn(h, layer, attn_tail=None):
        out = _ffn_layer(h.reshape(ROWS, D_MODEL), layer, ffn_norm[:, None, :], w_up, ffn_conv_w,
                         ffn_conv_b[:, None, :], w_down, attn_tail)
        return out.reshape(BATCH, LP, D_MODEL)

    h = _pool_layer(x, meta_tokens.astype(x.dtype), row(mix_norm[0]), pool_w[0].astype(BF16),
                    row(pool_scale[0]))
    h = ffn(h, 0)

    w_k, w_v = w_kv[:, :D_MODEL], w_kv[:, D_MODEL:]
    k, v_t, q_t = _projections(h, row(kv_norm), row(mix_norm[1]), w_k.astype(BF16),
                               w_v.T.astype(BF16), (w_q[0].T * (HEAD_DIM ** -0.5)).astype(BF16))
    attn = _attention(q_t, k, v_t)
    h = ffn(h, 1, (attn.reshape(ROWS, D_MODEL), w_o[0].astype(BF16), row(final_norm)))
    return h[:, BLK:]
```

```python
import functools

import numpy as np
import jax
import jax.numpy as jnp
from jax import lax
from jax.experimental import pallas as pl
from jax.experimental.pallas import tpu as pltpu

D_MODEL = 1024
BATCH = 4
SEQ = 4096
N_META = 16
POOL_WINDOWS = (2, 4, 8, 16)
POOL_GROUP_DIM = D_MODEL // len(POOL_WINDOWS)
N_HEADS = 16
HEAD_DIM = D_MODEL // N_HEADS
D_FF = 2816
CONV_WIDTH = 3
RMS_EPS = 1e-6

SUBLANES = 8
LANES = 128
MXU_DIM = 256

BLK = MXU_DIM
ROW_PAD = BLK - N_META
LP = SEQ + BLK
N_BLK = LP // BLK
ROWS = BATCH * LP
MAX_WINDOW = max(POOL_WINDOWS)
HEAD_GROUP = 8
PAIR = 2 * HEAD_DIM
UNDERFLOW_LOG = -104.0
LOG2_E = 1.4426950408889634

TM_POOL = BLK
TM_FFN = 512
FFN_CHUNK = MXU_DIM
N_FFN_CHUNKS = D_FF // FFN_CHUNK
VMEM_LIMIT = 56 * 1024 * 1024

F32 = jnp.float32
BF16 = jnp.bfloat16


def _rms_scale(x):
    return lax.rsqrt(jnp.mean(x * x, axis=-1, keepdims=True) + RMS_EPS)


def _pool_kernel(x_ref, meta_ref, g_ref, w_ref, sc_ref, o_ref, buf_ref):
    j = pl.program_id(0)
    tm = TM_POOL

    @pl.when(j == 0)
    def _():
        buf_ref[:, 0:MAX_WINDOW, :] = jnp.zeros((BATCH, MAX_WINDOW, D_MODEL), F32)

    head_tile = jnp.concatenate([jnp.zeros((ROW_PAD, D_MODEL), F32), meta_ref[...]], axis=0)
    pos = j * tm + lax.broadcasted_iota(jnp.int32, (tm, 1), 0) - ROW_PAD
    xs, diffs = [], [[] for _ in POOL_WINDOWS]
    for b in range(BATCH):
        x = jnp.where(j == 0, head_tile, x_ref[b])
        xn = (x * _rms_scale(x)) * g_ref[...]
        buf_ref[b, MAX_WINDOW:MAX_WINDOW + tm, :] = xn
        win = buf_ref[b]
        shift = 1
        for g, w in enumerate(POOL_WINDOWS):
            assert w == 2 * shift
            win = win[:, (POOL_GROUP_DIM if g else 0):]
            win = win + pltpu.roll(win, shift, axis=0)
            count = jnp.clip(pos + 1, 1, w).astype(F32)
            mean = win[MAX_WINDOW:, 0:POOL_GROUP_DIM] / count
            diffs[g].append((mean - xn[:, g * POOL_GROUP_DIM:(g + 1) * POOL_GROUP_DIM]).astype(BF16))
            shift = w
        buf_ref[b, 0:MAX_WINDOW, :] = buf_ref[b, tm:tm + MAX_WINDOW, :]
        xs.append(x)
    y = jnp.concatenate([jnp.dot(jnp.concatenate(diffs[g], axis=0), w_ref[g], preferred_element_type=F32)
                         for g in range(len(POOL_WINDOWS))], axis=1) * sc_ref[...]
    for b in range(BATCH):
        o_ref[b] = jnp.where(pos >= 0, xs[b] + y[b * tm:(b + 1) * tm], 0.0)


def _pool_layer(x, meta, gain, w, scale):
    const2 = lambda j: (0, 0)
    return pl.pallas_call(
        _pool_kernel,
        out_shape=jax.ShapeDtypeStruct((BATCH, LP, D_MODEL), F32),
        grid=(LP // TM_POOL,),
        in_specs=[
            pl.BlockSpec((BATCH, TM_POOL, D_MODEL), lambda j: (0, jnp.maximum(j - 1, 0), 0)),
            pl.BlockSpec((N_META, D_MODEL), const2),
            pl.BlockSpec((1, D_MODEL), const2),
            pl.BlockSpec((len(POOL_WINDOWS), POOL_GROUP_DIM, POOL_GROUP_DIM), lambda j: (0, 0, 0)),
            pl.BlockSpec((1, D_MODEL), const2),
        ],
        out_specs=pl.BlockSpec((BATCH, TM_POOL, D_MODEL), lambda j: (0, j, 0)),
        scratch_shapes=[pltpu.VMEM((BATCH, MAX_WINDOW + TM_POOL, D_MODEL), F32)],
        compiler_params=pltpu.CompilerParams(dimension_semantics=("arbitrary",),
                                             vmem_limit_bytes=VMEM_LIMIT),
        name="pool_mixer",
    )(x, meta, gain, w, scale)


def _pad_row_mask(row0, tm):
    row = row0 + lax.broadcasted_iota(jnp.int32, (tm, 1), 0)
    is_pad = jnp.zeros((tm, 1), jnp.bool_)
    for b in range(BATCH):
        is_pad = is_pad | ((row >= b * LP) & (row < b * LP + ROW_PAD))
    return jnp.logical_not(is_pad)


def _ffn_kernel(*refs, attn_tail):
    if attn_tail:
        h_ref, attn_ref, wo_ref, fg_ref, *refs = refs
    else:
        h_ref, *refs = refs
    g_ref, wup_ref, cw_ref, cb_ref, wdn_ref, o_ref, carry_ref, xn_ref, act_ref = refs
    i = pl.program_id(0)
    tm = TM_FFN
    cw = 2 * FFN_CHUNK

    @pl.when(i == 0)
    def _():
        carry_ref[...] = jnp.zeros(carry_ref.shape, F32)

    x = h_ref[...]
    if attn_tail:
        x = x + jnp.dot(attn_ref[...], wo_ref[...], preferred_element_type=F32)
    xn_ref[...] = ((x * _rms_scale(x)) * g_ref[...]).astype(BF16)

    def chunk_cols(ref, c):
        g0 = c * FFN_CHUNK
        return jnp.concatenate([ref[:, g0:g0 + FFN_CHUNK], ref[:, D_FF + g0:D_FF + g0 + FFN_CHUNK]], axis=1)

    for c in range(N_FFN_CHUNKS):
        u = jnp.dot(xn_ref[...], chunk_cols(wup_ref, c), preferred_element_type=F32)
        ext = jnp.concatenate([carry_ref[c], u], axis=0)
        carry_ref[c] = u[tm - SUBLANES:tm, :]
        w = chunk_cols(cw_ref, c)
        cv = chunk_cols(cb_ref, c) + w[CONV_WIDTH - 1:CONV_WIDTH] * u
        for k in range(CONV_WIDTH - 1):
            first = SUBLANES - (CONV_WIDTH - 1) + k
            cv = cv + w[k:k + 1] * ext[first:first + tm, :]
        half_gate = 0.5 * cv[:, 0:FFN_CHUNK]
        act = (half_gate + half_gate * jnp.tanh(half_gate)) * cv[:, FFN_CHUNK:cw]
        act_ref[:, c * FFN_CHUNK:(c + 1) * FFN_CHUNK] = act.astype(BF16)

    out = x + jnp.dot(act_ref[...], wdn_ref[...], preferred_element_type=F32)
    if attn_tail:
        out = (out * _rms_scale(out)) * fg_ref[...]
    o_ref[...] = jnp.where(_pad_row_mask(i * tm, tm), out, 0.0)


def _ffn_layer(h, layer, gain, w_up, conv_w, conv_b, w_down, attn_tail=None):
    const = lambda i: (0, 0)
    of_layer = lambda i: (layer, 0, 0)
    row_tile = pl.BlockSpec((TM_FFN, D_MODEL), lambda i: (i, 0))
    operands, in_specs = [h], [row_tile]
    if attn_tail is not None:
        operands += list(attn_tail)
        in_specs += [row_tile,
                     pl.BlockSpec((D_MODEL, D_MODEL), const, pipeline_mode=pl.Buffered(1)),
                     pl.BlockSpec((1, D_MODEL), const)]
    operands += [gain, w_up, conv_w, conv_b, w_down]
    in_specs += [
        pl.BlockSpec((None, 1, D_MODEL), of_layer),
        pl.BlockSpec((None, D_MODEL, 2 * D_FF), of_layer, pipeline_mode=pl.Buffered(1)),
        pl.BlockSpec((None, CONV_WIDTH, 2 * D_FF), of_layer),
        pl.BlockSpec((None, 1, 2 * D_FF), of_layer),
        pl.BlockSpec((None, D_FF, D_MODEL), of_layer, pipeline_mode=pl.Buffered(1)),
    ]
    return pl.pallas_call(
        functools.partial(_ffn_kernel, attn_tail=attn_tail is not None),
        out_shape=jax.ShapeDtypeStruct((ROWS, D_MODEL), F32),
        grid=(ROWS // TM_FFN,),
        in_specs=in_specs,
        out_specs=row_tile,
        scratch_shapes=[
            pltpu.VMEM((N_FFN_CHUNKS, SUBLANES, 2 * FFN_CHUNK), F32),
            pltpu.VMEM((TM_FFN, D_MODEL), BF16),
            pltpu.VMEM((TM_FFN, D_FF), BF16),
        ],
        compiler_params=pltpu.CompilerParams(dimension_semantics=("arbitrary",),
                                             vmem_limit_bytes=VMEM_LIMIT),
        name="conv_ffn" if attn_tail is None else "attn_out_conv_ffn_norm",
    )(*operands)


_NT_DIMS = (((1,), (1,)), ((), ()))


def _proj_kernel(h_ref, gkv_ref, gq_ref, wk_ref, wvt_ref, wqt_ref, k_ref, vt_ref, qt_ref):
    tm = TM_POOL
    x = h_ref[...].reshape(BATCH * tm, D_MODEL)
    xr = x * _rms_scale(x)
    xk = (xr * gkv_ref[...]).astype(BF16)
    xq = (xr * gq_ref[...]).astype(BF16)
    k = jnp.dot(xk, wk_ref[...], preferred_element_type=F32).astype(BF16)
    vt = lax.dot_general(wvt_ref[...], xk, _NT_DIMS, preferred_element_type=F32).astype(BF16)
    qt = lax.dot_general(wqt_ref[...], xq, _NT_DIMS, preferred_element_type=F32).astype(BF16)
    for b in range(BATCH):
        k_ref[b] = k[b * tm:(b + 1) * tm]
        vt_ref[b] = vt[:, b * tm:(b + 1) * tm]
        qt_ref[b] = qt[:, b * tm:(b + 1) * tm]


def _projections(h, g_kv, g_q, w_k, w_vt, w_qt):
    const = lambda j: (0, 0)
    tm = TM_POOL
    row_spec = pl.BlockSpec((BATCH, tm, D_MODEL), lambda j: (0, j, 0))
    col_spec = pl.BlockSpec((BATCH, D_MODEL, tm), lambda j: (0, 0, j))
    w_spec = pl.BlockSpec((D_MODEL, D_MODEL), const, pipeline_mode=pl.Buffered(1))
    return pl.pallas_call(
        _proj_kernel,
        out_shape=(jax.ShapeDtypeStruct((BATCH, LP, D_MODEL), BF16),
                   jax.ShapeDtypeStruct((BATCH, D_MODEL, LP), BF16),
                   jax.ShapeDtypeStruct((BATCH, D_MODEL, LP), BF16)),
        grid=(LP // tm,),
        in_specs=[row_spec, pl.BlockSpec((1, D_MODEL), const), pl.BlockSpec((1, D_MODEL), const),
                  w_spec, w_spec, w_spec],
        out_specs=(row_spec, col_spec, col_spec),
        compiler_params=pltpu.CompilerParams(dimension_semantics=("arbitrary",),
                                             vmem_limit_bytes=VMEM_LIMIT),
        name="qkv_proj",
    )(h, g_kv, g_q, w_k, w_vt, w_qt)


def _suffix_sum_matrix():
    s = np.arange(BLK + SUBLANES)[:, None]
    j = np.arange(BLK)[None, :]
    return jnp.asarray(np.where(s < BLK, j > s, True), BF16)


def _attn_kernel(qt_ref, k_ref, vt_ref, u_ref, o_ref, qm_ref, acc_ref, carry_ref, go_ref):
    i = pl.program_id(2)
    pair_row = lax.broadcasted_iota(jnp.int32, (PAIR, 1), 0)
    for h in range(HEAD_GROUP):
        first = (h % 2) * HEAD_DIM
        mine = (pair_row >= first) & (pair_row < first + HEAD_DIM)
        qp = qt_ref[0, (h // 2) * PAIR:(h // 2 + 1) * PAIR, :]
        qm_ref[h] = jnp.where(mine, qp, jnp.zeros((), BF16))
    acc_ref[...] = jnp.zeros(acc_ref.shape, F32)
    carry_ref[...] = jnp.zeros(carry_ref.shape, F32)

    def sweep(blocks):
        row = lax.broadcasted_iota(jnp.int32, (BLK, BLK), 0)
        col = lax.broadcasted_iota(jnp.int32, (BLK, BLK), 1)
        valid = {None: None}
        for j, kind in blocks:
            if kind == "causal":
                valid[kind] = row < col
            elif kind == "edge":
                key = j * BLK + row
                valid[kind] = (key < i * BLK + col) & (key >= ROW_PAD)
        chains = [(pl.multiple_of(j * BLK, BLK), kind, h) for j, kind in blocks for h in range(HEAD_GROUP)]
        zs = [jnp.dot(k_ref[0, pl.ds(start, BLK), (h // 2) * PAIR:(h // 2 + 1) * PAIR], qm_ref[h],
                      preferred_element_type=F32) for start, _, h in chains]
        log_betas, sums = [], []
        for (_, kind, h), z in zip(chains, zs):
            soft = jnp.log(1.0 + jnp.exp2(jnp.abs(z) * -LOG2_E))
            log_beta = jnp.minimum(z, 0.0) - soft
            log_1m = log_beta - z
            if kind is not None:
                log_1m = jnp.where(valid[kind], log_1m, 0.0)
            log_betas.append(log_beta)
            sums.append(jnp.dot(u_ref[...], log_1m.astype(BF16), preferred_element_type=F32))
        carry = [carry_ref[h] for h in range(HEAD_GROUP)]
        carry_in = []
        for (_, _, h), s in zip(chains, sums):
            carry_in.append(carry[h])
            carry[h] = carry[h] + s[BLK:BLK + SUBLANES]
        most = carry[0]
        for h in range(HEAD_GROUP):
            carry_ref[h] = carry[h]
            most = jnp.maximum(most, carry[h])
        go_ref[0] = (jnp.max(most) >= UNDERFLOW_LOG).astype(jnp.int32)
        for (start, kind, h), log_beta, s, c_in in zip(chains, log_betas, sums, carry_in):
            a = jnp.exp(log_beta + s[0:BLK] + c_in[0:1, :])
            if kind is not None:
                a = jnp.where(valid[kind], a, 0.0)
            vb = vt_ref[0, h * HEAD_DIM:(h + 1) * HEAD_DIM, pl.ds(start, BLK)]
            rows = slice((h % 2) * HEAD_DIM, (h % 2 + 1) * HEAD_DIM)
            acc_ref[h // 2, rows, :] += jnp.dot(vb, a.astype(BF16), preferred_element_type=F32)

    @pl.when(i >= 2)
    def _():
        sweep([(i, "causal"), (i - 1, None)])

    @pl.when(i < 2)
    def _():
        sweep([(i, "edge")])

    def interior(j):
        sweep([(j, None)])
        return j - 1

    j_end = lax.while_loop(lambda j: (j >= 1) & (go_ref[0] != 0), interior,
                           jnp.where(i >= 2, i - 2, i - 1))

    @pl.when((j_end == 0) & (go_ref[0] != 0))
    def _():
        sweep([(0, "edge")])

    for p in range(HEAD_GROUP // 2):
        o_ref[0, :, p * PAIR:(p + 1) * PAIR] = acc_ref[p].T.astype(BF16)


def _attention(q_t, k, v_t):
    u = _suffix_sum_matrix()
    gw = HEAD_GROUP * HEAD_DIM
    return pl.pallas_call(
        _attn_kernel,
        out_shape=jax.ShapeDtypeStruct((BATCH, LP, D_MODEL), BF16),
        grid=(BATCH, N_HEADS // HEAD_GROUP, N_BLK),
        in_specs=[
            pl.BlockSpec((1, gw, BLK), lambda b, g, i: (b, g, i)),
            pl.BlockSpec((1, LP, gw), lambda b, g, i: (b, 0, g)),
            pl.BlockSpec((1, gw, LP), lambda b, g, i: (b, g, 0)),
            pl.BlockSpec((BLK + SUBLANES, BLK), lambda b, g, i: (0, 0)),
        ],
        out_specs=pl.BlockSpec((1, BLK, gw), lambda b, g, i: (b, i, g)),
        scratch_shapes=[pltpu.VMEM((HEAD_GROUP, PAIR, BLK), BF16),
                        pltpu.VMEM((HEAD_GROUP // 2, PAIR, BLK), F32),
                        pltpu.VMEM((HEAD_GROUP, SUBLANES, BLK), F32),
                        pltpu.SMEM((1,), jnp.int32)],
        compiler_params=pltpu.CompilerParams(
            dimension_semantics=("arbitrary", "arbitrary", "arbitrary"), vmem_limit_bytes=VMEM_LIMIT),
        name="stickbreak_attn",
    )(q_t, k, v_t, u)


def kernel(x, meta_tokens, mix_norm, ffn_norm, pool_w, pool_scale, kv_norm, w_kv, w_q, w_o,
           ffn_w_up, ffn_conv_w, ffn_conv_b, ffn_w_down, final_norm):
    row = lambda v: v.reshape(1, -1)

    w_up, w_down = ffn_w_up.astype(BF16), ffn_w_down.astype(BF16)

    def ffn(h, layer, attn_tail=None):
        out = _ffn_layer(h.reshape(ROWS, D_MODEL), layer, ffn_norm[:, None, :], w_up, ffn_conv_w,
                         ffn_conv_b[:, None, :], w_down, attn_tail)
        return out.reshape(BATCH, LP, D_MODEL)

    h = _pool_layer(x, meta_tokens.astype(x.dtype), row(mix_norm[0]), pool_w[0].astype(BF16),
                    row(pool_scale[0]))
    h = ffn(h, 0)

    w_k, w_v = w_kv[:, :D_MODEL], w_kv[:, D_MODEL:]
    k, v_t, q_t = _projections(h, row(kv_norm), row(mix_norm[1]), w_k.astype(BF16),
                               w_v.T.astype(BF16), (w_q[0].T * (HEAD_DIM ** -0.5)).astype(BF16))
    attn = _attention(q_t, k, v_t)
    h = ffn(h, 1, (attn.reshape(ROWS, D_MODEL), w_o[0].astype(BF16), row(final_norm)))
    return h[:, BLK:]
```

```python
import functools

import numpy as np
import jax
import jax.numpy as jnp
from jax import lax
from jax.experimental import pallas as pl
from jax.experimental.pallas import tpu as pltpu

D_MODEL = 1024
BATCH = 4
SEQ = 4096
N_META = 16
POOL_WINDOWS = (2, 4, 8, 16)
POOL_GROUP_DIM = D_MODEL // len(POOL_WINDOWS)
N_HEADS = 16
HEAD_DIM = D_MODEL // N_HEADS
D_FF = 2816
CONV_WIDTH = 3
RMS_EPS = 1e-6

SUBLANES = 8
LANES = 128
MXU_DIM = 256

BLK = MXU_DIM
ROW_PAD = BLK - N_META
LP = SEQ + BLK
N_BLK = LP // BLK
ROWS = BATCH * LP
MAX_WINDOW = max(POOL_WINDOWS)
HEAD_GROUP = 8
PAIR = 2 * HEAD_DIM
UNDERFLOW_LOG = -104.0
LOG2_E = 1.4426950408889634

TM_POOL = BLK
TM_FFN = 512
FFN_CHUNK = MXU_DIM
N_FFN_CHUNKS = D_FF // FFN_CHUNK
VMEM_LIMIT = 56 * 1024 * 1024

F32 = jnp.float32
BF16 = jnp.bfloat16


def _rms_scale(x):
    return lax.rsqrt(jnp.mean(x * x, axis=-1, keepdims=True) + RMS_EPS)


def _pool_kernel(x_ref, meta_ref, g_ref, w_ref, sc_ref, o_ref, buf_ref):
    j = pl.program_id(0)
    tm = TM_POOL

    @pl.when(j == 0)
    def _():
        buf_ref[:, 0:MAX_WINDOW, :] = jnp.zeros((BATCH, MAX_WINDOW, D_MODEL), F32)

    head_tile = jnp.concatenate([jnp.zeros((ROW_PAD, D_MODEL), F32), meta_ref[...]], axis=0)
    pos = j * tm + lax.broadcasted_iota(jnp.int32, (tm, 1), 0) - ROW_PAD
    xs, diffs = [], [[] for _ in POOL_WINDOWS]
    for b in range(BATCH):
        x = jnp.where(j == 0, head_tile, x_ref[b])
        xn = (x * _rms_scale(x)) * g_ref[...]
        buf_ref[b, MAX_WINDOW:MAX_WINDOW + tm, :] = xn
        win = buf_ref[b]
        shift = 1
        for g, w in enumerate(POOL_WINDOWS):
            assert w == 2 * shift
            win = win[:, (POOL_GROUP_DIM if g else 0):]
            win = win + pltpu.roll(win, shift, axis=0)
            count = jnp.clip(pos + 1, 1, w).astype(F32)
            mean = win[MAX_WINDOW:, 0:POOL_GROUP_DIM] / count
            diffs[g].append((mean - xn[:, g * POOL_GROUP_DIM:(g + 1) * POOL_GROUP_DIM]).astype(BF16))
            shift = w
        buf_ref[b, 0:MAX_WINDOW, :] = buf_ref[b, tm:tm + MAX_WINDOW, :]
        xs.append(x)
    y = jnp.concatenate([jnp.dot(jnp.concatenate(diffs[g], axis=0), w_ref[g], preferred_element_type=F32)
                         for g in range(len(POOL_WINDOWS))], axis=1) * sc_ref[...]
    for b in range(BATCH):
        o_ref[b] = jnp.where(pos >= 0, xs[b] + y[b * tm:(b + 1) * tm], 0.0)


def _pool_layer(x, meta, gain, w, scale):
    const2 = lambda j: (0, 0)
    return pl.pallas_call(
        _pool_kernel,
        out_shape=jax.ShapeDtypeStruct((BATCH, LP, D_MODEL), F32),
        grid=(LP // TM_POOL,),
        in_specs=[
            pl.BlockSpec((BATCH, TM_POOL, D_MODEL), lambda j: (0, jnp.maximum(j - 1, 0), 0)),
            pl.BlockSpec((N_META, D_MODEL), const2),
            pl.BlockSpec((1, D_MODEL), const2),
            pl.BlockSpec((len(POOL_WINDOWS), POOL_GROUP_DIM, POOL_GROUP_DIM), lambda j: (0, 0, 0)),
            pl.BlockSpec((1, D_MODEL), const2),
        ],
        out_specs=pl.BlockSpec((BATCH, TM_POOL, D_MODEL), lambda j: (0, j, 0)),
        scratch_shapes=[pltpu.VMEM((BATCH, MAX_WINDOW + TM_POOL, D_MODEL), F32)],
        compiler_params=pltpu.CompilerParams(dimension_semantics=("arbitrary",),
                                             vmem_limit_bytes=VMEM_LIMIT),
        name="pool_mixer",
    )(x, meta, gain, w, scale)


def _pad_row_mask(row0, tm):
    row = row0 + lax.broadcasted_iota(jnp.int32, (tm, 1), 0)
    is_pad = jnp.zeros((tm, 1), jnp.bool_)
    for b in range(BATCH):
        is_pad = is_pad | ((row >= b * LP) & (row < b * LP + ROW_PAD))
    return jnp.logical_not(is_pad)


def _ffn_kernel(*refs, attn_tail):
    if attn_tail:
        h_ref, attn_ref, wo_ref, fg_ref, g_ref, wup_ref, cw_ref, cb_ref, wdn_ref, o_ref, *scratch = refs
    else:
        (h_ref, gkv_ref, gq_ref, g_ref, wup_ref, cw_ref, cb_ref, wdn_ref,
         o_ref, xk_ref, xq_ref, *scratch) = refs
    carry_ref, xn_ref, act_ref = scratch
    i = pl.program_id(0)
    tm = TM_FFN
    cw = 2 * FFN_CHUNK

    @pl.when(i == 0)
    def _():
        carry_ref[...] = jnp.zeros(carry_ref.shape, F32)

    x = h_ref[...]
    if attn_tail:
        x = x + jnp.dot(attn_ref[...], wo_ref[...], preferred_element_type=F32)
    xn_ref[...] = ((x * _rms_scale(x)) * g_ref[...]).astype(BF16)

    def chunk_cols(ref, c):
        g0 = c * FFN_CHUNK
        return jnp.concatenate([ref[:, g0:g0 + FFN_CHUNK], ref[:, D_FF + g0:D_FF + g0 + FFN_CHUNK]], axis=1)

    for c in range(N_FFN_CHUNKS):
        u = jnp.dot(xn_ref[...], chunk_cols(wup_ref, c), preferred_element_type=F32)
        ext = jnp.concatenate([carry_ref[c], u], axis=0)
        carry_ref[c] = u[tm - SUBLANES:tm, :]
        w = chunk_cols(cw_ref, c)
        cv = chunk_cols(cb_ref, c) + w[CONV_WIDTH - 1:CONV_WIDTH] * u
        for k in range(CONV_WIDTH - 1):
            first = SUBLANES - (CONV_WIDTH - 1) + k
            cv = cv + w[k:k + 1] * ext[first:first + tm, :]
        half_gate = 0.5 * cv[:, 0:FFN_CHUNK]
        act = (half_gate + half_gate * jnp.tanh(half_gate)) * cv[:, FFN_CHUNK:cw]
        act_ref[:, c * FFN_CHUNK:(c + 1) * FFN_CHUNK] = act.astype(BF16)

    out = x + jnp.dot(act_ref[...], wdn_ref[...], preferred_element_type=F32)
    if attn_tail:
        out = (out * _rms_scale(out)) * fg_ref[...]
    out = jnp.where(_pad_row_mask(i * tm, tm), out, 0.0)
    o_ref[...] = out
    if not attn_tail:
        normed = out * _rms_scale(out)
        xk_ref[...] = (normed * gkv_ref[...]).astype(BF16)
        xq_ref[...] = (normed * gq_ref[...]).astype(BF16)


def _ffn_layer(h, layer, gain, w_up, conv_w, conv_b, w_down, attn_tail=None, qk_gains=None):
    const = lambda i: (0, 0)
    of_layer = lambda i: (layer, 0, 0)
    row_tile = pl.BlockSpec((TM_FFN, D_MODEL), lambda i: (i, 0))
    operands, in_specs = [h], [row_tile]
    out_shape, out_specs = jax.ShapeDtypeStruct((ROWS, D_MODEL), F32), row_tile
    if attn_tail is not None:
        operands += list(attn_tail)
        in_specs += [row_tile,
                     pl.BlockSpec((D_MODEL, D_MODEL), const, pipeline_mode=pl.Buffered(1)),
                     pl.BlockSpec((1, D_MODEL), const)]
    else:
        operands += list(qk_gains)
        in_specs += [pl.BlockSpec((1, D_MODEL), const)] * 2
        out_shape = (out_shape,) + (jax.ShapeDtypeStruct((ROWS, D_MODEL), BF16),) * 2
        out_specs = (row_tile,) * 3
    operands += [gain, w_up, conv_w, conv_b, w_down]
    in_specs += [
        pl.BlockSpec((None, 1, D_MODEL), of_layer),
        pl.BlockSpec((None, D_MODEL, 2 * D_FF), of_layer, pipeline_mode=pl.Buffered(1)),
        pl.BlockSpec((None, CONV_WIDTH, 2 * D_FF), of_layer),
        pl.BlockSpec((None, 1, 2 * D_FF), of_layer),
        pl.BlockSpec((None, D_FF, D_MODEL), of_layer, pipeline_mode=pl.Buffered(1)),
    ]
    return pl.pallas_call(
        functools.partial(_ffn_kernel, attn_tail=attn_tail is not None),
        out_shape=out_shape,
        grid=(ROWS // TM_FFN,),
        in_specs=in_specs,
        out_specs=out_specs,
        scratch_shapes=[
            pltpu.VMEM((N_FFN_CHUNKS, SUBLANES, 2 * FFN_CHUNK), F32),
            pltpu.VMEM((TM_FFN, D_MODEL), BF16),
            pltpu.VMEM((TM_FFN, D_FF), BF16),
        ],
        compiler_params=pltpu.CompilerParams(dimension_semantics=("arbitrary",),
                                             vmem_limit_bytes=VMEM_LIMIT),
        name="conv_ffn" if attn_tail is None else "attn_out_conv_ffn_norm",
    )(*operands)


_NT_DIMS = (((1,), (1,)), ((), ()))


def _suffix_sum_matrix():
    s = np.arange(BLK + SUBLANES)[:, None]
    j = np.arange(BLK)[None, :]
    return jnp.asarray(np.where(s < BLK, j > s, True), BF16)


def _attn_kernel(xk0_ref, xq0_ref, xkn_ref, xqn_ref, wk_ref, wvt_ref, wqt_ref, u_ref, o_ref,
                 k_ref, vt_ref, qt_ref, qm_ref, acc_ref, carry_ref, go_ref):
    i = pl.program_id(2)

    def project(xk_ref, xq_ref, block, slot):
        rows = pl.ds(pl.multiple_of(block * BLK, BLK), BLK)
        xk = xk_ref[0]
        k_ref[rows, :] = jnp.dot(xk, wk_ref[...], preferred_element_type=F32).astype(BF16)
        vt_ref[:, rows] = lax.dot_general(wvt_ref[...], xk, _NT_DIMS,
                                          preferred_element_type=F32).astype(BF16)
        qt_ref[slot] = lax.dot_general(wqt_ref[...], xq_ref[0], _NT_DIMS,
                                       preferred_element_type=F32).astype(BF16)

    @pl.when(i == 0)
    def _():
        project(xk0_ref, xq0_ref, 0, 0)

    pair_row = lax.broadcasted_iota(jnp.int32, (PAIR, 1), 0)
    for h in range(HEAD_GROUP):
        first = (h % 2) * HEAD_DIM
        mine = (pair_row >= first) & (pair_row < first + HEAD_DIM)
        qp = qt_ref[lax.rem(i, 2), (h // 2) * PAIR:(h // 2 + 1) * PAIR, :]
        qm_ref[h] = jnp.where(mine, qp, jnp.zeros((), BF16))
    acc_ref[...] = jnp.zeros(acc_ref.shape, F32)
    carry_ref[...] = jnp.zeros(carry_ref.shape, F32)

    def sweep(blocks, project_ahead=False):
        row = lax.broadcasted_iota(jnp.int32, (BLK, BLK), 0)
        col = lax.broadcasted_iota(jnp.int32, (BLK, BLK), 1)
        valid = {None: None}
        for j, kind in blocks:
            if kind == "causal":
                valid[kind] = row < col
            elif kind == "edge":
                key = j * BLK + row
                valid[kind] = (key < i * BLK + col) & (key >= ROW_PAD)
        chains = [(pl.multiple_of(j * BLK, BLK), kind, h) for j, kind in blocks for h in range(HEAD_GROUP)]
        zs = [jnp.dot(k_ref[pl.ds(start, BLK), (h // 2) * PAIR:(h // 2 + 1) * PAIR], qm_ref[h],
                      preferred_element_type=F32) for start, _, h in chains]
        if project_ahead:
            project(xkn_ref, xqn_ref, jnp.minimum(i + 1, N_BLK - 1), lax.rem(i + 1, 2))
        log_betas, sums = [], []
        for (_, kind, h), z in zip(chains, zs):
            soft = jnp.log(1.0 + jnp.exp2(jnp.abs(z) * -LOG2_E))
            log_beta = jnp.minimum(z, 0.0) - soft
            log_1m = log_beta - z
            if kind is not None:
                log_1m = jnp.where(valid[kind], log_1m, 0.0)
            log_betas.append(log_beta)
            sums.append(jnp.dot(u_ref[...], log_1m.astype(BF16), preferred_element_type=F32))
        carry = [carry_ref[h] for h in range(HEAD_GROUP)]
        carry_in = []
        for (_, _, h), s in zip(chains, sums):
            carry_in.append(carry[h])
            carry[h] = carry[h] + s[BLK:BLK + SUBLANES]
        most = carry[0]
        for h in range(HEAD_GROUP):
            carry_ref[h] = carry[h]
            most = jnp.maximum(most, carry[h])
        go_ref[0] = (jnp.max(most) >= UNDERFLOW_LOG).astype(jnp.int32)
        for (start, kind, h), log_beta, s, c_in in zip(chains, log_betas, sums, carry_in):
            a = jnp.exp(log_beta + s[0:BLK] + c_in[0:1, :])
            if kind is not None:
                a = jnp.where(valid[kind], a, 0.0)
            vb = vt_ref[h * HEAD_DIM:(h + 1) * HEAD_DIM, pl.ds(start, BLK)]
            rows = slice((h % 2) * HEAD_DIM, (h % 2 + 1) * HEAD_DIM)
            acc_ref[h // 2, rows, :] += jnp.dot(vb, a.astype(BF16), preferred_element_type=F32)

    @pl.when(i >= 2)
    def _():
        sweep([(i, "causal"), (i - 1, None)], project_ahead=True)

    @pl.when(i < 2)
    def _():
        sweep([(i, "edge")], project_ahead=True)

    def interior(j):
        sweep([(j, None)])
        return j - 1

    j_end = lax.while_loop(lambda j: (j >= 1) & (go_ref[0] != 0), interior,
                           jnp.where(i >= 2, i - 2, i - 1))

    @pl.when((j_end == 0) & (go_ref[0] != 0))
    def _():
        sweep([(0, "edge")])

    for p in range(HEAD_GROUP // 2):
        o_ref[0, :, p * PAIR:(p + 1) * PAIR] = acc_ref[p].T.astype(BF16)


def _attention(xk, xq, w_k, w_vt, w_qt):
    gw = HEAD_GROUP * HEAD_DIM
    first_block = pl.BlockSpec((1, BLK, D_MODEL), lambda b, g, i: (b, 0, 0))
    next_block = pl.BlockSpec((1, BLK, D_MODEL), lambda b, g, i: (b, jnp.minimum(i + 1, N_BLK - 1), 0))
    return pl.pallas_call(
        _attn_kernel,
        out_shape=jax.ShapeDtypeStruct((BATCH, LP, D_MODEL), BF16),
        grid=(BATCH, N_HEADS // HEAD_GROUP, N_BLK),
        in_specs=[
            first_block, first_block, next_block, next_block,
            pl.BlockSpec((D_MODEL, gw), lambda b, g, i: (0, g)),
            pl.BlockSpec((gw, D_MODEL), lambda b, g, i: (g, 0)),
            pl.BlockSpec((gw, D_MODEL), lambda b, g, i: (g, 0)),
            pl.BlockSpec((BLK + SUBLANES, BLK), lambda b, g, i: (0, 0)),
        ],
        out_specs=pl.BlockSpec((1, BLK, gw), lambda b, g, i: (b, i, g)),
        scratch_shapes=[pltpu.VMEM((LP, gw), BF16),
                        pltpu.VMEM((gw, LP), BF16),
                        pltpu.VMEM((2, gw, BLK), BF16),
                        pltpu.VMEM((HEAD_GROUP, PAIR, BLK), BF16),
                        pltpu.VMEM((HEAD_GROUP // 2, PAIR, BLK), F32),
                        pltpu.VMEM((HEAD_GROUP, SUBLANES, BLK), F32),
                        pltpu.SMEM((1,), jnp.int32)],
        compiler_params=pltpu.CompilerParams(
            dimension_semantics=("arbitrary", "arbitrary", "arbitrary"), vmem_limit_bytes=VMEM_LIMIT),
        name="stickbreak_attn",
    )(xk, xq, xk, xq, w_k, w_vt, w_qt, _suffix_sum_matrix())


def kernel(x, meta_tokens, mix_norm, ffn_norm, pool_w, pool_scale, kv_norm, w_kv, w_q, w_o,
           ffn_w_up, ffn_conv_w, ffn_conv_b, ffn_w_down, final_norm):
    row = lambda v: v.reshape(1, -1)

    w_up, w_down = ffn_w_up.astype(BF16), ffn_w_down.astype(BF16)

    def ffn(h, layer, **mode):
        return _ffn_layer(h.reshape(ROWS, D_MODEL), layer, ffn_norm[:, None, :], w_up, ffn_conv_w,
                          ffn_conv_b[:, None, :], w_down, **mode)

    h = _pool_layer(x, meta_tokens.astype(x.dtype), row(mix_norm[0]), pool_w[0].astype(BF16),
                    row(pool_scale[0]))
    h, xk, xq = ffn(h, 0, qk_gains=(row(kv_norm), row(mix_norm[1])))

    w_k, w_v = w_kv[:, :D_MODEL], w_kv[:, D_MODEL:]
    by_batch = lambda a: a.reshape(BATCH, LP, D_MODEL)
    attn = _attention(by_batch(xk), by_batch(xq), w_k.astype(BF16), w_v.T.astype(BF16),
                      (w_q[0].T * (HEAD_DIM ** -0.5)).astype(BF16))
    h = ffn(h, 1, attn_tail=(attn.reshape(ROWS, D_MODEL), w_o[0].astype(BF16), row(final_norm)))
    return by_batch(h)[:, BLK:]
```

```python
import functools

import numpy as np
import jax
import jax.numpy as jnp
from jax import lax
from jax.experimental import pallas as pl
from jax.experimental.pallas import tpu as pltpu

D_MODEL = 1024
BATCH = 4
SEQ = 4096
N_META = 16
POOL_WINDOWS = (2, 4, 8, 16)
POOL_GROUP_DIM = D_MODEL // len(POOL_WINDOWS)
N_HEADS = 16
HEAD_DIM = D_MODEL // N_HEADS
D_FF = 2816
CONV_WIDTH = 3
RMS_EPS = 1e-6

SUBLANES = 8
LANES = 128
MXU_DIM = 256

BLK = MXU_DIM
ROW_PAD = BLK - N_META
LP = SEQ + BLK
N_BLK = LP // BLK
ROWS = BATCH * LP
MAX_WINDOW = max(POOL_WINDOWS)
HEAD_GROUP = 16
PAIR = 2 * HEAD_DIM
UNDERFLOW_LOG = -104.0
LOG2_E = 1.4426950408889634
PROJECT_AFTER_CHAINS = 0

TM_POOL = BLK
TM_FFN = 512
FFN_CHUNK = MXU_DIM
N_FFN_CHUNKS = D_FF // FFN_CHUNK
VMEM_LIMIT = 56 * 1024 * 1024

F32 = jnp.float32
BF16 = jnp.bfloat16


def _rms_scale(x):
    return lax.rsqrt(jnp.mean(x * x, axis=-1, keepdims=True) + RMS_EPS)


def _pool_kernel(x_ref, meta_ref, g_ref, w_ref, sc_ref, o_ref, buf_ref):
    j = pl.program_id(0)
    tm = TM_POOL

    @pl.when(j == 0)
    def _():
        buf_ref[:, 0:MAX_WINDOW, :] = jnp.zeros((BATCH, MAX_WINDOW, D_MODEL), F32)

    head_tile = jnp.concatenate([jnp.zeros((ROW_PAD, D_MODEL), F32), meta_ref[...]], axis=0)
    pos = j * tm + lax.broadcasted_iota(jnp.int32, (tm, 1), 0) - ROW_PAD
    xs, diffs = [], [[] for _ in POOL_WINDOWS]
    for b in range(BATCH):
        x = jnp.where(j == 0, head_tile, x_ref[b])
        xn = (x * _rms_scale(x)) * g_ref[...]
        buf_ref[b, MAX_WINDOW:MAX_WINDOW + tm, :] = xn
        win = buf_ref[b]
        shift = 1
        for g, w in enumerate(POOL_WINDOWS):
            assert w == 2 * shift
            win = win[:, (POOL_GROUP_DIM if g else 0):]
            win = win + pltpu.roll(win, shift, axis=0)
            count = jnp.clip(pos + 1, 1, w).astype(F32)
            mean = win[MAX_WINDOW:, 0:POOL_GROUP_DIM] / count
            diffs[g].append((mean - xn[:, g * POOL_GROUP_DIM:(g + 1) * POOL_GROUP_DIM]).astype(BF16))
            shift = w
        buf_ref[b, 0:MAX_WINDOW, :] = buf_ref[b, tm:tm + MAX_WINDOW, :]
        xs.append(x)
    y = jnp.concatenate([jnp.dot(jnp.concatenate(diffs[g], axis=0), w_ref[g], preferred_element_type=F32)
                         for g in range(len(POOL_WINDOWS))], axis=1) * sc_ref[...]
    for b in range(BATCH):
        o_ref[b] = jnp.where(pos >= 0, xs[b] + y[b * tm:(b + 1) * tm], 0.0)


def _pool_layer(x, meta, gain, w, scale):
    const2 = lambda j: (0, 0)
    return pl.pallas_call(
        _pool_kernel,
        out_shape=jax.ShapeDtypeStruct((BATCH, LP, D_MODEL), F32),
        grid=(LP // TM_POOL,),
        in_specs=[
            pl.BlockSpec((BATCH, TM_POOL, D_MODEL), lambda j: (0, jnp.maximum(j - 1, 0), 0)),
            pl.BlockSpec((N_META, D_MODEL), const2),
            pl.BlockSpec((1, D_MODEL), const2),
            pl.BlockSpec((len(POOL_WINDOWS), POOL_GROUP_DIM, POOL_GROUP_DIM), lambda j: (0, 0, 0)),
            pl.BlockSpec((1, D_MODEL), const2),
        ],
        out_specs=pl.BlockSpec((BATCH, TM_POOL, D_MODEL), lambda j: (0, j, 0)),
        scratch_shapes=[pltpu.VMEM((BATCH, MAX_WINDOW + TM_POOL, D_MODEL), F32)],
        compiler_params=pltpu.CompilerParams(dimension_semantics=("arbitrary",),
                                             vmem_limit_bytes=VMEM_LIMIT),
        name="pool_mixer",
    )(x, meta, gain, w, scale)


def _pad_row_mask(row0, tm):
    row = row0 + lax.broadcasted_iota(jnp.int32, (tm, 1), 0)
    is_pad = jnp.zeros((tm, 1), jnp.bool_)
    for b in range(BATCH):
        is_pad = is_pad | ((row >= b * LP) & (row < b * LP + ROW_PAD))
    return jnp.logical_not(is_pad)


def _ffn_kernel(*refs, attn_tail):
    if attn_tail:
        h_ref, attn_ref, wo_ref, fg_ref, g_ref, wup_ref, cw_ref, cb_ref, wdn_ref, o_ref, *scratch = refs
    else:
        (h_ref, gkv_ref, gq_ref, g_ref, wup_ref, cw_ref, cb_ref, wdn_ref,
         o_ref, xk_ref, xq_ref, *scratch) = refs
    carry_ref, xn_ref, act_ref = scratch
    i = pl.program_id(0)
    tm = TM_FFN
    cw = 2 * FFN_CHUNK

    @pl.when(i == 0)
    def _():
        carry_ref[...] = jnp.zeros(carry_ref.shape, F32)

    x = h_ref[...]
    if attn_tail:
        x = x + jnp.dot(attn_ref[...], wo_ref[...], preferred_element_type=F32)
    xn_ref[...] = ((x * _rms_scale(x)) * g_ref[...]).astype(BF16)

    def chunk_cols(ref, c):
        g0 = c * FFN_CHUNK
        return jnp.concatenate([ref[:, g0:g0 + FFN_CHUNK], ref[:, D_FF + g0:D_FF + g0 + FFN_CHUNK]], axis=1)

    for c in range(N_FFN_CHUNKS):
        u = jnp.dot(xn_ref[...], chunk_cols(wup_ref, c), preferred_element_type=F32)
        ext = jnp.concatenate([carry_ref[c], u], axis=0)
        carry_ref[c] = u[tm - SUBLANES:tm, :]
        w = chunk_cols(cw_ref, c)
        cv = chunk_cols(cb_ref, c) + w[CONV_WIDTH - 1:CONV_WIDTH] * u
        for k in range(CONV_WIDTH - 1):
            first = SUBLANES - (CONV_WIDTH - 1) + k
            cv = cv + w[k:k + 1] * ext[first:first + tm, :]
        half_gate = 0.5 * cv[:, 0:FFN_CHUNK]
        act = (half_gate + half_gate * jnp.tanh(half_gate)) * cv[:, FFN_CHUNK:cw]
        act_ref[:, c * FFN_CHUNK:(c + 1) * FFN_CHUNK] = act.astype(BF16)

    out = x + jnp.dot(act_ref[...], wdn_ref[...], preferred_element_type=F32)
    if attn_tail:
        out = (out * _rms_scale(out)) * fg_ref[...]
    out = jnp.where(_pad_row_mask(i * tm, tm), out, 0.0)
    o_ref[...] = out
    if not attn_tail:
        normed = out * _rms_scale(out)
        xk_ref[...] = (normed * gkv_ref[...]).astype(BF16)
        xq_ref[...] = (normed * gq_ref[...]).astype(BF16)


def _ffn_layer(h, layer, gain, w_up, conv_w, conv_b, w_down, attn_tail=None, qk_gains=None):
    const = lambda i: (0, 0)
    of_layer = lambda i: (layer, 0, 0)
    row_tile = pl.BlockSpec((TM_FFN, D_MODEL), lambda i: (i, 0))
    operands, in_specs = [h], [row_tile]
    out_shape, out_specs = jax.ShapeDtypeStruct((ROWS, D_MODEL), F32), row_tile
    if attn_tail is not None:
        operands += list(attn_tail)
        in_specs += [row_tile,
                     pl.BlockSpec((D_MODEL, D_MODEL), const, pipeline_mode=pl.Buffered(1)),
                     pl.BlockSpec((1, D_MODEL), const)]
    else:
        operands += list(qk_gains)
        in_specs += [pl.BlockSpec((1, D_MODEL), const)] * 2
        out_shape = (out_shape,) + (jax.ShapeDtypeStruct((ROWS, D_MODEL), BF16),) * 2
        out_specs = (row_tile,) * 3
    operands += [gain, w_up, conv_w, conv_b, w_down]
    in_specs += [
        pl.BlockSpec((None, 1, D_MODEL), of_layer),
        pl.BlockSpec((None, D_MODEL, 2 * D_FF), of_layer, pipeline_mode=pl.Buffered(1)),
        pl.BlockSpec((None, CONV_WIDTH, 2 * D_FF), of_layer),
        pl.BlockSpec((None, 1, 2 * D_FF), of_layer),
        pl.BlockSpec((None, D_FF, D_MODEL), of_layer, pipeline_mode=pl.Buffered(1)),
    ]
    return pl.pallas_call(
        functools.partial(_ffn_kernel, attn_tail=attn_tail is not None),
        out_shape=out_shape,
        grid=(ROWS // TM_FFN,),
        in_specs=in_specs,
        out_specs=out_specs,
        scratch_shapes=[
            pltpu.VMEM((N_FFN_CHUNKS, SUBLANES, 2 * FFN_CHUNK), F32),
            pltpu.VMEM((TM_FFN, D_MODEL), BF16),
            pltpu.VMEM((TM_FFN, D_FF), BF16),
        ],
        compiler_params=pltpu.CompilerParams(dimension_semantics=("arbitrary",),
                                             vmem_limit_bytes=VMEM_LIMIT),
        name="conv_ffn" if attn_tail is None else "attn_out_conv_ffn_norm",
    )(*operands)


_NT_DIMS = (((1,), (1,)), ((), ()))


def _suffix_sum_matrix():
    s = np.arange(BLK + SUBLANES)[:, None]
    j = np.arange(BLK)[None, :]
    return jnp.asarray(np.where(s < BLK, j > s, True), BF16)


def _attn_kernel(xk0_ref, xq0_ref, xkn_ref, xqn_ref, wk_ref, wvt_ref, wqt_ref, u_ref, o_ref,
                 k_ref, vt_ref, qt_ref, qm_ref, acc_ref, carry_ref, go_ref):
    i = pl.program_id(2)

    def project(xk_ref, xq_ref, block, slot):
        rows = pl.ds(pl.multiple_of(block * BLK, BLK), BLK)
        xk = xk_ref[0]
        k_ref[rows, :] = jnp.dot(xk, wk_ref[...], preferred_element_type=F32).astype(BF16)
        vt_ref[:, rows] = lax.dot_general(wvt_ref[...], xk, _NT_DIMS,
                                          preferred_element_type=F32).astype(BF16)
        qt_ref[slot] = lax.dot_general(wqt_ref[...], xq_ref[0], _NT_DIMS,
                                       preferred_element_type=F32).astype(BF16)

    @pl.when(i == 0)
    def _():
        project(xk0_ref, xq0_ref, 0, 0)

    pair_row = lax.broadcasted_iota(jnp.int32, (PAIR, 1), 0)
    for h in range(HEAD_GROUP):
        first = (h % 2) * HEAD_DIM
        mine = (pair_row >= first) & (pair_row < first + HEAD_DIM)
        qp = qt_ref[lax.rem(i, 2), (h // 2) * PAIR:(h // 2 + 1) * PAIR, :]
        qm_ref[h] = jnp.where(mine, qp, jnp.zeros((), BF16))
    acc_ref[...] = jnp.zeros(acc_ref.shape, F32)
    carry_ref[...] = jnp.zeros(carry_ref.shape, F32)

    def sweep(blocks, project_ahead=False):
        row = lax.broadcasted_iota(jnp.int32, (BLK, BLK), 0)
        col = lax.broadcasted_iota(jnp.int32, (BLK, BLK), 1)
        valid = {None: None}
        for j, kind in blocks:
            if kind == "causal":
                valid[kind] = row < col
            elif kind == "edge":
                key = j * BLK + row
                valid[kind] = (key < i * BLK + col) & (key >= ROW_PAD)
        chains = [(pl.multiple_of(j * BLK, BLK), kind, h) for j, kind in blocks for h in range(HEAD_GROUP)]
        zs = [jnp.dot(k_ref[pl.ds(start, BLK), (h // 2) * PAIR:(h // 2 + 1) * PAIR], qm_ref[h],
                      preferred_element_type=F32) for start, _, h in chains]
        log_betas, sums = [], []
        for n, ((_, kind, h), z) in enumerate(zip(chains, zs)):
            if project_ahead and n == PROJECT_AFTER_CHAINS:
                project(xkn_ref, xqn_ref, jnp.minimum(i + 1, N_BLK - 1), lax.rem(i + 1, 2))
            soft = jnp.log(1.0 + jnp.exp2(jnp.abs(z) * -LOG2_E))
            log_beta = jnp.minimum(z, 0.0) - soft
            log_1m = log_beta - z
            if kind is not None:
                log_1m = jnp.where(valid[kind], log_1m, 0.0)
            log_betas.append(log_beta)
            sums.append(jnp.dot(u_ref[...], log_1m.astype(BF16), preferred_element_type=F32))
        carry = [carry_ref[h] for h in range(HEAD_GROUP)]
        carry_in = []
        for (_, _, h), s in zip(chains, sums):
            carry_in.append(carry[h])
            carry[h] = carry[h] + s[BLK:BLK + SUBLANES]
        most = carry[0]
        for h in range(HEAD_GROUP):
            carry_ref[h] = carry[h]
            most = jnp.maximum(most, carry[h])
        go_ref[0] = (jnp.max(most) >= UNDERFLOW_LOG).astype(jnp.int32)
        for (start, kind, h), log_beta, s, c_in in zip(chains, log_betas, sums, carry_in):
            a = jnp.exp(log_beta + s[0:BLK] + c_in[0:1, :])
            if kind is not None:
                a = jnp.where(valid[kind], a, 0.0)
            vb = vt_ref[h * HEAD_DIM:(h + 1) * HEAD_DIM, pl.ds(start, BLK)]
            rows = slice((h % 2) * HEAD_DIM, (h % 2 + 1) * HEAD_DIM)
            acc_ref[h // 2, rows, :] += jnp.dot(vb, a.astype(BF16), preferred_element_type=F32)

    @pl.when(i >= 2)
    def _():
        sweep([(i, "causal"), (i - 1, None)], project_ahead=True)

    @pl.when(i < 2)
    def _():
        sweep([(i, "edge")], project_ahead=True)

    def interior(j):
        sweep([(j, None)])
        return j - 1

    j_end = lax.while_loop(lambda j: (j >= 1) & (go_ref[0] != 0), interior,
                           jnp.where(i >= 2, i - 2, i - 1))

    @pl.when((j_end == 0) & (go_ref[0] != 0))
    def _():
        sweep([(0, "edge")])

    for p in range(HEAD_GROUP // 2):
        o_ref[0, :, p * PAIR:(p + 1) * PAIR] = acc_ref[p].T.astype(BF16)


def _attention(xk, xq, w_k, w_vt, w_qt):
    gw = HEAD_GROUP * HEAD_DIM
    first_block = pl.BlockSpec((1, BLK, D_MODEL), lambda b, g, i: (b, 0, 0))
    next_block = pl.BlockSpec((1, BLK, D_MODEL), lambda b, g, i: (b, jnp.minimum(i + 1, N_BLK - 1), 0))
    return pl.pallas_call(
        _attn_kernel,
        out_shape=jax.ShapeDtypeStruct((BATCH, LP, D_MODEL), BF16),
        grid=(BATCH, N_HEADS // HEAD_GROUP, N_BLK),
        in_specs=[
            first_block, first_block, next_block, next_block,
            pl.BlockSpec((D_MODEL, gw), lambda b, g, i: (0, g)),
            pl.BlockSpec((gw, D_MODEL), lambda b, g, i: (g, 0)),
            pl.BlockSpec((gw, D_MODEL), lambda b, g, i: (g, 0)),
            pl.BlockSpec((BLK + SUBLANES, BLK), lambda b, g, i: (0, 0)),
        ],
        out_specs=pl.BlockSpec((1, BLK, gw), lambda b, g, i: (b, i, g)),
        scratch_shapes=[pltpu.VMEM((LP, gw), BF16),
                        pltpu.VMEM((gw, LP), BF16),
                        pltpu.VMEM((2, gw, BLK), BF16),
                        pltpu.VMEM((HEAD_GROUP, PAIR, BLK), BF16),
                        pltpu.VMEM((HEAD_GROUP // 2, PAIR, BLK), F32),
                        pltpu.VMEM((HEAD_GROUP, SUBLANES, BLK), F32),
                        pltpu.SMEM((1,), jnp.int32)],
        compiler_params=pltpu.CompilerParams(
            dimension_semantics=("arbitrary", "arbitrary", "arbitrary"), vmem_limit_bytes=VMEM_LIMIT),
        name="stickbreak_attn",
    )(xk, xq, xk, xq, w_k, w_vt, w_qt, _suffix_sum_matrix())


def kernel(x, meta_tokens, mix_norm, ffn_norm, pool_w, pool_scale, kv_norm, w_kv, w_q, w_o,
           ffn_w_up, ffn_conv_w, ffn_conv_b, ffn_w_down, final_norm):
    row = lambda v: v.reshape(1, -1)

    w_up, w_down = ffn_w_up.astype(BF16), ffn_w_down.astype(BF16)

    def ffn(h, layer, **mode):
        return _ffn_layer(h.reshape(ROWS, D_MODEL), layer, ffn_norm[:, None, :], w_up, ffn_conv_w,
                          ffn_conv_b[:, None, :], w_down, **mode)

    h = _pool_layer(x, meta_tokens.astype(x.dtype), row(mix_norm[0]), pool_w[0].astype(BF16),
                    row(pool_scale[0]))
    h, xk, xq = ffn(h, 0, qk_gains=(row(kv_norm), row(mix_norm[1])))

    w_k, w_v = w_kv[:, :D_MODEL], w_kv[:, D_MODEL:]
    by_batch = lambda a: a.reshape(BATCH, LP, D_MODEL)
    attn = _attention(by_batch(xk), by_batch(xq), w_k.astype(BF16), w_v.T.astype(BF16),
                      (w_q[0].T * (HEAD_DIM ** -0.5)).astype(BF16))
    h = ffn(h, 1, attn_tail=(attn.reshape(ROWS, D_MODEL), w_o[0].astype(BF16), row(final_norm)))
    return by_batch(h)[:, BLK:]
```

```python
import functools

import numpy as np
import jax
import jax.numpy as jnp
from jax import lax
from jax.experimental import pallas as pl
from jax.experimental.pallas import tpu as pltpu

D_MODEL = 1024
BATCH = 4
SEQ = 4096
N_META = 16
POOL_WINDOWS = (2, 4, 8, 16)
POOL_GROUP_DIM = D_MODEL // len(POOL_WINDOWS)
N_HEADS = 16
HEAD_DIM = D_MODEL // N_HEADS
D_FF = 2816
CONV_WIDTH = 3
RMS_EPS = 1e-6

SUBLANES = 8
LANES = 128
MXU_DIM = 256

BLK = MXU_DIM
ROW_PAD = BLK - N_META
LP = SEQ + BLK
N_BLK = LP // BLK
ROWS = BATCH * LP
MAX_WINDOW = max(POOL_WINDOWS)
HEAD_GROUP = 8
PAIR = 2 * HEAD_DIM
UNDERFLOW_LOG = -104.0
LOG2_E = 1.4426950408889634

TM_POOL = BLK
TM_FFN = 512
FFN_CHUNK = MXU_DIM
N_FFN_CHUNKS = D_FF // FFN_CHUNK
VMEM_LIMIT = 56 * 1024 * 1024

F32 = jnp.float32
BF16 = jnp.bfloat16


def _rms_scale(x):
    return lax.rsqrt(jnp.mean(x * x, axis=-1, keepdims=True) + RMS_EPS)


def _pool_kernel(x_ref, meta_ref, g_ref, w_ref, sc_ref, o_ref, buf_ref):
    j = pl.program_id(0)
    tm = TM_POOL

    @pl.when(j == 0)
    def _():
        buf_ref[:, 0:MAX_WINDOW, :] = jnp.zeros((BATCH, MAX_WINDOW, D_MODEL), F32)

    head_tile = jnp.concatenate([jnp.zeros((ROW_PAD, D_MODEL), F32), meta_ref[...]], axis=0)
    pos = j * tm + lax.broadcasted_iota(jnp.int32, (tm, 1), 0) - ROW_PAD
    xs, diffs = [], [[] for _ in POOL_WINDOWS]
    for b in range(BATCH):
        x = jnp.where(j == 0, head_tile, x_ref[b])
        xn = (x * _rms_scale(x)) * g_ref[...]
        buf_ref[b, MAX_WINDOW:MAX_WINDOW + tm, :] = xn
        win = buf_ref[b]
        shift = 1
        for g, w in enumerate(POOL_WINDOWS):
            assert w == 2 * shift
            win = win[:, (POOL_GROUP_DIM if g else 0):]
            win = win + pltpu.roll(win, shift, axis=0)
            count = jnp.clip(pos + 1, 1, w).astype(F32)
            mean = win[MAX_WINDOW:, 0:POOL_GROUP_DIM] / count
            diffs[g].append((mean - xn[:, g * POOL_GROUP_DIM:(g + 1) * POOL_GROUP_DIM]).astype(BF16))
            shift = w
        buf_ref[b, 0:MAX_WINDOW, :] = buf_ref[b, tm:tm + MAX_WINDOW, :]
        xs.append(x)
    y = jnp.concatenate([jnp.dot(jnp.concatenate(diffs[g], axis=0), w_ref[g], preferred_element_type=F32)
                         for g in range(len(POOL_WINDOWS))], axis=1) * sc_ref[...]
    for b in range(BATCH):
        o_ref[b] = jnp.where(pos >= 0, xs[b] + y[b * tm:(b + 1) * tm], 0.0)


def _pool_layer(x, meta, gain, w, scale):
    const2 = lambda j: (0, 0)
    return pl.pallas_call(
        _pool_kernel,
        out_shape=jax.ShapeDtypeStruct((BATCH, LP, D_MODEL), F32),
        grid=(LP // TM_POOL,),
        in_specs=[
            pl.BlockSpec((BATCH, TM_POOL, D_MODEL), lambda j: (0, jnp.maximum(j - 1, 0), 0)),
            pl.BlockSpec((N_META, D_MODEL), const2),
            pl.BlockSpec((1, D_MODEL), const2),
            pl.BlockSpec((len(POOL_WINDOWS), POOL_GROUP_DIM, POOL_GROUP_DIM), lambda j: (0, 0, 0)),
            pl.BlockSpec((1, D_MODEL), const2),
        ],
        out_specs=pl.BlockSpec((BATCH, TM_POOL, D_MODEL), lambda j: (0, j, 0)),
        scratch_shapes=[pltpu.VMEM((BATCH, MAX_WINDOW + TM_POOL, D_MODEL), F32)],
        compiler_params=pltpu.CompilerParams(dimension_semantics=("arbitrary",),
                                             vmem_limit_bytes=VMEM_LIMIT),
        name="pool_mixer",
    )(x, meta, gain, w, scale)


def _pad_row_mask(row0, tm):
    row = row0 + lax.broadcasted_iota(jnp.int32, (tm, 1), 0)
    is_pad = jnp.zeros((tm, 1), jnp.bool_)
    for b in range(BATCH):
        is_pad = is_pad | ((row >= b * LP) & (row < b * LP + ROW_PAD))
    return jnp.logical_not(is_pad)


def _ffn_kernel(*refs, attn_tail):
    if attn_tail:
        (h_ref, attn_ref, wo_ref, fg_ref, meta_up_ref, g_ref, wup_ref, cw_ref, cb_ref, wdn_ref,
         o_ref, *scratch) = refs
    else:
        (h_ref, gkv_ref, gq_ref, g_ref, wup_ref, cw_ref, cb_ref, wdn_ref,
         o_ref, xk_ref, xq_ref, *scratch) = refs
    carry_ref, xn_ref, act_ref = scratch
    tm = TM_FFN
    cw = 2 * FFN_CHUNK

    def chunk_cols(ref, c, rows=slice(None)):
        g0 = c * FFN_CHUNK
        return jnp.concatenate([ref[rows, g0:g0 + FFN_CHUNK],
                                ref[rows, D_FF + g0:D_FF + g0 + FFN_CHUNK]], axis=1)

    if attn_tail:
        @pl.when(pl.program_id(1) == 0)
        def _():
            for c in range(N_FFN_CHUNKS):
                carry_ref[c] = chunk_cols(meta_up_ref, c, slice(N_META - SUBLANES, N_META))
    else:
        i = pl.program_id(0)

        @pl.when(i == 0)
        def _():
            carry_ref[...] = jnp.zeros(carry_ref.shape, F32)

    x = h_ref[...]
    if attn_tail:
        x = x + jnp.dot(attn_ref[...], wo_ref[...], preferred_element_type=F32)
    xn_ref[...] = ((x * _rms_scale(x)) * g_ref[...]).astype(BF16)

    for c in range(N_FFN_CHUNKS):
        u = jnp.dot(xn_ref[...], chunk_cols(wup_ref, c), preferred_element_type=F32)
        ext = jnp.concatenate([carry_ref[c], u], axis=0)
        carry_ref[c] = u[tm - SUBLANES:tm, :]
        w = chunk_cols(cw_ref, c)
        cv = chunk_cols(cb_ref, c) + w[CONV_WIDTH - 1:CONV_WIDTH] * u
        for k in range(CONV_WIDTH - 1):
            first = SUBLANES - (CONV_WIDTH - 1) + k
            cv = cv + w[k:k + 1] * ext[first:first + tm, :]
        half_gate = 0.5 * cv[:, 0:FFN_CHUNK]
        act = (half_gate + half_gate * jnp.tanh(half_gate)) * cv[:, FFN_CHUNK:cw]
        act_ref[:, c * FFN_CHUNK:(c + 1) * FFN_CHUNK] = act.astype(BF16)

    out = x + jnp.dot(act_ref[...], wdn_ref[...], preferred_element_type=F32)
    if attn_tail:
        o_ref[...] = (out * _rms_scale(out)) * fg_ref[...]
    else:
        out = jnp.where(_pad_row_mask(i * tm, tm), out, 0.0)
        o_ref[...] = out
        normed = out * _rms_scale(out)
        xk_ref[...] = (normed * gkv_ref[...]).astype(BF16)
        xq_ref[...] = (normed * gq_ref[...]).astype(BF16)


def _meta_up_kernel(h_ref, attn_ref, wo_ref, g_ref, wup_ref, o_ref):
    x = h_ref[...] + jnp.dot(attn_ref[...], wo_ref[...], preferred_element_type=F32)
    xn = ((x * _rms_scale(x)) * g_ref[...]).astype(BF16)
    o_ref[...] = jnp.dot(xn, wup_ref[...], preferred_element_type=F32)


def _meta_up_projection(h, attn, w_o, layer, gain, w_up):
    const = lambda b: (0, 0)
    of_layer = lambda b: (layer, 0, 0)
    meta_rows = pl.BlockSpec((pl.Element(N_META), pl.Element(D_MODEL)),
                              lambda b: ((b * (LP // N_META) + ROW_PAD // N_META) * N_META, 0))
    return pl.pallas_call(
        _meta_up_kernel,
        out_shape=jax.ShapeDtypeStruct((BATCH, N_META, 2 * D_FF), F32),
        grid=(BATCH,),
        in_specs=[meta_rows, meta_rows,
                  pl.BlockSpec((D_MODEL, D_MODEL), const, pipeline_mode=pl.Buffered(1)),
                  pl.BlockSpec((None, 1, D_MODEL), of_layer),
                  pl.BlockSpec((None, D_MODEL, 2 * D_FF), of_layer, pipeline_mode=pl.Buffered(1))],
        out_specs=pl.BlockSpec((None, N_META, 2 * D_FF), lambda b: (b, 0, 0)),
        compiler_params=pltpu.CompilerParams(dimension_semantics=("arbitrary",),
                                             vmem_limit_bytes=VMEM_LIMIT),
        name="meta_up_proj",
    )(h, attn, w_o, gain, w_up)


def _ffn_layer(h, layer, gain, w_up, conv_w, conv_b, w_down, attn_tail=None, qk_gains=None):
    const = lambda *_: (0, 0)
    of_layer = lambda *_: (layer, 0, 0)
    if attn_tail is not None:
        attn, w_o, final_gain = attn_tail
        meta_up = _meta_up_projection(h, attn, w_o, layer, gain, w_up)
        grid = (BATCH, SEQ // TM_FFN)
        blk_per_tile = TM_FFN // BLK
        window = lambda b, k: ((b * N_BLK + 1 + k * blk_per_tile) * BLK, 0)
        row_tile = pl.BlockSpec((pl.Element(TM_FFN), pl.Element(D_MODEL)), window)
        operands = [h, attn, w_o, final_gain, meta_up]
        in_specs = [row_tile, row_tile,
                    pl.BlockSpec((D_MODEL, D_MODEL), const, pipeline_mode=pl.Buffered(1)),
                    pl.BlockSpec((1, D_MODEL), const),
                    pl.BlockSpec((None, N_META, 2 * D_FF), lambda b, k: (b, 0, 0))]
        out_shape = jax.ShapeDtypeStruct((BATCH, SEQ, D_MODEL), F32)
        out_specs = pl.BlockSpec((None, TM_FFN, D_MODEL), lambda b, k: (b, k, 0))
    else:
        grid = (ROWS // TM_FFN,)
        row_tile = pl.BlockSpec((TM_FFN, D_MODEL), lambda i: (i, 0))
        operands = [h] + list(qk_gains)
        in_specs = [row_tile] + [pl.BlockSpec((1, D_MODEL), const)] * 2
        out_shape = (jax.ShapeDtypeStruct((ROWS, D_MODEL), F32),) + (
            jax.ShapeDtypeStruct((ROWS, D_MODEL), BF16),) * 2
        out_specs = (row_tile,) * 3
    operands += [gain, w_up, conv_w, conv_b, w_down]
    in_specs += [
        pl.BlockSpec((None, 1, D_MODEL), of_layer),
        pl.BlockSpec((None, D_MODEL, 2 * D_FF), of_layer, pipeline_mode=pl.Buffered(1)),
        pl.BlockSpec((None, CONV_WIDTH, 2 * D_FF), of_layer),
        pl.BlockSpec((None, 1, 2 * D_FF), of_layer),
        pl.BlockSpec((None, D_FF, D_MODEL), of_layer, pipeline_mode=pl.Buffered(1)),
    ]
    return pl.pallas_call(
        functools.partial(_ffn_kernel, attn_tail=attn_tail is not None),
        out_shape=out_shape,
        grid=grid,
        in_specs=in_specs,
        out_specs=out_specs,
        scratch_shapes=[
            pltpu.VMEM((N_FFN_CHUNKS, SUBLANES, 2 * FFN_CHUNK), F32),
            pltpu.VMEM((TM_FFN, D_MODEL), BF16),
            pltpu.VMEM((TM_FFN, D_FF), BF16),
        ],
        compiler_params=pltpu.CompilerParams(dimension_semantics=("arbitrary",) * len(grid),
                                             vmem_limit_bytes=VMEM_LIMIT),
        name="conv_ffn" if attn_tail is None else "attn_out_conv_ffn_norm",
    )(*operands)


_NT_DIMS = (((1,), (1,)), ((), ()))


def _suffix_sum_matrix():
    s = np.arange(BLK + SUBLANES)[:, None]
    j = np.arange(BLK)[None, :]
    return jnp.asarray(np.where(s < BLK, j > s, True), BF16)


def _attn_kernel(xk0_ref, xq0_ref, xkn_ref, xqn_ref, wk_ref, wvt_ref, wqt_ref, u_ref, o_ref,
                 k_ref, vt_ref, qt_ref, qm_ref, acc_ref, carry_ref, go_ref):
    i = pl.program_id(2)

    def project(xk_ref, xq_ref, block, slot):
        rows = pl.ds(pl.multiple_of(block * BLK, BLK), BLK)
        xk = xk_ref[0]
        k_ref[rows, :] = jnp.dot(xk, wk_ref[...], preferred_element_type=F32).astype(BF16)
        vt_ref[:, rows] = lax.dot_general(wvt_ref[...], xk, _NT_DIMS,
                                          preferred_element_type=F32).astype(BF16)
        qt_ref[slot] = lax.dot_general(wqt_ref[...], xq_ref[0], _NT_DIMS,
                                       preferred_element_type=F32).astype(BF16)

    @pl.when(i == 0)
    def _():
        project(xk0_ref, xq0_ref, 0, 0)

    pair_row = lax.broadcasted_iota(jnp.int32, (PAIR, 1), 0)
    for h in range(HEAD_GROUP):
        first = (h % 2) * HEAD_DIM
        mine = (pair_row >= first) & (pair_row < first + HEAD_DIM)
        qp = qt_ref[lax.rem(i, 2), (h // 2) * PAIR:(h // 2 + 1) * PAIR, :]
        qm_ref[h] = jnp.where(mine, qp, jnp.zeros((), BF16))
    acc_ref[...] = jnp.zeros(acc_ref.shape, F32)
    carry_ref[...] = jnp.zeros(carry_ref.shape, F32)

    def sweep(blocks, project_ahead=False):
        row = lax.broadcasted_iota(jnp.int32, (BLK, BLK), 0)
        col = lax.broadcasted_iota(jnp.int32, (BLK, BLK), 1)
        valid = {None: None}
        for j, kind in blocks:
            if kind == "causal":
                valid[kind] = row < col
            elif kind == "edge":
                key = j * BLK + row
                valid[kind] = (key < i * BLK + col) & (key >= ROW_PAD)
        chains = [(pl.multiple_of(j * BLK, BLK), kind, h) for j, kind in blocks for h in range(HEAD_GROUP)]
        zs = [jnp.dot(k_ref[pl.ds(start, BLK), (h // 2) * PAIR:(h // 2 + 1) * PAIR], qm_ref[h],
                      preferred_element_type=F32) for start, _, h in chains]
        if project_ahead:
            project(xkn_ref, xqn_ref, jnp.minimum(i + 1, N_BLK - 1), lax.rem(i + 1, 2))

        log_betas, sums = [], []
        for (_, kind, h), z in zip(chains, zs):
            soft = jnp.log(1.0 + jnp.exp2(jnp.abs(z) * -LOG2_E))
            log_beta = jnp.minimum(z, 0.0) - soft
            log_1m = log_beta - z
            if kind is not None:
                log_1m = jnp.where(valid[kind], log_1m, 0.0)
            log_betas.append(log_beta)
            sums.append(jnp.dot(u_ref[...], log_1m.astype(BF16), preferred_element_type=F32))
        carry = [carry_ref[h] for h in range(HEAD_GROUP)]
        carry_in = []
        for (_, _, h), s in zip(chains, sums):
            carry_in.append(carry[h])
            carry[h] = carry[h] + s[BLK:BLK + SUBLANES]
        most = carry[0]
        for h in range(HEAD_GROUP):
            carry_ref[h] = carry[h]
            most = jnp.maximum(most, carry[h])
        go_ref[0] = (jnp.max(most) >= UNDERFLOW_LOG).astype(jnp.int32)
        for (start, kind, h), log_beta, s, c_in in zip(chains, log_betas, sums, carry_in):
            a = jnp.exp(log_beta + s[0:BLK] + c_in[0:1, :])
            if kind is not None:
                a = jnp.where(valid[kind], a, 0.0)
            vb = vt_ref[h * HEAD_DIM:(h + 1) * HEAD_DIM, pl.ds(start, BLK)]
            rows = slice((h % 2) * HEAD_DIM, (h % 2 + 1) * HEAD_DIM)
            acc_ref[h // 2, rows, :] += jnp.dot(vb, a.astype(BF16), preferred_element_type=F32)

    @pl.when(i >= 2)
    def _():
        sweep([(i, "causal"), (i - 1, None)], project_ahead=True)

    @pl.when(i < 2)
    def _():
        sweep([(i, "edge")], project_ahead=True)

    def interior(j):
        sweep([(j, None)])
        return j - 1

    j_end = lax.while_loop(lambda j: (j >= 1) & (go_ref[0] != 0), interior,
                           jnp.where(i >= 2, i - 2, i - 1))

    @pl.when((j_end == 0) & (go_ref[0] != 0))
    def _():
        sweep([(0, "edge")])

    for p in range(HEAD_GROUP // 2):
        o_ref[0, :, p * PAIR:(p + 1) * PAIR] = acc_ref[p].T.astype(BF16)


def _attention(xk, xq, w_k, w_vt, w_qt):
    gw = HEAD_GROUP * HEAD_DIM
    first_block = pl.BlockSpec((1, BLK, D_MODEL), lambda b, g, i: (b, 0, 0))
    next_block = pl.BlockSpec((1, BLK, D_MODEL), lambda b, g, i: (b, jnp.minimum(i + 1, N_BLK - 1), 0))
    return pl.pallas_call(
        _attn_kernel,
        out_shape=jax.ShapeDtypeStruct((BATCH, LP, D_MODEL), BF16),
        grid=(BATCH, N_HEADS // HEAD_GROUP, N_BLK),
        in_specs=[
            first_block, first_block, next_block, next_block,
            pl.BlockSpec((D_MODEL, gw), lambda b, g, i: (0, g)),
            pl.BlockSpec((gw, D_MODEL), lambda b, g, i: (g, 0)),
            pl.BlockSpec((gw, D_MODEL), lambda b, g, i: (g, 0)),
            pl.BlockSpec((BLK + SUBLANES, BLK), lambda b, g, i: (0, 0)),
        ],
        out_specs=pl.BlockSpec((1, BLK, gw), lambda b, g, i: (b, i, g)),
        scratch_shapes=[pltpu.VMEM((LP, gw), BF16),
                        pltpu.VMEM((gw, LP), BF16),
                        pltpu.VMEM((2, gw, BLK), BF16),
                        pltpu.VMEM((HEAD_GROUP, PAIR, BLK), BF16),
                        pltpu.VMEM((HEAD_GROUP // 2, PAIR, BLK), F32),
                        pltpu.VMEM((HEAD_GROUP, SUBLANES, BLK), F32),
                        pltpu.SMEM((1,), jnp.int32)],
        compiler_params=pltpu.CompilerParams(
            dimension_semantics=("arbitrary", "arbitrary", "arbitrary"), vmem_limit_bytes=VMEM_LIMIT),
        name="stickbreak_attn",
    )(xk, xq, xk, xq, w_k, w_vt, w_qt, _suffix_sum_matrix())


def kernel(x, meta_tokens, mix_norm, ffn_norm, pool_w, pool_scale, kv_norm, w_kv, w_q, w_o,
           ffn_w_up, ffn_conv_w, ffn_conv_b, ffn_w_down, final_norm):
    row = lambda v: v.reshape(1, -1)

    w_up, w_down = ffn_w_up.astype(BF16), ffn_w_down.astype(BF16)

    def ffn(h, layer, **mode):
        return _ffn_layer(h.reshape(ROWS, D_MODEL), layer, ffn_norm[:, None, :], w_up, ffn_conv_w,
                          ffn_conv_b[:, None, :], w_down, **mode)

    h = _pool_layer(x, meta_tokens.astype(x.dtype), row(mix_norm[0]), pool_w[0].astype(BF16),
                    row(pool_scale[0]))
    h, xk, xq = ffn(h, 0, qk_gains=(row(kv_norm), row(mix_norm[1])))

    w_k, w_v = w_kv[:, :D_MODEL], w_kv[:, D_MODEL:]
    by_batch = lambda a: a.reshape(BATCH, LP, D_MODEL)
    attn = _attention(by_batch(xk), by_batch(xq), w_k.astype(BF16), w_v.T.astype(BF16),
                      (w_q[0].T * (HEAD_DIM ** -0.5)).astype(BF16))
    return ffn(h, 1, attn_tail=(attn.reshape(ROWS, D_MODEL), w_o[0].astype(BF16), row(final_norm)))
```

```python
import functools

import numpy as np
import jax
import jax.numpy as jnp
from jax import lax
from jax.experimental import pallas as pl
from jax.experimental.pallas import tpu as pltpu

D_MODEL = 1024
BATCH = 4
SEQ = 4096
N_META = 16
POOL_WINDOWS = (2, 4, 8, 16)
POOL_GROUP_DIM = D_MODEL // len(POOL_WINDOWS)
N_HEADS = 16
HEAD_DIM = D_MODEL // N_HEADS
D_FF = 2816
CONV_WIDTH = 3
RMS_EPS = 1e-6

SUBLANES = 8
LANES = 128
MXU_DIM = 256

BLK = MXU_DIM
ROW_PAD = BLK - N_META
LP = SEQ + BLK
N_BLK = LP // BLK
ROWS = BATCH * LP
MAX_WINDOW = max(POOL_WINDOWS)
HEAD_GROUP = 8
PAIR = 2 * HEAD_DIM
UNDERFLOW_LOG = -104.0
LOG2_E = 1.4426950408889634

TM_POOL = BLK
TM_FFN = 1024
FFN_CHUNK = MXU_DIM
N_FFN_CHUNKS = D_FF // FFN_CHUNK
VMEM_LIMIT = 56 * 1024 * 1024

F32 = jnp.float32
BF16 = jnp.bfloat16


def _rms_scale(x):
    return lax.rsqrt(jnp.mean(x * x, axis=-1, keepdims=True) + RMS_EPS)


def _pool_kernel(x_ref, meta_ref, g_ref, w_ref, sc_ref, o_ref, buf_ref):
    j = pl.program_id(0)
    tm = TM_POOL

    @pl.when(j == 0)
    def _():
        buf_ref[:, 0:MAX_WINDOW, :] = jnp.zeros((BATCH, MAX_WINDOW, D_MODEL), F32)

    head_tile = jnp.concatenate([jnp.zeros((ROW_PAD, D_MODEL), F32), meta_ref[...]], axis=0)
    pos = j * tm + lax.broadcasted_iota(jnp.int32, (tm, 1), 0) - ROW_PAD
    xs, diffs = [], [[] for _ in POOL_WINDOWS]
    for b in range(BATCH):
        x = jnp.where(j == 0, head_tile, x_ref[b])
        xn = (x * _rms_scale(x)) * g_ref[...]
        buf_ref[b, MAX_WINDOW:MAX_WINDOW + tm, :] = xn
        win = buf_ref[b]
        shift = 1
        for g, w in enumerate(POOL_WINDOWS):
            assert w == 2 * shift
            win = win[:, (POOL_GROUP_DIM if g else 0):]
            win = win + pltpu.roll(win, shift, axis=0)
            count = jnp.clip(pos + 1, 1, w).astype(F32)
            mean = win[MAX_WINDOW:, 0:POOL_GROUP_DIM] / count
            diffs[g].append((mean - xn[:, g * POOL_GROUP_DIM:(g + 1) * POOL_GROUP_DIM]).astype(BF16))
            shift = w
        buf_ref[b, 0:MAX_WINDOW, :] = buf_ref[b, tm:tm + MAX_WINDOW, :]
        xs.append(x)
    y = jnp.concatenate([jnp.dot(jnp.concatenate(diffs[g], axis=0), w_ref[g], preferred_element_type=F32)
                         for g in range(len(POOL_WINDOWS))], axis=1) * sc_ref[...]
    for b in range(BATCH):
        o_ref[b] = jnp.where(pos >= 0, xs[b] + y[b * tm:(b + 1) * tm], 0.0)


def _pool_layer(x, meta, gain, w, scale):
    const2 = lambda j: (0, 0)
    return pl.pallas_call(
        _pool_kernel,
        out_shape=jax.ShapeDtypeStruct((BATCH, LP, D_MODEL), F32),
        grid=(LP // TM_POOL,),
        in_specs=[
            pl.BlockSpec((BATCH, TM_POOL, D_MODEL), lambda j: (0, jnp.maximum(j - 1, 0), 0)),
            pl.BlockSpec((N_META, D_MODEL), const2),
            pl.BlockSpec((1, D_MODEL), const2),
            pl.BlockSpec((len(POOL_WINDOWS), POOL_GROUP_DIM, POOL_GROUP_DIM), lambda j: (0, 0, 0)),
            pl.BlockSpec((1, D_MODEL), const2),
        ],
        out_specs=pl.BlockSpec((BATCH, TM_POOL, D_MODEL), lambda j: (0, j, 0)),
        scratch_shapes=[pltpu.VMEM((BATCH, MAX_WINDOW + TM_POOL, D_MODEL), F32)],
        compiler_params=pltpu.CompilerParams(dimension_semantics=("arbitrary",),
                                             vmem_limit_bytes=VMEM_LIMIT),
        name="pool_mixer",
    )(x, meta, gain, w, scale)


def _pad_row_mask(row0, tm):
    row = row0 + lax.broadcasted_iota(jnp.int32, (tm, 1), 0)
    is_pad = jnp.zeros((tm, 1), jnp.bool_)
    for b in range(BATCH):
        is_pad = is_pad | ((row >= b * LP) & (row < b * LP + ROW_PAD))
    return jnp.logical_not(is_pad)


def _ffn_kernel(*refs, attn_tail):
    if attn_tail:
        (h_ref, attn_ref, wo_ref, fg_ref, meta_up_ref, g_ref, wup_ref, cw_ref, cb_ref, wdn_ref,
         o_ref, *scratch) = refs
    else:
        (h_ref, gkv_ref, gq_ref, g_ref, wup_ref, cw_ref, cb_ref, wdn_ref,
         o_ref, xk_ref, xq_ref, *scratch) = refs
    carry_ref, xn_ref, act_ref = scratch
    tm = TM_FFN
    cw = 2 * FFN_CHUNK

    def chunk_cols(ref, c, rows=slice(None)):
        g0 = c * FFN_CHUNK
        return jnp.concatenate([ref[rows, g0:g0 + FFN_CHUNK],
                                ref[rows, D_FF + g0:D_FF + g0 + FFN_CHUNK]], axis=1)

    if attn_tail:
        @pl.when(pl.program_id(1) == 0)
        def _():
            for c in range(N_FFN_CHUNKS):
                carry_ref[c] = chunk_cols(meta_up_ref, c, slice(N_META - SUBLANES, N_META))
    else:
        i = pl.program_id(0)

        @pl.when(i == 0)
        def _():
            carry_ref[...] = jnp.zeros(carry_ref.shape, F32)

    x = h_ref[...]
    if attn_tail:
        x = x + jnp.dot(attn_ref[...], wo_ref[...], preferred_element_type=F32)
    xn_ref[...] = ((x * _rms_scale(x)) * g_ref[...]).astype(BF16)

    for c in range(N_FFN_CHUNKS):
        u = jnp.dot(xn_ref[...], chunk_cols(wup_ref, c), preferred_element_type=F32)
        ext = jnp.concatenate([carry_ref[c], u], axis=0)
        carry_ref[c] = u[tm - SUBLANES:tm, :]
        w = chunk_cols(cw_ref, c)
        cv = chunk_cols(cb_ref, c) + w[CONV_WIDTH - 1:CONV_WIDTH] * u
        for k in range(CONV_WIDTH - 1):
            first = SUBLANES - (CONV_WIDTH - 1) + k
            cv = cv + w[k:k + 1] * ext[first:first + tm, :]
        half_gate = 0.5 * cv[:, 0:FFN_CHUNK]
        act = (half_gate + half_gate * jnp.tanh(half_gate)) * cv[:, FFN_CHUNK:cw]
        act_ref[:, c * FFN_CHUNK:(c + 1) * FFN_CHUNK] = act.astype(BF16)

    out = x + jnp.dot(act_ref[...], wdn_ref[...], preferred_element_type=F32)
    if attn_tail:
        o_ref[...] = (out * _rms_scale(out)) * fg_ref[...]
    else:
        out = jnp.where(_pad_row_mask(i * tm, tm), out, 0.0)
        o_ref[...] = out
        normed = out * _rms_scale(out)
        xk_ref[...] = (normed * gkv_ref[...]).astype(BF16)
        xq_ref[...] = (normed * gq_ref[...]).astype(BF16)


def _meta_up_kernel(h_ref, attn_ref, wo_ref, g_ref, wup_ref, o_ref):
    x = h_ref[...] + jnp.dot(attn_ref[...], wo_ref[...], preferred_element_type=F32)
    xn = ((x * _rms_scale(x)) * g_ref[...]).astype(BF16)
    o_ref[...] = jnp.dot(xn, wup_ref[...], preferred_element_type=F32)


def _meta_up_projection(h, attn, w_o, layer, gain, w_up):
    const = lambda s: (0, 0)
    of_layer = lambda s: (layer, 0, 0)
    meta_rows = lambda a: a.reshape(BATCH, LP, D_MODEL)[:, ROW_PAD:ROW_PAD + N_META].reshape(
        BATCH * N_META, D_MODEL)
    all_meta = pl.BlockSpec((BATCH * N_META, D_MODEL), const)
    out = pl.pallas_call(
        _meta_up_kernel,
        out_shape=jax.ShapeDtypeStruct((BATCH * N_META, 2 * D_FF), F32),
        grid=(1,),
        in_specs=[all_meta, all_meta,
                  pl.BlockSpec((D_MODEL, D_MODEL), const, pipeline_mode=pl.Buffered(1)),
                  pl.BlockSpec((None, 1, D_MODEL), of_layer),
                  pl.BlockSpec((None, D_MODEL, 2 * D_FF), of_layer, pipeline_mode=pl.Buffered(1))],
        out_specs=pl.BlockSpec((BATCH * N_META, 2 * D_FF), const),
        compiler_params=pltpu.CompilerParams(dimension_semantics=("arbitrary",),
                                             vmem_limit_bytes=VMEM_LIMIT),
        name="meta_up_proj",
    )(meta_rows(h), meta_rows(attn), w_o, gain, w_up)
    return out.reshape(BATCH, N_META, 2 * D_FF)


def _ffn_layer(h, layer, gain, w_up, conv_w, conv_b, w_down, attn_tail=None, qk_gains=None):
    const = lambda *_: (0, 0)
    of_layer = lambda *_: (layer, 0, 0)
    if attn_tail is not None:
        attn, w_o, final_gain = attn_tail
        meta_up = _meta_up_projection(h, attn, w_o, layer, gain, w_up)
        grid = (BATCH, SEQ // TM_FFN)
        blk_per_tile = TM_FFN // BLK
        window = lambda b, k: ((b * N_BLK + 1 + k * blk_per_tile) * BLK, 0)
        row_tile = pl.BlockSpec((pl.Element(TM_FFN), pl.Element(D_MODEL)), window)
        operands = [h, attn, w_o, final_gain, meta_up]
        in_specs = [row_tile, row_tile,
                    pl.BlockSpec((D_MODEL, D_MODEL), const, pipeline_mode=pl.Buffered(1)),
                    pl.BlockSpec((1, D_MODEL), const),
                    pl.BlockSpec((None, N_META, 2 * D_FF), lambda b, k: (b, 0, 0))]
        out_shape = jax.ShapeDtypeStruct((BATCH, SEQ, D_MODEL), F32)
        out_specs = pl.BlockSpec((None, TM_FFN, D_MODEL), lambda b, k: (b, k, 0))
    else:
        grid = (ROWS // TM_FFN,)
        row_tile = pl.BlockSpec((TM_FFN, D_MODEL), lambda i: (i, 0))
        operands = [h] + list(qk_gains)
        in_specs = [row_tile] + [pl.BlockSpec((1, D_MODEL), const)] * 2
        out_shape = (jax.ShapeDtypeStruct((ROWS, D_MODEL), F32),) + (
            jax.ShapeDtypeStruct((ROWS, D_MODEL), BF16),) * 2
        out_specs = (row_tile,) * 3
    operands += [gain, w_up, conv_w, conv_b, w_down]
    in_specs += [
        pl.BlockSpec((None, 1, D_MODEL), of_layer),
        pl.BlockSpec((None, D_MODEL, 2 * D_FF), of_layer, pipeline_mode=pl.Buffered(1)),
        pl.BlockSpec((None, CONV_WIDTH, 2 * D_FF), of_layer),
        pl.BlockSpec((None, 1, 2 * D_FF), of_layer),
        pl.BlockSpec((None, D_FF, D_MODEL), of_layer, pipeline_mode=pl.Buffered(1)),
    ]
    return pl.pallas_call(
        functools.partial(_ffn_kernel, attn_tail=attn_tail is not None),
        out_shape=out_shape,
        grid=grid,
        in_specs=in_specs,
        out_specs=out_specs,
        scratch_shapes=[
            pltpu.VMEM((N_FFN_CHUNKS, SUBLANES, 2 * FFN_CHUNK), F32),
            pltpu.VMEM((TM_FFN, D_MODEL), BF16),
            pltpu.VMEM((TM_FFN, D_FF), BF16),
        ],
        compiler_params=pltpu.CompilerParams(dimension_semantics=("arbitrary",) * len(grid),
                                             vmem_limit_bytes=VMEM_LIMIT),
        name="conv_ffn" if attn_tail is None else "attn_out_conv_ffn_norm",
    )(*operands)


_NT_DIMS = (((1,), (1,)), ((), ()))


def _suffix_sum_matrix():
    s = np.arange(BLK + SUBLANES)[:, None]
    j = np.arange(BLK)[None, :]
    return jnp.asarray(np.where(s < BLK, j > s, True), BF16)


def _attn_kernel(xk0_ref, xq0_ref, xkn_ref, xqn_ref, wk_ref, wvt_ref, wqt_ref, u_ref, o_ref,
                 k_ref, vt_ref, qt_ref, qm_ref, acc_ref, carry_ref, go_ref):
    i = pl.program_id(2)

    def project(xk_ref, xq_ref, block, slot):
        rows = pl.ds(pl.multiple_of(block * BLK, BLK), BLK)
        xk = xk_ref[0]
        k_ref[rows, :] = jnp.dot(xk, wk_ref[...], preferred_element_type=F32).astype(BF16)
        vt_ref[:, rows] = lax.dot_general(wvt_ref[...], xk, _NT_DIMS,
                                          preferred_element_type=F32).astype(BF16)
        qt_ref[slot] = lax.dot_general(wqt_ref[...], xq_ref[0], _NT_DIMS,
                                       preferred_element_type=F32).astype(BF16)

    @pl.when(i == 0)
    def _():
        project(xk0_ref, xq0_ref, 0, 0)

    pair_row = lax.broadcasted_iota(jnp.int32, (PAIR, 1), 0)
    for h in range(HEAD_GROUP):
        first = (h % 2) * HEAD_DIM
        mine = (pair_row >= first) & (pair_row < first + HEAD_DIM)
        qp = qt_ref[lax.rem(i, 2), (h // 2) * PAIR:(h // 2 + 1) * PAIR, :]
        qm_ref[h] = jnp.where(mine, qp, jnp.zeros((), BF16))
    acc_ref[...] = jnp.zeros(acc_ref.shape, F32)
    carry_ref[...] = jnp.zeros(carry_ref.shape, F32)

    def sweep(blocks, project_ahead=False):
        row = lax.broadcasted_iota(jnp.int32, (BLK, BLK), 0)
        col = lax.broadcasted_iota(jnp.int32, (BLK, BLK), 1)
        valid = {None: None}
        for j, kind in blocks:
            if kind == "causal":
                valid[kind] = row < col
            elif kind == "edge":
                key = j * BLK + row
                valid[kind] = (key < i * BLK + col) & (key >= ROW_PAD)
        chains = [(pl.multiple_of(j * BLK, BLK), kind, h) for j, kind in blocks for h in range(HEAD_GROUP)]
        zs = [jnp.dot(k_ref[pl.ds(start, BLK), (h // 2) * PAIR:(h // 2 + 1) * PAIR], qm_ref[h],
                      preferred_element_type=F32) for start, _, h in chains]
        if project_ahead:
            project(xkn_ref, xqn_ref, jnp.minimum(i + 1, N_BLK - 1), lax.rem(i + 1, 2))

        log_betas, sums = [], []
        for (_, kind, h), z in zip(chains, zs):
            soft = jnp.log(1.0 + jnp.exp2(jnp.abs(z) * -LOG2_E))
            log_beta = jnp.minimum(z, 0.0) - soft
            log_1m = log_beta - z
            if kind is not None:
                log_1m = jnp.where(valid[kind], log_1m, 0.0)
            log_betas.append(log_beta)
            sums.append(jnp.dot(u_ref[...], log_1m.astype(BF16), preferred_element_type=F32))
        carry = [carry_ref[h] for h in range(HEAD_GROUP)]
        carry_in = []
        for (_, _, h), s in zip(chains, sums):
            carry_in.append(carry[h])
            carry[h] = carry[h] + s[BLK:BLK + SUBLANES]
        most = carry[0]
        for h in range(HEAD_GROUP):
            carry_ref[h] = carry[h]
            most = jnp.maximum(most, carry[h])
        go_ref[0] = (jnp.max(most) >= UNDERFLOW_LOG).astype(jnp.int32)
        for (start, kind, h), log_beta, s, c_in in zip(chains, log_betas, sums, carry_in):
            a = jnp.exp(log_beta + s[0:BLK] + c_in[0:1, :])
            if kind is not None:
                a = jnp.where(valid[kind], a, 0.0)
            vb = vt_ref[h * HEAD_DIM:(h + 1) * HEAD_DIM, pl.ds(start, BLK)]
            rows = slice((h % 2) * HEAD_DIM, (h % 2 + 1) * HEAD_DIM)
            acc_ref[h // 2, rows, :] += jnp.dot(vb, a.astype(BF16), preferred_element_type=F32)

    @pl.when(i >= 2)
    def _():
        sweep([(i, "causal"), (i - 1, None)], project_ahead=True)

    @pl.when(i < 2)
    def _():
        sweep([(i, "edge")], project_ahead=True)

    def interior(j):
        sweep([(j, None)])
        return j - 1

    j_end = lax.while_loop(lambda j: (j >= 1) & (go_ref[0] != 0), interior,
                           jnp.where(i >= 2, i - 2, i - 1))

    @pl.when((j_end == 0) & (go_ref[0] != 0))
    def _():
        sweep([(0, "edge")])

    for p in range(HEAD_GROUP // 2):
        o_ref[0, :, p * PAIR:(p + 1) * PAIR] = acc_ref[p].T.astype(BF16)


def _attention(xk, xq, w_k, w_vt, w_qt):
    gw = HEAD_GROUP * HEAD_DIM
    first_block = pl.BlockSpec((1, BLK, D_MODEL), lambda b, g, i: (b, 0, 0))
    next_block = pl.BlockSpec((1, BLK, D_MODEL), lambda b, g, i: (b, jnp.minimum(i + 1, N_BLK - 1), 0))
    return pl.pallas_call(
        _attn_kernel,
        out_shape=jax.ShapeDtypeStruct((BATCH, LP, D_MODEL), BF16),
        grid=(BATCH, N_HEADS // HEAD_GROUP, N_BLK),
        in_specs=[
            first_block, first_block, next_block, next_block,
            pl.BlockSpec((D_MODEL, gw), lambda b, g, i: (0, g)),
            pl.BlockSpec((gw, D_MODEL), lambda b, g, i: (g, 0)),
            pl.BlockSpec((gw, D_MODEL), lambda b, g, i: (g, 0)),
            pl.BlockSpec((BLK + SUBLANES, BLK), lambda b, g, i: (0, 0)),
        ],
        out_specs=pl.BlockSpec((1, BLK, gw), lambda b, g, i: (b, i, g)),
        scratch_shapes=[pltpu.VMEM((LP, gw), BF16),
                        pltpu.VMEM((gw, LP), BF16),
                        pltpu.VMEM((2, gw, BLK), BF16),
                        pltpu.VMEM((HEAD_GROUP, PAIR, BLK), BF16),
                        pltpu.VMEM((HEAD_GROUP // 2, PAIR, BLK), F32),
                        pltpu.VMEM((HEAD_GROUP, SUBLANES, BLK), F32),
                        pltpu.SMEM((1,), jnp.int32)],
        compiler_params=pltpu.CompilerParams(
            dimension_semantics=("arbitrary", "arbitrary", "arbitrary"), vmem_limit_bytes=VMEM_LIMIT),
        name="stickbreak_attn",
    )(xk, xq, xk, xq, w_k, w_vt, w_qt, _suffix_sum_matrix())


def kernel(x, meta_tokens, mix_norm, ffn_norm, pool_w, pool_scale, kv_norm, w_kv, w_q, w_o,
           ffn_w_up, ffn_conv_w, ffn_conv_b, ffn_w_down, final_norm):
    row = lambda v: v.reshape(1, -1)

    w_up, w_down = ffn_w_up.astype(BF16), ffn_w_down.astype(BF16)

    def ffn(h, layer, **mode):
        return _ffn_layer(h.reshape(ROWS, D_MODEL), layer, ffn_norm[:, None, :], w_up, ffn_conv_w,
                          ffn_conv_b[:, None, :], w_down, **mode)

    h = _pool_layer(x, meta_tokens.astype(x.dtype), row(mix_norm[0]), pool_w[0].astype(BF16),
                    row(pool_scale[0]))
    h, xk, xq = ffn(h, 0, qk_gains=(row(kv_norm), row(mix_norm[1])))

    w_k, w_v = w_kv[:, :D_MODEL], w_kv[:, D_MODEL:]
    by_batch = lambda a: a.reshape(BATCH, LP, D_MODEL)
    attn = _attention(by_batch(xk), by_batch(xq), w_k.astype(BF16), w_v.T.astype(BF16),
                      (w_q[0].T * (HEAD_DIM ** -0.5)).astype(BF16))
    return ffn(h, 1, attn_tail=(attn.reshape(ROWS, D_MODEL), w_o[0].astype(BF16), row(final_norm)))
```

```python
import functools

import numpy as np
import jax
import jax.numpy as jnp
from jax import lax
from jax.experimental import pallas as pl
from jax.experimental.pallas import tpu as pltpu

D_MODEL = 1024
BATCH = 4
SEQ = 4096
N_META = 16
POOL_WINDOWS = (2, 4, 8, 16)
POOL_GROUP_DIM = D_MODEL // len(POOL_WINDOWS)
N_HEADS = 16
HEAD_DIM = D_MODEL // N_HEADS
D_FF = 2816
CONV_WIDTH = 3
RMS_EPS = 1e-6

SUBLANES = 8
MXU_DIM = 256

BLK = MXU_DIM
ROW_PAD = BLK - N_META
LP = SEQ + BLK
N_BLK = LP // BLK
ROWS = BATCH * LP
MAX_WINDOW = max(POOL_WINDOWS)
HEAD_GROUP = 8
PAIR = 2 * HEAD_DIM
UNDERFLOW_LOG = -104.0
LOG2_E = 1.4426950408889634

TM_POOL = BLK
TM_FFN = 512
FFN_CHUNK = MXU_DIM
N_FFN_CHUNKS = D_FF // FFN_CHUNK
VMEM_LIMIT = 56 * 1024 * 1024

F32 = jnp.float32
BF16 = jnp.bfloat16


def _rms_scale(x):
    return lax.rsqrt(jnp.mean(x * x, axis=-1, keepdims=True) + RMS_EPS)


def _pool_kernel(x_ref, meta_ref, g_ref, w_ref, sc_ref, o_ref, buf_ref):
    j = pl.program_id(0)
    tm = TM_POOL

    @pl.when(j == 0)
    def _():
        buf_ref[:, 0:MAX_WINDOW, :] = jnp.zeros((BATCH, MAX_WINDOW, D_MODEL), F32)

    head_tile = jnp.concatenate([jnp.zeros((ROW_PAD, D_MODEL), F32), meta_ref[...]], axis=0)
    pos = j * tm + lax.broadcasted_iota(jnp.int32, (tm, 1), 0) - ROW_PAD
    xs, diffs = [], [[] for _ in POOL_WINDOWS]
    for b in range(BATCH):
        x = jnp.where(j == 0, head_tile, x_ref[b])
        xn = (x * _rms_scale(x)) * g_ref[...]
        buf_ref[b, MAX_WINDOW:MAX_WINDOW + tm, :] = xn
        win = buf_ref[b]
        shift = 1
        for g, w in enumerate(POOL_WINDOWS):
            assert w == 2 * shift
            win = win[:, (POOL_GROUP_DIM if g else 0):]
            win = win + pltpu.roll(win, shift, axis=0)
            count = jnp.clip(pos + 1, 1, w).astype(F32)
            mean = win[MAX_WINDOW:, 0:POOL_GROUP_DIM] / count
            diffs[g].append((mean - xn[:, g * POOL_GROUP_DIM:(g + 1) * POOL_GROUP_DIM]).astype(BF16))
            shift = w
        buf_ref[b, 0:MAX_WINDOW, :] = buf_ref[b, tm:tm + MAX_WINDOW, :]
        xs.append(x)
    y = jnp.concatenate([jnp.dot(jnp.concatenate(diffs[g], axis=0), w_ref[g], preferred_element_type=F32)
                         for g in range(len(POOL_WINDOWS))], axis=1) * sc_ref[...]
    for b in range(BATCH):
        o_ref[b] = jnp.where(pos >= 0, xs[b] + y[b * tm:(b + 1) * tm], 0.0)


def _pool_layer(x, meta, gain, w, scale):
    const2 = lambda j: (0, 0)
    return pl.pallas_call(
        _pool_kernel,
        out_shape=jax.ShapeDtypeStruct((BATCH, LP, D_MODEL), F32),
        grid=(LP // TM_POOL,),
        in_specs=[
            pl.BlockSpec((BATCH, TM_POOL, D_MODEL), lambda j: (0, jnp.maximum(j - 1, 0), 0)),
            pl.BlockSpec((N_META, D_MODEL), const2),
            pl.BlockSpec((1, D_MODEL), const2),
            pl.BlockSpec((len(POOL_WINDOWS), POOL_GROUP_DIM, POOL_GROUP_DIM), lambda j: (0, 0, 0)),
            pl.BlockSpec((1, D_MODEL), const2),
        ],
        out_specs=pl.BlockSpec((BATCH, TM_POOL, D_MODEL), lambda j: (0, j, 0)),
        scratch_shapes=[pltpu.VMEM((BATCH, MAX_WINDOW + TM_POOL, D_MODEL), F32)],
        compiler_params=pltpu.CompilerParams(dimension_semantics=("arbitrary",),
                                             vmem_limit_bytes=VMEM_LIMIT),
        name="pool_mixer",
    )(x, meta, gain, w, scale)


def _pad_row_mask(row0, tm):
    row = row0 + lax.broadcasted_iota(jnp.int32, (tm, 1), 0)
    is_pad = jnp.zeros((tm, 1), jnp.bool_)
    for b in range(BATCH):
        is_pad = is_pad | ((row >= b * LP) & (row < b * LP + ROW_PAD))
    return jnp.logical_not(is_pad)


def _ffn_kernel(*refs, attn_tail):
    if attn_tail:
        (h_ref, attn_ref, wo_ref, fg_ref, meta_up_ref, g_ref, wup_ref, cw_ref, cb_ref, wdn_ref,
         o_ref, *scratch) = refs
    else:
        (h_ref, gkv_ref, gq_ref, g_ref, wup_ref, cw_ref, cb_ref, wdn_ref,
         o_ref, xk_ref, xq_ref, *scratch) = refs
    carry_ref, xn_ref, act_ref = scratch
    tm = TM_FFN
    cw = 2 * FFN_CHUNK

    def chunk_cols(ref, c, rows=slice(None)):
        g0 = c * FFN_CHUNK
        return jnp.concatenate([ref[rows, g0:g0 + FFN_CHUNK],
                                ref[rows, D_FF + g0:D_FF + g0 + FFN_CHUNK]], axis=1)

    if attn_tail:
        @pl.when(pl.program_id(1) == 0)
        def _():
            for c in range(N_FFN_CHUNKS):
                carry_ref[c] = chunk_cols(meta_up_ref, c, slice(N_META - SUBLANES, N_META))
    else:
        i = pl.program_id(0)

        @pl.when(i == 0)
        def _():
            carry_ref[...] = jnp.zeros(carry_ref.shape, F32)

    x = h_ref[...]
    if attn_tail:
        x = x + jnp.dot(attn_ref[...], wo_ref[...], preferred_element_type=F32)
    xn_ref[...] = ((x * _rms_scale(x)) * g_ref[...]).astype(BF16)

    for c in range(N_FFN_CHUNKS):
        u = jnp.dot(xn_ref[...], chunk_cols(wup_ref, c), preferred_element_type=F32)
        ext = jnp.concatenate([carry_ref[c], u], axis=0)
        carry_ref[c] = u[tm - SUBLANES:tm, :]
        w = chunk_cols(cw_ref, c)
        cv = chunk_cols(cb_ref, c) + w[CONV_WIDTH - 1:CONV_WIDTH] * u
        for k in range(CONV_WIDTH - 1):
            first = SUBLANES - (CONV_WIDTH - 1) + k
            cv = cv + w[k:k + 1] * ext[first:first + tm, :]
        half_gate = 0.5 * cv[:, 0:FFN_CHUNK]
        act = (half_gate + half_gate * jnp.tanh(half_gate)) * cv[:, FFN_CHUNK:cw]
        act_ref[:, c * FFN_CHUNK:(c + 1) * FFN_CHUNK] = act.astype(BF16)

    out = x + jnp.dot(act_ref[...], wdn_ref[...], preferred_element_type=F32)
    if attn_tail:
        o_ref[...] = (out * _rms_scale(out)) * fg_ref[...]
    else:
        out = jnp.where(_pad_row_mask(i * tm, tm), out, 0.0)
        o_ref[...] = out
        normed = out * _rms_scale(out)
        xk_ref[...] = (normed * gkv_ref[...]).astype(BF16)
        xq_ref[...] = (normed * gq_ref[...]).astype(BF16)


def _meta_up_kernel(h_ref, attn_ref, wo_ref, g_ref, wup_ref, o_ref):
    x = h_ref[...] + jnp.dot(attn_ref[...], wo_ref[...], preferred_element_type=F32)
    xn = ((x * _rms_scale(x)) * g_ref[...]).astype(BF16)
    o_ref[...] = jnp.dot(xn, wup_ref[...], preferred_element_type=F32)


def _meta_up_projection(h, attn, w_o, layer, gain, w_up):
    const = lambda b: (0, 0)
    of_layer = lambda b: (layer, 0, 0)
    meta_rows = pl.BlockSpec((pl.Element(N_META), pl.Element(D_MODEL)),
                              lambda b: ((b * (LP // N_META) + ROW_PAD // N_META) * N_META, 0))
    return pl.pallas_call(
        _meta_up_kernel,
        out_shape=jax.ShapeDtypeStruct((BATCH, N_META, 2 * D_FF), F32),
        grid=(BATCH,),
        in_specs=[meta_rows, meta_rows,
                  pl.BlockSpec((D_MODEL, D_MODEL), const, pipeline_mode=pl.Buffered(1)),
                  pl.BlockSpec((None, 1, D_MODEL), of_layer),
                  pl.BlockSpec((None, D_MODEL, 2 * D_FF), of_layer, pipeline_mode=pl.Buffered(1))],
        out_specs=pl.BlockSpec((None, N_META, 2 * D_FF), lambda b: (b, 0, 0)),
        compiler_params=pltpu.CompilerParams(dimension_semantics=("arbitrary",),
                                             vmem_limit_bytes=VMEM_LIMIT),
        name="meta_up_proj",
    )(h, attn, w_o, gain, w_up)


def _ffn_layer(h, layer, gain, w_up, conv_w, conv_b, w_down, attn_tail=None, qk_gains=None):
    const = lambda *_: (0, 0)
    of_layer = lambda *_: (layer, 0, 0)
    if attn_tail is not None:
        attn, w_o, final_gain = attn_tail
        meta_up = _meta_up_projection(h, attn, w_o, layer, gain, w_up)
        grid = (BATCH, SEQ // TM_FFN)
        blk_per_tile = TM_FFN // BLK
        window = lambda b, k: ((b * N_BLK + 1 + k * blk_per_tile) * BLK, 0)
        row_tile = pl.BlockSpec((pl.Element(TM_FFN), pl.Element(D_MODEL)), window)
        operands = [h, attn, w_o, final_gain, meta_up]
        in_specs = [row_tile, row_tile,
                    pl.BlockSpec((D_MODEL, D_MODEL), const, pipeline_mode=pl.Buffered(1)),
                    pl.BlockSpec((1, D_MODEL), const),
                    pl.BlockSpec((None, N_META, 2 * D_FF), lambda b, k: (b, 0, 0))]
        out_shape = jax.ShapeDtypeStruct((BATCH, SEQ, D_MODEL), F32)
        out_specs = pl.BlockSpec((None, TM_FFN, D_MODEL), lambda b, k: (b, k, 0))
    else:
        grid = (ROWS // TM_FFN,)
        row_tile = pl.BlockSpec((TM_FFN, D_MODEL), lambda i: (i, 0))
        operands = [h] + list(qk_gains)
        in_specs = [row_tile] + [pl.BlockSpec((1, D_MODEL), const)] * 2
        out_shape = (jax.ShapeDtypeStruct((ROWS, D_MODEL), F32),) + (
            jax.ShapeDtypeStruct((ROWS, D_MODEL), BF16),) * 2
        out_specs = (row_tile,) * 3
    operands += [gain, w_up, conv_w, conv_b, w_down]
    in_specs += [
        pl.BlockSpec((None, 1, D_MODEL), of_layer),
        pl.BlockSpec((None, D_MODEL, 2 * D_FF), of_layer, pipeline_mode=pl.Buffered(1)),
        pl.BlockSpec((None, CONV_WIDTH, 2 * D_FF), of_layer),
        pl.BlockSpec((None, 1, 2 * D_FF), of_layer),
        pl.BlockSpec((None, D_FF, D_MODEL), of_layer, pipeline_mode=pl.Buffered(1)),
    ]
    return pl.pallas_call(
        functools.partial(_ffn_kernel, attn_tail=attn_tail is not None),
        out_shape=out_shape,
        grid=grid,
        in_specs=in_specs,
        out_specs=out_specs,
        scratch_shapes=[
            pltpu.VMEM((N_FFN_CHUNKS, SUBLANES, 2 * FFN_CHUNK), F32),
            pltpu.VMEM((TM_FFN, D_MODEL), BF16),
            pltpu.VMEM((TM_FFN, D_FF), BF16),
        ],
        compiler_params=pltpu.CompilerParams(dimension_semantics=("arbitrary",) * len(grid),
                                             vmem_limit_bytes=VMEM_LIMIT),
        name="conv_ffn" if attn_tail is None else "attn_out_conv_ffn_norm",
    )(*operands)


_NT_DIMS = (((1,), (1,)), ((), ()))


def _suffix_sum_matrix():
    s = np.arange(BLK + SUBLANES)[:, None]
    j = np.arange(BLK)[None, :]
    return jnp.asarray(np.where(s < BLK, j > s, True), BF16)


def _attn_kernel(xk0_ref, xq0_ref, xkn_ref, xqn_ref, wk_ref, wvt_ref, wqt_ref, u_ref, o_ref,
                 k_ref, vt_ref, qt_ref, qm_ref, acc_ref, carry_ref, go_ref):
    i = pl.program_id(2)

    def project(xk_ref, xq_ref, block, slot):
        rows = pl.ds(pl.multiple_of(block * BLK, BLK), BLK)
        xk = xk_ref[0]
        k_ref[rows, :] = jnp.dot(xk, wk_ref[...], preferred_element_type=F32).astype(BF16)
        vt_ref[:, rows] = lax.dot_general(wvt_ref[...], xk, _NT_DIMS,
                                          preferred_element_type=F32).astype(BF16)
        qt_ref[slot] = lax.dot_general(wqt_ref[...], xq_ref[0], _NT_DIMS,
                                       preferred_element_type=F32).astype(BF16)

    @pl.when(i == 0)
    def _():
        project(xk0_ref, xq0_ref, 0, 0)

    pair_row = lax.broadcasted_iota(jnp.int32, (PAIR, 1), 0)
    for h in range(HEAD_GROUP):
        first = (h % 2) * HEAD_DIM
        mine = (pair_row >= first) & (pair_row < first + HEAD_DIM)
        qp = qt_ref[lax.rem(i, 2), (h // 2) * PAIR:(h // 2 + 1) * PAIR, :]
        qm_ref[h] = jnp.where(mine, qp, jnp.zeros((), BF16))
    acc_ref[...] = jnp.zeros(acc_ref.shape, F32)
    carry_ref[...] = jnp.zeros(carry_ref.shape, F32)

    def sweep(blocks, project_ahead=False):
        row = lax.broadcasted_iota(jnp.int32, (BLK, BLK), 0)
        col = lax.broadcasted_iota(jnp.int32, (BLK, BLK), 1)
        valid = {None: None}
        for j, kind in blocks:
            if kind == "causal":
                valid[kind] = row < col
            elif kind == "edge":
                key = j * BLK + row
                valid[kind] = (key < i * BLK + col) & (key >= ROW_PAD)
        chains = [(pl.multiple_of(j * BLK, BLK), kind, h) for j, kind in blocks for h in range(HEAD_GROUP)]
        zs = [jnp.dot(k_ref[pl.ds(start, BLK), (h // 2) * PAIR:(h // 2 + 1) * PAIR], qm_ref[h],
                      preferred_element_type=F32) for start, _, h in chains]
        if project_ahead:
            project(xkn_ref, xqn_ref, jnp.minimum(i + 1, N_BLK - 1), lax.rem(i + 1, 2))

        log_betas, sums = [], []
        for (_, kind, h), z in zip(chains, zs):
            soft = jnp.log(1.0 + jnp.exp2(jnp.abs(z) * -LOG2_E))
            log_beta = jnp.minimum(z, 0.0) - soft
            log_1m = log_beta - z
            if kind is not None:
                log_1m = jnp.where(valid[kind], log_1m, 0.0)
            log_betas.append(log_beta)
            sums.append(jnp.dot(u_ref[...], log_1m.astype(BF16), preferred_element_type=F32))
        carry = [carry_ref[h] for h in range(HEAD_GROUP)]
        carry_in = []
        for (_, _, h), s in zip(chains, sums):
            carry_in.append(carry[h])
            carry[h] = carry[h] + s[BLK:BLK + SUBLANES]
        most = carry[0]
        for h in range(HEAD_GROUP):
            carry_ref[h] = carry[h]
            most = jnp.maximum(most, carry[h])
        go_ref[0] = (jnp.max(most) >= UNDERFLOW_LOG).astype(jnp.int32)
        for (start, kind, h), log_beta, s, c_in in zip(chains, log_betas, sums, carry_in):
            a = jnp.exp(log_beta + s[0:BLK] + c_in[0:1, :])
            if kind is not None:
                a = jnp.where(valid[kind], a, 0.0)
            vb = vt_ref[h * HEAD_DIM:(h + 1) * HEAD_DIM, pl.ds(start, BLK)]
            rows = slice((h % 2) * HEAD_DIM, (h % 2 + 1) * HEAD_DIM)
            acc_ref[h // 2, rows, :] += jnp.dot(vb, a.astype(BF16), preferred_element_type=F32)

    @pl.when(i >= 2)
    def _():
        sweep([(i, "causal"), (i - 1, None)], project_ahead=True)

    @pl.when(i < 2)
    def _():
        sweep([(i, "edge")], project_ahead=True)

    def interior(j):
        sweep([(j, None)])
        return j - 1

    j_end = lax.while_loop(lambda j: (j >= 1) & (go_ref[0] != 0), interior,
                           jnp.where(i >= 2, i - 2, i - 1))

    @pl.when((j_end == 0) & (go_ref[0] != 0))
    def _():
        sweep([(0, "edge")])

    for p in range(HEAD_GROUP // 2):
        o_ref[0, :, p * PAIR:(p + 1) * PAIR] = acc_ref[p].T.astype(BF16)


def _attention(xk, xq, w_k, w_vt, w_qt):
    gw = HEAD_GROUP * HEAD_DIM
    first_block = pl.BlockSpec((1, BLK, D_MODEL), lambda b, g, i: (b, 0, 0))
    next_block = pl.BlockSpec((1, BLK, D_MODEL), lambda b, g, i: (b, jnp.minimum(i + 1, N_BLK - 1), 0))
    return pl.pallas_call(
        _attn_kernel,
        out_shape=jax.ShapeDtypeStruct((BATCH, LP, D_MODEL), BF16),
        grid=(BATCH, N_HEADS // HEAD_GROUP, N_BLK),
        in_specs=[
            first_block, first_block, next_block, next_block,
            pl.BlockSpec((D_MODEL, gw), lambda b, g, i: (0, g)),
            pl.BlockSpec((gw, D_MODEL), lambda b, g, i: (g, 0)),
            pl.BlockSpec((gw, D_MODEL), lambda b, g, i: (g, 0)),
            pl.BlockSpec((BLK + SUBLANES, BLK), lambda b, g, i: (0, 0)),
        ],
        out_specs=pl.BlockSpec((1, BLK, gw), lambda b, g, i: (b, i, g)),
        scratch_shapes=[pltpu.VMEM((LP, gw), BF16),
                        pltpu.VMEM((gw, LP), BF16),
                        pltpu.VMEM((2, gw, BLK), BF16),
                        pltpu.VMEM((HEAD_GROUP, PAIR, BLK), BF16),
                        pltpu.VMEM((HEAD_GROUP // 2, PAIR, BLK), F32),
                        pltpu.VMEM((HEAD_GROUP, SUBLANES, BLK), F32),
                        pltpu.SMEM((1,), jnp.int32)],
        compiler_params=pltpu.CompilerParams(
            dimension_semantics=("arbitrary", "arbitrary", "arbitrary"), vmem_limit_bytes=VMEM_LIMIT),
        name="stickbreak_attn",
    )(xk, xq, xk, xq, w_k, w_vt, w_qt, _suffix_sum_matrix())


def kernel(x, meta_tokens, mix_norm, ffn_norm, pool_w, pool_scale, kv_norm, w_kv, w_q, w_o,
           ffn_w_up, ffn_conv_w, ffn_conv_b, ffn_w_down, final_norm):
    row = lambda v: v.reshape(1, -1)

    w_up, w_down = ffn_w_up.astype(BF16), ffn_w_down.astype(BF16)

    def ffn(h, layer, **mode):
        return _ffn_layer(h.reshape(ROWS, D_MODEL), layer, ffn_norm[:, None, :], w_up, ffn_conv_w,
                          ffn_conv_b[:, None, :], w_down, **mode)

    h = _pool_layer(x, meta_tokens.astype(x.dtype), row(mix_norm[0]), pool_w[0].astype(BF16),
                    row(pool_scale[0]))
    h, xk, xq = ffn(h, 0, qk_gains=(row(kv_norm), row(mix_norm[1])))

    w_k, w_v = w_kv[:, :D_MODEL], w_kv[:, D_MODEL:]
    by_batch = lambda a: a.reshape(BATCH, LP, D_MODEL)
    attn = _attention(by_batch(xk), by_batch(xq), w_k.astype(BF16), w_v.T.astype(BF16),
                      (w_q[0].T * (HEAD_DIM ** -0.5)).astype(BF16))
    return ffn(h, 1, attn_tail=(attn.reshape(ROWS, D_MODEL), w_o[0].astype(BF16), row(final_norm)))
```

```python
import functools

import numpy as np
import jax
import jax.numpy as jnp
from jax import lax
from jax.experimental import pallas as pl
from jax.experimental.pallas import tpu as pltpu

D_MODEL = 1024
BATCH = 4
SEQ = 4096
N_META = 16
POOL_WINDOWS = (2, 4, 8, 16)
POOL_GROUP_DIM = D_MODEL // len(POOL_WINDOWS)
N_HEADS = 16
HEAD_DIM = D_MODEL // N_HEADS
D_FF = 2816
CONV_WIDTH = 3
RMS_EPS = 1e-6

SUBLANES = 8
MXU_DIM = 256

BLK = MXU_DIM
ROW_PAD = BLK - N_META
LP = SEQ + BLK
N_BLK = LP // BLK
ROWS = BATCH * LP
MAX_WINDOW = max(POOL_WINDOWS)
HEAD_GROUP = 8
PAIR = 2 * HEAD_DIM
UNDERFLOW_LOG = -104.0
LOG2_E = 1.4426950408889634

TM_POOL = BLK
TM_FFN = 512
FFN_CHUNK = MXU_DIM
N_FFN_CHUNKS = D_FF // FFN_CHUNK
VMEM_LIMIT = 56 * 1024 * 1024

F32 = jnp.float32
BF16 = jnp.bfloat16


def _rms_scale(x):
    return lax.rsqrt(jnp.mean(x * x, axis=-1, keepdims=True) + RMS_EPS)


def _pool_kernel(x_ref, meta_ref, g_ref, w_ref, sc_ref, o_ref, buf_ref):
    j = pl.program_id(0)
    tm = TM_POOL

    @pl.when(j == 0)
    def _():
        buf_ref[:, 0:MAX_WINDOW, :] = jnp.zeros((BATCH, MAX_WINDOW, D_MODEL), F32)

    head_tile = jnp.concatenate([jnp.zeros((ROW_PAD, D_MODEL), F32), meta_ref[...]], axis=0)
    pos = j * tm + lax.broadcasted_iota(jnp.int32, (tm, 1), 0) - ROW_PAD
    xs, diffs = [], [[] for _ in POOL_WINDOWS]
    for b in range(BATCH):
        x = jnp.where(j == 0, head_tile, x_ref[b])
        xn = (x * _rms_scale(x)) * g_ref[...]
        buf_ref[b, MAX_WINDOW:MAX_WINDOW + tm, :] = xn
        win = buf_ref[b]
        shift = 1
        for g, w in enumerate(POOL_WINDOWS):
            assert w == 2 * shift
            win = win[:, (POOL_GROUP_DIM if g else 0):]
            win = win + pltpu.roll(win, shift, axis=0)
            count = jnp.clip(pos + 1, 1, w).astype(F32)
            mean = win[MAX_WINDOW:, 0:POOL_GROUP_DIM] / count
            diffs[g].append((mean - xn[:, g * POOL_GROUP_DIM:(g + 1) * POOL_GROUP_DIM]).astype(BF16))
            shift = w
        buf_ref[b, 0:MAX_WINDOW, :] = buf_ref[b, tm:tm + MAX_WINDOW, :]
        xs.append(x)
    y = jnp.concatenate([jnp.dot(jnp.concatenate(diffs[g], axis=0), w_ref[g], preferred_element_type=F32)
                         for g in range(len(POOL_WINDOWS))], axis=1) * sc_ref[...]
    for b in range(BATCH):
        o_ref[b] = jnp.where(pos >= 0, xs[b] + y[b * tm:(b + 1) * tm], 0.0)


def _pool_layer(x, meta, gain, w, scale):
    const2 = lambda j: (0, 0)
    return pl.pallas_call(
        _pool_kernel,
        out_shape=jax.ShapeDtypeStruct((BATCH, LP, D_MODEL), F32),
        grid=(LP // TM_POOL,),
        in_specs=[
            pl.BlockSpec((BATCH, TM_POOL, D_MODEL), lambda j: (0, jnp.maximum(j - 1, 0), 0)),
            pl.BlockSpec((N_META, D_MODEL), const2),
            pl.BlockSpec((1, D_MODEL), const2),
            pl.BlockSpec((len(POOL_WINDOWS), POOL_GROUP_DIM, POOL_GROUP_DIM), lambda j: (0, 0, 0)),
            pl.BlockSpec((1, D_MODEL), const2),
        ],
        out_specs=pl.BlockSpec((BATCH, TM_POOL, D_MODEL), lambda j: (0, j, 0)),
        scratch_shapes=[pltpu.VMEM((BATCH, MAX_WINDOW + TM_POOL, D_MODEL), F32)],
        compiler_params=pltpu.CompilerParams(dimension_semantics=("arbitrary",),
                                             vmem_limit_bytes=VMEM_LIMIT),
        name="pool_mixer",
    )(x, meta, gain, w, scale)


def _pad_row_mask(row0, tm):
    row = row0 + lax.broadcasted_iota(jnp.int32, (tm, 1), 0)
    is_pad = jnp.zeros((tm, 1), jnp.bool_)
    for b in range(BATCH):
        is_pad = is_pad | ((row >= b * LP) & (row < b * LP + ROW_PAD))
    return jnp.logical_not(is_pad)


def _ffn_kernel(*refs, attn_tail):
    if attn_tail:
        (h_ref, attn_ref, wo_ref, fg_ref, meta_up_ref, g_ref, wup_ref, cw_ref, cb_ref, wdn_ref,
         o_ref, *scratch) = refs
    else:
        (h_ref, gkv_ref, gq_ref, g_ref, wup_ref, cw_ref, cb_ref, wdn_ref,
         o_ref, xk_ref, xq_ref, *scratch) = refs
    carry_ref, xn_ref, act_ref = scratch
    tm = TM_FFN
    cw = 2 * FFN_CHUNK

    def chunk_cols(ref, c, rows=slice(None)):
        g0 = c * FFN_CHUNK
        return jnp.concatenate([ref[rows, g0:g0 + FFN_CHUNK],
                                ref[rows, D_FF + g0:D_FF + g0 + FFN_CHUNK]], axis=1)

    if attn_tail:
        @pl.when(pl.program_id(1) == 0)
        def _():
            for c in range(N_FFN_CHUNKS):
                carry_ref[c] = chunk_cols(meta_up_ref, c, slice(N_META - SUBLANES, N_META))
    else:
        i = pl.program_id(0)

        @pl.when(i == 0)
        def _():
            carry_ref[...] = jnp.zeros(carry_ref.shape, F32)

    x = h_ref[...]
    if attn_tail:
        x = x + jnp.dot(attn_ref[...], wo_ref[...], preferred_element_type=F32)
    xn_ref[...] = ((x * _rms_scale(x)) * g_ref[...]).astype(BF16)

    for c in range(N_FFN_CHUNKS):
        u = jnp.dot(xn_ref[...], chunk_cols(wup_ref, c), preferred_element_type=F32)
        ext = jnp.concatenate([carry_ref[c], u], axis=0)
        carry_ref[c] = u[tm - SUBLANES:tm, :]
        w = chunk_cols(cw_ref, c)
        cv = chunk_cols(cb_ref, c) + w[CONV_WIDTH - 1:CONV_WIDTH] * u
        for k in range(CONV_WIDTH - 1):
            first = SUBLANES - (CONV_WIDTH - 1) + k
            cv = cv + w[k:k + 1] * ext[first:first + tm, :]
        half_gate = 0.5 * cv[:, 0:FFN_CHUNK]
        act = (half_gate + half_gate * jnp.tanh(half_gate)) * cv[:, FFN_CHUNK:cw]
        act_ref[:, c * FFN_CHUNK:(c + 1) * FFN_CHUNK] = act.astype(BF16)

    out = x + jnp.dot(act_ref[...], wdn_ref[...], preferred_element_type=F32)
    if attn_tail:
        o_ref[...] = (out * _rms_scale(out)) * fg_ref[...]
    else:
        out = jnp.where(_pad_row_mask(i * tm, tm), out, 0.0)
        o_ref[...] = out
        normed = out * _rms_scale(out)
        xk_ref[...] = (normed * gkv_ref[...]).astype(BF16)
        xq_ref[...] = (normed * gq_ref[...]).astype(BF16)


def _meta_up_kernel(*refs):
    h_refs, attn_refs = refs[:BATCH], refs[BATCH:2 * BATCH]
    wo_ref, g_ref, wup_ref, o_ref = refs[2 * BATCH:]
    h = jnp.concatenate([r[...] for r in h_refs], axis=0)
    attn = jnp.concatenate([r[...] for r in attn_refs], axis=0)
    x = h + jnp.dot(attn, wo_ref[...], preferred_element_type=F32)
    xn = ((x * _rms_scale(x)) * g_ref[...]).astype(BF16)
    o_ref[...] = jnp.dot(xn, wup_ref[...], preferred_element_type=F32)


def _meta_up_projection(h, attn, w_o, layer, gain, w_up):
    const = lambda s: (0, 0)
    of_layer = lambda s: (layer, 0, 0)
    meta_rows = [pl.BlockSpec((pl.Element(N_META), pl.Element(D_MODEL)),
                              functools.partial(lambda b, s: (b * LP + ROW_PAD, 0), b))
                 for b in range(BATCH)]
    out = pl.pallas_call(
        _meta_up_kernel,
        out_shape=jax.ShapeDtypeStruct((BATCH * N_META, 2 * D_FF), F32),
        grid=(1,),
        in_specs=meta_rows + meta_rows + [
            pl.BlockSpec((D_MODEL, D_MODEL), const, pipeline_mode=pl.Buffered(1)),
            pl.BlockSpec((None, 1, D_MODEL), of_layer),
            pl.BlockSpec((None, D_MODEL, 2 * D_FF), of_layer, pipeline_mode=pl.Buffered(1))],
        out_specs=pl.BlockSpec((BATCH * N_META, 2 * D_FF), const),
        compiler_params=pltpu.CompilerParams(dimension_semantics=("arbitrary",),
                                             vmem_limit_bytes=VMEM_LIMIT),
        name="meta_up_proj",
    )(*([h] * BATCH), *([attn] * BATCH), w_o, gain, w_up)
    return out.reshape(BATCH, N_META, 2 * D_FF)


def _ffn_layer(h, layer, gain, w_up, conv_w, conv_b, w_down, attn_tail=None, qk_gains=None):
    const = lambda *_: (0, 0)
    of_layer = lambda *_: (layer, 0, 0)
    if attn_tail is not None:
        attn, w_o, final_gain = attn_tail
        meta_up = _meta_up_projection(h, attn, w_o, layer, gain, w_up)
        grid = (BATCH, SEQ // TM_FFN)
        blk_per_tile = TM_FFN // BLK
        window = lambda b, k: ((b * N_BLK + 1 + k * blk_per_tile) * BLK, 0)
        row_tile = pl.BlockSpec((pl.Element(TM_FFN), pl.Element(D_MODEL)), window)
        operands = [h, attn, w_o, final_gain, meta_up]
        in_specs = [row_tile, row_tile,
                    pl.BlockSpec((D_MODEL, D_MODEL), const, pipeline_mode=pl.Buffered(1)),
                    pl.BlockSpec((1, D_MODEL), const),
                    pl.BlockSpec((None, N_META, 2 * D_FF), lambda b, k: (b, 0, 0))]
        out_shape = jax.ShapeDtypeStruct((BATCH, SEQ, D_MODEL), F32)
        out_specs = pl.BlockSpec((None, TM_FFN, D_MODEL), lambda b, k: (b, k, 0))
    else:
        grid = (ROWS // TM_FFN,)
        row_tile = pl.BlockSpec((TM_FFN, D_MODEL), lambda i: (i, 0))
        operands = [h] + list(qk_gains)
        in_specs = [row_tile] + [pl.BlockSpec((1, D_MODEL), const)] * 2
        out_shape = (jax.ShapeDtypeStruct((ROWS, D_MODEL), F32),) + (
            jax.ShapeDtypeStruct((ROWS, D_MODEL), BF16),) * 2
        out_specs = (row_tile,) * 3
    operands += [gain, w_up, conv_w, conv_b, w_down]
    in_specs += [
        pl.BlockSpec((None, 1, D_MODEL), of_layer),
        pl.BlockSpec((None, D_MODEL, 2 * D_FF), of_layer, pipeline_mode=pl.Buffered(1)),
        pl.BlockSpec((None, CONV_WIDTH, 2 * D_FF), of_layer),
        pl.BlockSpec((None, 1, 2 * D_FF), of_layer),
        pl.BlockSpec((None, D_FF, D_MODEL), of_layer, pipeline_mode=pl.Buffered(1)),
    ]
    return pl.pallas_call(
        functools.partial(_ffn_kernel, attn_tail=attn_tail is not None),
        out_shape=out_shape,
        grid=grid,
        in_specs=in_specs,
        out_specs=out_specs,
        scratch_shapes=[
            pltpu.VMEM((N_FFN_CHUNKS, SUBLANES, 2 * FFN_CHUNK), F32),
            pltpu.VMEM((TM_FFN, D_MODEL), BF16),
            pltpu.VMEM((TM_FFN, D_FF), BF16),
        ],
        compiler_params=pltpu.CompilerParams(dimension_semantics=("arbitrary",) * len(grid),
                                             vmem_limit_bytes=VMEM_LIMIT),
        name="conv_ffn" if attn_tail is None else "attn_out_conv_ffn_norm",
    )(*operands)


_NT_DIMS = (((1,), (1,)), ((), ()))


def _suffix_sum_matrix():
    s = np.arange(BLK)[:, None]
    j = np.arange(BLK)[None, :]
    return jnp.asarray(j > s, BF16)


def _attn_kernel(xk0_ref, xq0_ref, xkn_ref, xqn_ref, wk_ref, wvt_ref, wqt_ref, u_ref, o_ref,
                 k_ref, vt_ref, qt_ref, qm_ref, acc_ref, carry_ref, go_ref):
    i = pl.program_id(2)

    def project(xk_ref, xq_ref, block, slot):
        rows = pl.ds(pl.multiple_of(block * BLK, BLK), BLK)
        xk = xk_ref[0]
        k_ref[rows, :] = jnp.dot(xk, wk_ref[...], preferred_element_type=F32).astype(BF16)
        vt_ref[:, rows] = lax.dot_general(wvt_ref[...], xk, _NT_DIMS,
                                          preferred_element_type=F32).astype(BF16)
        qt_ref[slot] = lax.dot_general(wqt_ref[...], xq_ref[0], _NT_DIMS,
                                       preferred_element_type=F32).astype(BF16)

    @pl.when(i == 0)
    def _():
        project(xk0_ref, xq0_ref, 0, 0)

    pair_row = lax.broadcasted_iota(jnp.int32, (PAIR, 1), 0)
    for h in range(HEAD_GROUP):
        first = (h % 2) * HEAD_DIM
        mine = (pair_row >= first) & (pair_row < first + HEAD_DIM)
        qp = qt_ref[lax.rem(i, 2), (h // 2) * PAIR:(h // 2 + 1) * PAIR, :]
        qm_ref[h] = jnp.where(mine, qp, jnp.zeros((), BF16))
    acc_ref[...] = jnp.zeros(acc_ref.shape, F32)
    carry_ref[...] = jnp.zeros(carry_ref.shape, F32)

    def sweep(blocks, project_ahead=False):
        row = lax.broadcasted_iota(jnp.int32, (BLK, BLK), 0)
        col = lax.broadcasted_iota(jnp.int32, (BLK, BLK), 1)
        valid = {None: None}
        for j, kind in blocks:
            if kind == "causal":
                valid[kind] = row < col
            elif kind == "edge":
                key = j * BLK + row
                valid[kind] = (key < i * BLK + col) & (key >= ROW_PAD)
        chains = [(pl.multiple_of(j * BLK, BLK), kind, h) for j, kind in blocks for h in range(HEAD_GROUP)]
        zs = [jnp.dot(k_ref[pl.ds(start, BLK), (h // 2) * PAIR:(h // 2 + 1) * PAIR], qm_ref[h],
                      preferred_element_type=F32) for start, _, h in chains]
        if project_ahead:
            project(xkn_ref, xqn_ref, jnp.minimum(i + 1, N_BLK - 1), lax.rem(i + 1, 2))

        log_betas, sums, totals = [], [], []
        for (_, kind, h), z in zip(chains, zs):
            soft = jnp.log(1.0 + jnp.exp2(jnp.abs(z) * -LOG2_E))
            log_beta = jnp.minimum(z, 0.0) - soft
            log_1m = log_beta - z
            if kind is not None:
                log_1m = jnp.where(valid[kind], log_1m, 0.0)
            log_betas.append(log_beta)
            rounded = log_1m.astype(BF16)
            s = jnp.dot(u_ref[...], rounded, preferred_element_type=F32)
            sums.append(s)
            totals.append(s[0:1, :] + rounded[0:1, :].astype(F32))
        carry = [carry_ref[h] for h in range(HEAD_GROUP)]
        carry_in = []
        for (_, _, h), total in zip(chains, totals):
            carry_in.append(carry[h])
            carry[h] = carry[h] + total
        most = carry[0]
        for h in range(HEAD_GROUP):
            carry_ref[h] = carry[h]
            most = jnp.maximum(most, carry[h])
        go_ref[0] = (jnp.max(most) >= UNDERFLOW_LOG).astype(jnp.int32)
        for (start, kind, h), log_beta, s, c_in in zip(chains, log_betas, sums, carry_in):
            a = jnp.exp(log_beta + s[0:BLK] + c_in[0:1, :])
            if kind is not None:
                a = jnp.where(valid[kind], a, 0.0)
            vb = vt_ref[h * HEAD_DIM:(h + 1) * HEAD_DIM, pl.ds(start, BLK)]
            rows = slice((h % 2) * HEAD_DIM, (h % 2 + 1) * HEAD_DIM)
            acc_ref[h // 2, rows, :] += jnp.dot(vb, a.astype(BF16), preferred_element_type=F32)

    @pl.when(i >= 2)
    def _():
        sweep([(i, "causal"), (i - 1, None)], project_ahead=True)

    @pl.when(i < 2)
    def _():
        sweep([(i, "edge")], project_ahead=True)

    def interior(j):
        sweep([(j, None)])
        return j - 1

    j_end = lax.while_loop(lambda j: (j >= 1) & (go_ref[0] != 0), interior,
                           jnp.where(i >= 2, i - 2, i - 1))

    @pl.when((j_end == 0) & (go_ref[0] != 0))
    def _():
        sweep([(0, "edge")])

    for p in range(HEAD_GROUP // 2):
        o_ref[0, :, p * PAIR:(p + 1) * PAIR] = acc_ref[p].T.astype(BF16)


def _attention(xk, xq, w_k, w_vt, w_qt):
    gw = HEAD_GROUP * HEAD_DIM
    first_block = pl.BlockSpec((1, BLK, D_MODEL), lambda b, g, i: (b, 0, 0))
    next_block = pl.BlockSpec((1, BLK, D_MODEL), lambda b, g, i: (b, jnp.minimum(i + 1, N_BLK - 1), 0))
    return pl.pallas_call(
        _attn_kernel,
        out_shape=jax.ShapeDtypeStruct((BATCH, LP, D_MODEL), BF16),
        grid=(BATCH, N_HEADS // HEAD_GROUP, N_BLK),
        in_specs=[
            first_block, first_block, next_block, next_block,
            pl.BlockSpec((D_MODEL, gw), lambda b, g, i: (0, g)),
            pl.BlockSpec((gw, D_MODEL), lambda b, g, i: (g, 0)),
            pl.BlockSpec((gw, D_MODEL), lambda b, g, i: (g, 0)),
            pl.BlockSpec((BLK, BLK), lambda b, g, i: (0, 0)),
        ],
        out_specs=pl.BlockSpec((1, BLK, gw), lambda b, g, i: (b, i, g)),
        scratch_shapes=[pltpu.VMEM((LP, gw), BF16),
                        pltpu.VMEM((gw, LP), BF16),
                        pltpu.VMEM((2, gw, BLK), BF16),
                        pltpu.VMEM((HEAD_GROUP, PAIR, BLK), BF16),
                        pltpu.VMEM((HEAD_GROUP // 2, PAIR, BLK), F32),
                        pltpu.VMEM((HEAD_GROUP, SUBLANES, BLK), F32),
                        pltpu.SMEM((1,), jnp.int32)],
        compiler_params=pltpu.CompilerParams(
            dimension_semantics=("arbitrary", "arbitrary", "arbitrary"), vmem_limit_bytes=VMEM_LIMIT),
        name="stickbreak_attn",
    )(xk, xq, xk, xq, w_k, w_vt, w_qt, _suffix_sum_matrix())


def kernel(x, meta_tokens, mix_norm, ffn_norm, pool_w, pool_scale, kv_norm, w_kv, w_q, w_o,
           ffn_w_up, ffn_conv_w, ffn_conv_b, ffn_w_down, final_norm):
    row = lambda v: v.reshape(1, -1)

    w_up, w_down = ffn_w_up.astype(BF16), ffn_w_down.astype(BF16)

    def ffn(h, layer, **mode):
        return _ffn_layer(h.reshape(ROWS, D_MODEL), layer, ffn_norm[:, None, :], w_up, ffn_conv_w,
                          ffn_conv_b[:, None, :], w_down, **mode)

    h = _pool_layer(x, meta_tokens.astype(x.dtype), row(mix_norm[0]), pool_w[0].astype(BF16),
                    row(pool_scale[0]))
    h, xk, xq = ffn(h, 0, qk_gains=(row(kv_norm), row(mix_norm[1])))

    w_k, w_v = w_kv[:, :D_MODEL], w_kv[:, D_MODEL:]
    by_batch = lambda a: a.reshape(BATCH, LP, D_MODEL)
    attn = _attention(by_batch(xk), by_batch(xq), w_k.astype(BF16), w_v.T.astype(BF16),
                      (w_q[0].T * (HEAD_DIM ** -0.5)).astype(BF16))
    return ffn(h, 1, attn_tail=(attn.reshape(ROWS, D_MODEL), w_o[0].astype(BF16), row(final_norm)))
```

```python
import functools

import numpy as np
import jax
import jax.numpy as jnp
from jax import lax
from jax.experimental import pallas as pl
from jax.experimental.pallas import tpu as pltpu

D_MODEL = 1024
BATCH = 4
SEQ = 4096
N_META = 16
POOL_WINDOWS = (2, 4, 8, 16)
POOL_GROUP_DIM = D_MODEL // len(POOL_WINDOWS)
N_HEADS = 16
HEAD_DIM = D_MODEL // N_HEADS
D_FF = 2816
CONV_WIDTH = 3
RMS_EPS = 1e-6

SUBLANES = 8
MXU_DIM = 256

BLK = MXU_DIM
ROW_PAD = BLK - N_META
LP = SEQ + BLK
N_BLK = LP // BLK
ROWS = BATCH * LP
MAX_WINDOW = max(POOL_WINDOWS)
HEAD_GROUP = 8
PAIR = 2 * HEAD_DIM
UNDERFLOW_LOG = -104.0
LOG2_E = 1.4426950408889634

TM_POOL = BLK
TM_FFN = 512
FFN_CHUNK = MXU_DIM
N_FFN_CHUNKS = D_FF // FFN_CHUNK
VMEM_LIMIT = 56 * 1024 * 1024

F32 = jnp.float32
BF16 = jnp.bfloat16


def _rms_scale(x):
    return lax.rsqrt(jnp.mean(x * x, axis=-1, keepdims=True) + RMS_EPS)


def _pool_kernel(x_ref, meta_ref, g_ref, w_ref, sc_ref, o_ref, buf_ref):
    j = pl.program_id(0)
    tm = TM_POOL

    @pl.when(j == 0)
    def _():
        buf_ref[:, 0:MAX_WINDOW, :] = jnp.zeros((BATCH, MAX_WINDOW, D_MODEL), F32)

    head_tile = jnp.concatenate([jnp.zeros((ROW_PAD, D_MODEL), F32), meta_ref[...]], axis=0)
    pos = j * tm + lax.broadcasted_iota(jnp.int32, (tm, 1), 0) - ROW_PAD
    xs, diffs = [], [[] for _ in POOL_WINDOWS]
    for b in range(BATCH):
        x = jnp.where(j == 0, head_tile, x_ref[b])
        xn = (x * _rms_scale(x)) * g_ref[...]
        buf_ref[b, MAX_WINDOW:MAX_WINDOW + tm, :] = xn
        win = buf_ref[b]
        shift = 1
        for g, w in enumerate(POOL_WINDOWS):
            assert w == 2 * shift
            win = win[:, (POOL_GROUP_DIM if g else 0):]
            win = win + pltpu.roll(win, shift, axis=0)
            count = jnp.clip(pos + 1, 1, w).astype(F32)
            mean = win[MAX_WINDOW:, 0:POOL_GROUP_DIM] / count
            diffs[g].append((mean - xn[:, g * POOL_GROUP_DIM:(g + 1) * POOL_GROUP_DIM]).astype(BF16))
            shift = w
        buf_ref[b, 0:MAX_WINDOW, :] = buf_ref[b, tm:tm + MAX_WINDOW, :]
        xs.append(x)
    y = jnp.concatenate([jnp.dot(jnp.concatenate(diffs[g], axis=0), w_ref[g], preferred_element_type=F32)
                         for g in range(len(POOL_WINDOWS))], axis=1) * sc_ref[...]
    for b in range(BATCH):
        o_ref[b] = jnp.where(pos >= 0, xs[b] + y[b * tm:(b + 1) * tm], 0.0)


def _pool_layer(x, meta, gain, w, scale):
    const2 = lambda j: (0, 0)
    return pl.pallas_call(
        _pool_kernel,
        out_shape=jax.ShapeDtypeStruct((BATCH, LP, D_MODEL), F32),
        grid=(LP // TM_POOL,),
        in_specs=[
            pl.BlockSpec((BATCH, TM_POOL, D_MODEL), lambda j: (0, jnp.maximum(j - 1, 0), 0)),
            pl.BlockSpec((N_META, D_MODEL), const2),
            pl.BlockSpec((1, D_MODEL), const2),
            pl.BlockSpec((len(POOL_WINDOWS), POOL_GROUP_DIM, POOL_GROUP_DIM), lambda j: (0, 0, 0)),
            pl.BlockSpec((1, D_MODEL), const2),
        ],
        out_specs=pl.BlockSpec((BATCH, TM_POOL, D_MODEL), lambda j: (0, j, 0)),
        scratch_shapes=[pltpu.VMEM((BATCH, MAX_WINDOW + TM_POOL, D_MODEL), F32)],
        compiler_params=pltpu.CompilerParams(dimension_semantics=("arbitrary",),
                                             vmem_limit_bytes=VMEM_LIMIT),
        name="pool_mixer",
    )(x, meta, gain, w, scale)


def _pad_row_mask(row0, tm):
    row = row0 + lax.broadcasted_iota(jnp.int32, (tm, 1), 0)
    is_pad = jnp.zeros((tm, 1), jnp.bool_)
    for b in range(BATCH):
        is_pad = is_pad | ((row >= b * LP) & (row < b * LP + ROW_PAD))
    return jnp.logical_not(is_pad)


def _ffn_kernel(*refs, attn_tail):
    if attn_tail:
        (h_ref, attn_ref, wo_ref, fg_ref, meta_up_ref, g_ref, wup_ref, cw_ref, cb_ref, wdn_ref,
         o_ref, *scratch) = refs
    else:
        (h_ref, gkv_ref, gq_ref, g_ref, wup_ref, cw_ref, cb_ref, wdn_ref,
         o_ref, xk_ref, xq_ref, *scratch) = refs
    carry_ref, xn_ref, act_ref, wdn_bf_ref = scratch
    tm = TM_FFN
    cw = 2 * FFN_CHUNK
    first_step = (pl.program_id(0) == 0) & (pl.program_id(1) == 0) if attn_tail else pl.program_id(0) == 0

    @pl.when(first_step)
    def _():
        wdn_bf_ref[...] = wdn_ref[...].astype(BF16)

    def chunk_cols(ref, c, rows=slice(None)):
        g0 = c * FFN_CHUNK
        return jnp.concatenate([ref[rows, g0:g0 + FFN_CHUNK],
                                ref[rows, D_FF + g0:D_FF + g0 + FFN_CHUNK]], axis=1)

    if attn_tail:
        @pl.when(pl.program_id(1) == 0)
        def _():
            for c in range(N_FFN_CHUNKS):
                carry_ref[c] = chunk_cols(meta_up_ref, c, slice(N_META - SUBLANES, N_META))
    else:
        i = pl.program_id(0)

        @pl.when(i == 0)
        def _():
            carry_ref[...] = jnp.zeros(carry_ref.shape, F32)

    x = h_ref[...]
    if attn_tail:
        x = x + jnp.dot(attn_ref[...], wo_ref[...], preferred_element_type=F32)
    xn_ref[...] = ((x * _rms_scale(x)) * g_ref[...]).astype(BF16)

    for c in range(N_FFN_CHUNKS):
        u = jnp.dot(xn_ref[...], chunk_cols(wup_ref, c), preferred_element_type=F32)
        ext = jnp.concatenate([carry_ref[c], u], axis=0)
        carry_ref[c] = u[tm - SUBLANES:tm, :]
        w = chunk_cols(cw_ref, c)
        cv = chunk_cols(cb_ref, c) + w[CONV_WIDTH - 1:CONV_WIDTH] * u
        for k in range(CONV_WIDTH - 1):
            first = SUBLANES - (CONV_WIDTH - 1) + k
            cv = cv + w[k:k + 1] * ext[first:first + tm, :]
        half_gate = 0.5 * cv[:, 0:FFN_CHUNK]
        act = (half_gate + half_gate * jnp.tanh(half_gate)) * cv[:, FFN_CHUNK:cw]
        act_ref[:, c * FFN_CHUNK:(c + 1) * FFN_CHUNK] = act.astype(BF16)

    out = x + jnp.dot(act_ref[...], wdn_bf_ref[...], preferred_element_type=F32)
    if attn_tail:
        o_ref[...] = (out * _rms_scale(out)) * fg_ref[...]
    else:
        out = jnp.where(_pad_row_mask(i * tm, tm), out, 0.0)
        o_ref[...] = out
        normed = out * _rms_scale(out)
        xk_ref[...] = (normed * gkv_ref[...]).astype(BF16)
        xq_ref[...] = (normed * gq_ref[...]).astype(BF16)


def _meta_up_kernel(*refs):
    h_refs, attn_refs = refs[:BATCH], refs[BATCH:2 * BATCH]
    wo_ref, g_ref, wup_ref, o_ref = refs[2 * BATCH:]
    h = jnp.concatenate([r[...] for r in h_refs], axis=0)
    attn = jnp.concatenate([r[...] for r in attn_refs], axis=0)
    x = h + jnp.dot(attn, wo_ref[...], preferred_element_type=F32)
    xn = ((x * _rms_scale(x)) * g_ref[...]).astype(BF16)
    o_ref[...] = jnp.dot(xn, wup_ref[...], preferred_element_type=F32)


def _meta_up_projection(h, attn, w_o, layer, gain, w_up):
    const = lambda s: (0, 0)
    of_layer = lambda s: (layer, 0, 0)
    meta_rows = [pl.BlockSpec((pl.Element(N_META), pl.Element(D_MODEL)),
                              functools.partial(lambda b, s: (b * LP + ROW_PAD, 0), b))
                 for b in range(BATCH)]
    out = pl.pallas_call(
        _meta_up_kernel,
        out_shape=jax.ShapeDtypeStruct((BATCH * N_META, 2 * D_FF), F32),
        grid=(1,),
        in_specs=meta_rows + meta_rows + [
            pl.BlockSpec((D_MODEL, D_MODEL), const, pipeline_mode=pl.Buffered(1)),
            pl.BlockSpec((None, 1, D_MODEL), of_layer),
            pl.BlockSpec((None, D_MODEL, 2 * D_FF), of_layer, pipeline_mode=pl.Buffered(1))],
        out_specs=pl.BlockSpec((BATCH * N_META, 2 * D_FF), const),
        compiler_params=pltpu.CompilerParams(dimension_semantics=("arbitrary",),
                                             vmem_limit_bytes=VMEM_LIMIT),
        name="meta_up_proj",
    )(*([h] * BATCH), *([attn] * BATCH), w_o, gain, w_up)
    return out.reshape(BATCH, N_META, 2 * D_FF)


def _ffn_layer(h, layer, gain, w_up, conv_w, conv_b, w_down, attn_tail=None, qk_gains=None):
    const = lambda *_: (0, 0)
    of_layer = lambda *_: (layer, 0, 0)
    if attn_tail is not None:
        attn, w_o, final_gain = attn_tail
        meta_up = _meta_up_projection(h, attn, w_o, layer, gain, w_up)
        grid = (BATCH, SEQ // TM_FFN)
        blk_per_tile = TM_FFN // BLK
        window = lambda b, k: ((b * N_BLK + 1 + k * blk_per_tile) * BLK, 0)
        row_tile = pl.BlockSpec((pl.Element(TM_FFN), pl.Element(D_MODEL)), window)
        operands = [h, attn, w_o, final_gain, meta_up]
        in_specs = [row_tile, row_tile,
                    pl.BlockSpec((D_MODEL, D_MODEL), const, pipeline_mode=pl.Buffered(1)),
                    pl.BlockSpec((1, D_MODEL), const),
                    pl.BlockSpec((None, N_META, 2 * D_FF), lambda b, k: (b, 0, 0))]
        out_shape = jax.ShapeDtypeStruct((BATCH, SEQ, D_MODEL), F32)
        out_specs = pl.BlockSpec((None, TM_FFN, D_MODEL), lambda b, k: (b, k, 0))
    else:
        grid = (ROWS // TM_FFN,)
        row_tile = pl.BlockSpec((TM_FFN, D_MODEL), lambda i: (i, 0))
        operands = [h] + list(qk_gains)
        in_specs = [row_tile] + [pl.BlockSpec((1, D_MODEL), const)] * 2
        out_shape = (jax.ShapeDtypeStruct((ROWS, D_MODEL), F32),) + (
            jax.ShapeDtypeStruct((ROWS, D_MODEL), BF16),) * 2
        out_specs = (row_tile,) * 3
    operands += [gain, w_up, conv_w, conv_b, w_down]
    in_specs += [
        pl.BlockSpec((None, 1, D_MODEL), of_layer),
        pl.BlockSpec((None, D_MODEL, 2 * D_FF), of_layer, pipeline_mode=pl.Buffered(1)),
        pl.BlockSpec((None, CONV_WIDTH, 2 * D_FF), of_layer),
        pl.BlockSpec((None, 1, 2 * D_FF), of_layer),
        pl.BlockSpec((None, D_FF, D_MODEL), of_layer, pipeline_mode=pl.Buffered(1)),
    ]
    return pl.pallas_call(
        functools.partial(_ffn_kernel, attn_tail=attn_tail is not None),
        out_shape=out_shape,
        grid=grid,
        in_specs=in_specs,
        out_specs=out_specs,
        scratch_shapes=[
            pltpu.VMEM((N_FFN_CHUNKS, SUBLANES, 2 * FFN_CHUNK), F32),
            pltpu.VMEM((TM_FFN, D_MODEL), BF16),
            pltpu.VMEM((TM_FFN, D_FF), BF16),
            pltpu.VMEM((D_FF, D_MODEL), BF16),
        ],
        compiler_params=pltpu.CompilerParams(dimension_semantics=("arbitrary",) * len(grid),
                                             vmem_limit_bytes=VMEM_LIMIT),
        name="conv_ffn" if attn_tail is None else "attn_out_conv_ffn_norm",
    )(*operands)


_NT_DIMS = (((1,), (1,)), ((), ()))


def _suffix_sum_matrix():
    s = np.arange(BLK)[:, None]
    j = np.arange(BLK)[None, :]
    return jnp.asarray(j > s, BF16)


def _attn_kernel(xk0_ref, xq0_ref, xkn_ref, xqn_ref, wk_ref, wvt_ref, wqt_ref, u_ref, o_ref,
                 k_ref, vt_ref, qt_ref, qm_ref, acc_ref, carry_ref, go_ref):
    i = pl.program_id(2)

    def project(xk_ref, xq_ref, block, slot):
        rows = pl.ds(pl.multiple_of(block * BLK, BLK), BLK)
        xk = xk_ref[0]
        k_ref[rows, :] = jnp.dot(xk, wk_ref[...], preferred_element_type=F32).astype(BF16)
        vt_ref[:, rows] = lax.dot_general(wvt_ref[...], xk, _NT_DIMS,
                                          preferred_element_type=F32).astype(BF16)
        qt_ref[slot] = lax.dot_general(wqt_ref[...], xq_ref[0], _NT_DIMS,
                                       preferred_element_type=F32).astype(BF16)

    @pl.when(i == 0)
    def _():
        project(xk0_ref, xq0_ref, 0, 0)

    pair_row = lax.broadcasted_iota(jnp.int32, (PAIR, 1), 0)
    for h in range(HEAD_GROUP):
        first = (h % 2) * HEAD_DIM
        mine = (pair_row >= first) & (pair_row < first + HEAD_DIM)
        qp = qt_ref[lax.rem(i, 2), (h // 2) * PAIR:(h // 2 + 1) * PAIR, :]
        qm_ref[h] = jnp.where(mine, qp, jnp.zeros((), BF16))
    acc_ref[...] = jnp.zeros(acc_ref.shape, F32)
    carry_ref[...] = jnp.zeros(carry_ref.shape, F32)

    def sweep(blocks, project_ahead=False):
        row = lax.broadcasted_iota(jnp.int32, (BLK, BLK), 0)
        col = lax.broadcasted_iota(jnp.int32, (BLK, BLK), 1)
        valid = {None: None}
        for j, kind in blocks:
            if kind == "causal":
                valid[kind] = row < col
            elif kind == "edge":
                key = j * BLK + row
                valid[kind] = (key < i * BLK + col) & (key >= ROW_PAD)
        chains = [(pl.multiple_of(j * BLK, BLK), kind, h) for j, kind in blocks for h in range(HEAD_GROUP)]
        zs = [jnp.dot(k_ref[pl.ds(start, BLK), (h // 2) * PAIR:(h // 2 + 1) * PAIR], qm_ref[h],
                      preferred_element_type=F32) for start, _, h in chains]
        if project_ahead:
            project(xkn_ref, xqn_ref, jnp.minimum(i + 1, N_BLK - 1), lax.rem(i + 1, 2))

        log_betas, sums, totals = [], [], []
        for (_, kind, h), z in zip(chains, zs):
            soft = jnp.log(1.0 + jnp.exp2(jnp.abs(z) * -LOG2_E))
            log_beta = jnp.minimum(z, 0.0) - soft
            log_1m = log_beta - z
            if kind is not None:
                log_1m = jnp.where(valid[kind], log_1m, 0.0)
            log_betas.append(log_beta)
            rounded = log_1m.astype(BF16)
            s = jnp.dot(u_ref[...], rounded, preferred_element_type=F32)
            sums.append(s)
            totals.append(s[0:1, :] + rounded[0:1, :].astype(F32))
        carry = [carry_ref[h] for h in range(HEAD_GROUP)]
        carry_in = []
        for (_, _, h), total in zip(chains, totals):
            carry_in.append(carry[h])
            carry[h] = carry[h] + total
        most = carry[0]
        for h in range(HEAD_GROUP):
            carry_ref[h] = carry[h]
            most = jnp.maximum(most, carry[h])
        go_ref[0] = (jnp.max(most) >= UNDERFLOW_LOG).astype(jnp.int32)
        for (start, kind, h), log_beta, s, c_in in zip(chains, log_betas, sums, carry_in):
            a = jnp.exp(log_beta + s[0:BLK] + c_in[0:1, :])
            if kind is not None:
                a = jnp.where(valid[kind], a, 0.0)
            vb = vt_ref[h * HEAD_DIM:(h + 1) * HEAD_DIM, pl.ds(start, BLK)]
            rows = slice((h % 2) * HEAD_DIM, (h % 2 + 1) * HEAD_DIM)
            acc_ref[h // 2, rows, :] += jnp.dot(vb, a.astype(BF16), preferred_element_type=F32)

    @pl.when(i >= 2)
    def _():
        sweep([(i, "causal"), (i - 1, None)], project_ahead=True)

    @pl.when(i < 2)
    def _():
        sweep([(i, "edge")], project_ahead=True)

    def interior(j):
        sweep([(j, None)])
        return j - 1

    j_end = lax.while_loop(lambda j: (j >= 1) & (go_ref[0] != 0), interior,
                           jnp.where(i >= 2, i - 2, i - 1))

    @pl.when((j_end == 0) & (go_ref[0] != 0))
    def _():
        sweep([(0, "edge")])

    for p in range(HEAD_GROUP // 2):
        o_ref[0, :, p * PAIR:(p + 1) * PAIR] = acc_ref[p].T.astype(BF16)


def _attention(xk, xq, w_k, w_vt, w_qt):
    gw = HEAD_GROUP * HEAD_DIM
    first_block = pl.BlockSpec((1, BLK, D_MODEL), lambda b, g, i: (b, 0, 0))
    next_block = pl.BlockSpec((1, BLK, D_MODEL), lambda b, g, i: (b, jnp.minimum(i + 1, N_BLK - 1), 0))
    return pl.pallas_call(
        _attn_kernel,
        out_shape=jax.ShapeDtypeStruct((BATCH, LP, D_MODEL), BF16),
        grid=(BATCH, N_HEADS // HEAD_GROUP, N_BLK),
        in_specs=[
            first_block, first_block, next_block, next_block,
            pl.BlockSpec((D_MODEL, gw), lambda b, g, i: (0, g)),
            pl.BlockSpec((gw, D_MODEL), lambda b, g, i: (g, 0)),
            pl.BlockSpec((gw, D_MODEL), lambda b, g, i: (g, 0)),
            pl.BlockSpec((BLK, BLK), lambda b, g, i: (0, 0)),
        ],
        out_specs=pl.BlockSpec((1, BLK, gw), lambda b, g, i: (b, i, g)),
        scratch_shapes=[pltpu.VMEM((LP, gw), BF16),
                        pltpu.VMEM((gw, LP), BF16),
                        pltpu.VMEM((2, gw, BLK), BF16),
                        pltpu.VMEM((HEAD_GROUP, PAIR, BLK), BF16),
                        pltpu.VMEM((HEAD_GROUP // 2, PAIR, BLK), F32),
                        pltpu.VMEM((HEAD_GROUP, SUBLANES, BLK), F32),
                        pltpu.SMEM((1,), jnp.int32)],
        compiler_params=pltpu.CompilerParams(
            dimension_semantics=("arbitrary", "arbitrary", "arbitrary"), vmem_limit_bytes=VMEM_LIMIT),
        name="stickbreak_attn",
    )(xk, xq, xk, xq, w_k, w_vt, w_qt, _suffix_sum_matrix())


def kernel(x, meta_tokens, mix_norm, ffn_norm, pool_w, pool_scale, kv_norm, w_kv, w_q, w_o,
           ffn_w_up, ffn_conv_w, ffn_conv_b, ffn_w_down, final_norm):
    row = lambda v: v.reshape(1, -1)

    w_up = ffn_w_up.astype(BF16)

    def ffn(h, layer, **mode):
        return _ffn_layer(h.reshape(ROWS, D_MODEL), layer, ffn_norm[:, None, :], w_up, ffn_conv_w,
                          ffn_conv_b[:, None, :], ffn_w_down, **mode)

    h = _pool_layer(x, meta_tokens.astype(x.dtype), row(mix_norm[0]), pool_w[0].astype(BF16),
                    row(pool_scale[0]))
    h, xk, xq = ffn(h, 0, qk_gains=(row(kv_norm), row(mix_norm[1])))

    w_k, w_v = w_kv[:, :D_MODEL], w_kv[:, D_MODEL:]
    by_batch = lambda a: a.reshape(BATCH, LP, D_MODEL)
    attn = _attention(by_batch(xk), by_batch(xq), w_k.astype(BF16), w_v.T.astype(BF16),
                      (w_q[0].T * (HEAD_DIM ** -0.5)).astype(BF16))
    return ffn(h, 1, attn_tail=(attn.reshape(ROWS, D_MODEL), w_o[0].astype(BF16), row(final_norm)))
```

```python
import functools

import numpy as np
import jax
import jax.numpy as jnp
from jax import lax
from jax.experimental import pallas as pl
from jax.experimental.pallas import tpu as pltpu

D_MODEL = 1024
BATCH = 4
SEQ = 4096
N_META = 16
POOL_WINDOWS = (2, 4, 8, 16)
POOL_GROUP_DIM = D_MODEL // len(POOL_WINDOWS)
N_HEADS = 16
HEAD_DIM = D_MODEL // N_HEADS
D_FF = 2816
CONV_WIDTH = 3
RMS_EPS = 1e-6

SUBLANES = 8
MXU_DIM = 256

BLK = MXU_DIM
ROW_PAD = BLK - N_META
LP = SEQ + BLK
N_BLK = LP // BLK
ROWS = BATCH * LP
MAX_WINDOW = max(POOL_WINDOWS)
HEAD_GROUP = 8
PAIR = 2 * HEAD_DIM
UNDERFLOW_LOG = -104.0
LOG2_E = 1.4426950408889634

TM_POOL = BLK
TM_FFN = 512
FFN_CHUNK = MXU_DIM
N_FFN_CHUNKS = D_FF // FFN_CHUNK
VMEM_LIMIT = 56 * 1024 * 1024

F32 = jnp.float32
BF16 = jnp.bfloat16


def _rms_scale(x):
    return lax.rsqrt(jnp.mean(x * x, axis=-1, keepdims=True) + RMS_EPS)


def _pool_kernel(x_ref, meta_ref, g_ref, w_ref, sc_ref, o_ref, buf_ref):
    j = pl.program_id(0)
    tm = TM_POOL

    @pl.when(j == 0)
    def _():
        buf_ref[:, 0:MAX_WINDOW, :] = jnp.zeros((BATCH, MAX_WINDOW, D_MODEL), F32)

    head_tile = jnp.concatenate([jnp.zeros((ROW_PAD, D_MODEL), F32), meta_ref[...]], axis=0)
    pos = j * tm + lax.broadcasted_iota(jnp.int32, (tm, 1), 0) - ROW_PAD
    xs, diffs = [], [[] for _ in POOL_WINDOWS]
    for b in range(BATCH):
        x = jnp.where(j == 0, head_tile, x_ref[b])
        xn = (x * _rms_scale(x)) * g_ref[...]
        buf_ref[b, MAX_WINDOW:MAX_WINDOW + tm, :] = xn
        win = buf_ref[b]
        shift = 1
        for g, w in enumerate(POOL_WINDOWS):
            assert w == 2 * shift
            win = win[:, (POOL_GROUP_DIM if g else 0):]
            win = win + pltpu.roll(win, shift, axis=0)
            count = jnp.clip(pos + 1, 1, w).astype(F32)
            mean = win[MAX_WINDOW:, 0:POOL_GROUP_DIM] / count
            diffs[g].append((mean - xn[:, g * POOL_GROUP_DIM:(g + 1) * POOL_GROUP_DIM]).astype(BF16))
            shift = w
        buf_ref[b, 0:MAX_WINDOW, :] = buf_ref[b, tm:tm + MAX_WINDOW, :]
        xs.append(x)
    y = jnp.concatenate([jnp.dot(jnp.concatenate(diffs[g], axis=0), w_ref[g], preferred_element_type=F32)
                         for g in range(len(POOL_WINDOWS))], axis=1) * sc_ref[...]
    for b in range(BATCH):
        o_ref[b] = jnp.where(pos >= 0, xs[b] + y[b * tm:(b + 1) * tm], 0.0)


def _pool_layer(x, meta, gain, w, scale):
    const2 = lambda j: (0, 0)
    return pl.pallas_call(
        _pool_kernel,
        out_shape=jax.ShapeDtypeStruct((BATCH, LP, D_MODEL), F32),
        grid=(LP // TM_POOL,),
        in_specs=[
            pl.BlockSpec((BATCH, TM_POOL, D_MODEL), lambda j: (0, jnp.maximum(j - 1, 0), 0)),
            pl.BlockSpec((N_META, D_MODEL), const2),
            pl.BlockSpec((1, D_MODEL), const2),
            pl.BlockSpec((len(POOL_WINDOWS), POOL_GROUP_DIM, POOL_GROUP_DIM), lambda j: (0, 0, 0)),
            pl.BlockSpec((1, D_MODEL), const2),
        ],
        out_specs=pl.BlockSpec((BATCH, TM_POOL, D_MODEL), lambda j: (0, j, 0)),
        scratch_shapes=[pltpu.VMEM((BATCH, MAX_WINDOW + TM_POOL, D_MODEL), F32)],
        compiler_params=pltpu.CompilerParams(dimension_semantics=("arbitrary",),
                                             vmem_limit_bytes=VMEM_LIMIT),
        name="pool_mixer",
    )(x, meta, gain, w, scale)


def _pad_row_mask(row0, tm):
    row = row0 + lax.broadcasted_iota(jnp.int32, (tm, 1), 0)
    is_pad = jnp.zeros((tm, 1), jnp.bool_)
    for b in range(BATCH):
        is_pad = is_pad | ((row >= b * LP) & (row < b * LP + ROW_PAD))
    return jnp.logical_not(is_pad)


def _ffn_kernel(*refs, attn_tail):
    if attn_tail:
        (h_ref, attn_ref, wo_ref, fg_ref, meta_up_ref, g_ref, wup_ref, cw_ref, cb_ref, wdn_ref,
         o_ref, *scratch) = refs
    else:
        (h_ref, gkv_ref, gq_ref, g_ref, wup_ref, cw_ref, cb_ref, wdn_ref,
         o_ref, xk_ref, xq_ref, *scratch) = refs
    carry_ref, xn_ref, act_ref, wdn_bf_ref = scratch
    tm = TM_FFN
    cw = 2 * FFN_CHUNK
    first_step = (pl.program_id(0) == 0) & (pl.program_id(1) == 0) if attn_tail else pl.program_id(0) == 0

    @pl.when(first_step)
    def _():
        wdn_bf_ref[...] = wdn_ref[...].astype(BF16)

    def chunk_cols(ref, c, rows=slice(None)):
        g0 = c * FFN_CHUNK
        return jnp.concatenate([ref[rows, g0:g0 + FFN_CHUNK],
                                ref[rows, D_FF + g0:D_FF + g0 + FFN_CHUNK]], axis=1)

    if attn_tail:
        @pl.when(pl.program_id(1) == 0)
        def _():
            for c in range(N_FFN_CHUNKS):
                carry_ref[c] = chunk_cols(meta_up_ref, c, slice(N_META - SUBLANES, N_META))
    else:
        i = pl.program_id(0)

        @pl.when(i == 0)
        def _():
            carry_ref[...] = jnp.zeros(carry_ref.shape, F32)

    x = h_ref[...]
    if attn_tail:
        x = x + jnp.dot(attn_ref[...], wo_ref[...], preferred_element_type=F32)
    xn_ref[...] = ((x * _rms_scale(x)) * g_ref[...]).astype(BF16)

    for c in range(N_FFN_CHUNKS):
        u = jnp.dot(xn_ref[...], chunk_cols(wup_ref, c), preferred_element_type=F32)
        ext = jnp.concatenate([carry_ref[c], u], axis=0)
        carry_ref[c] = u[tm - SUBLANES:tm, :]
        w = chunk_cols(cw_ref, c)
        cv = chunk_cols(cb_ref, c) + w[CONV_WIDTH - 1:CONV_WIDTH] * u
        for k in range(CONV_WIDTH - 1):
            first = SUBLANES - (CONV_WIDTH - 1) + k
            cv = cv + w[k:k + 1] * ext[first:first + tm, :]
        half_gate = 0.5 * cv[:, 0:FFN_CHUNK]
        act = (half_gate + half_gate * jnp.tanh(half_gate)) * cv[:, FFN_CHUNK:cw]
        act_ref[:, c * FFN_CHUNK:(c + 1) * FFN_CHUNK] = act.astype(BF16)

    out = x + jnp.dot(act_ref[...], wdn_bf_ref[...], preferred_element_type=F32)
    if attn_tail:
        o_ref[...] = (out * _rms_scale(out)) * fg_ref[...]
    else:
        out = jnp.where(_pad_row_mask(i * tm, tm), out, 0.0)
        o_ref[...] = out
        normed = out * _rms_scale(out)
        xk_ref[...] = (normed * gkv_ref[...]).astype(BF16)
        xq_ref[...] = (normed * gq_ref[...]).astype(BF16)


def _meta_up_kernel(*refs):
    h_refs, attn_refs = refs[:BATCH], refs[BATCH:2 * BATCH]
    wo_ref, g_ref, wup_ref, o_ref = refs[2 * BATCH:]
    h = jnp.concatenate([r[...] for r in h_refs], axis=0)
    attn = jnp.concatenate([r[...] for r in attn_refs], axis=0)
    x = h + jnp.dot(attn, wo_ref[...], preferred_element_type=F32)
    xn = ((x * _rms_scale(x)) * g_ref[...]).astype(BF16)
    o_ref[...] = jnp.dot(xn, wup_ref[...], preferred_element_type=F32)


def _meta_up_projection(h, attn, w_o, layer, gain, w_up):
    const = lambda s: (0, 0)
    of_layer = lambda s: (layer, 0, 0)
    meta_rows = [pl.BlockSpec((pl.Element(N_META), pl.Element(D_MODEL)),
                              functools.partial(lambda b, s: (b * LP + ROW_PAD, 0), b))
                 for b in range(BATCH)]
    out = pl.pallas_call(
        _meta_up_kernel,
        out_shape=jax.ShapeDtypeStruct((BATCH * N_META, 2 * D_FF), F32),
        grid=(1,),
        in_specs=meta_rows + meta_rows + [
            pl.BlockSpec((D_MODEL, D_MODEL), const, pipeline_mode=pl.Buffered(1)),
            pl.BlockSpec((None, 1, D_MODEL), of_layer),
            pl.BlockSpec((None, D_MODEL, 2 * D_FF), of_layer, pipeline_mode=pl.Buffered(1))],
        out_specs=pl.BlockSpec((BATCH * N_META, 2 * D_FF), const),
        compiler_params=pltpu.CompilerParams(dimension_semantics=("arbitrary",),
                                             vmem_limit_bytes=VMEM_LIMIT),
        name="meta_up_proj",
    )(*([h] * BATCH), *([attn] * BATCH), w_o, gain, w_up)
    return out.reshape(BATCH, N_META, 2 * D_FF)


def _ffn_layer(h, layer, gain, w_up, conv_w, conv_b, w_down, attn_tail=None, qk_gains=None):
    const = lambda *_: (0, 0)
    of_layer = lambda *_: (layer, 0, 0)
    if attn_tail is not None:
        attn, w_o, final_gain = attn_tail
        meta_up = _meta_up_projection(h, attn, w_o, layer, gain, w_up)
        grid = (BATCH, SEQ // TM_FFN)
        blk_per_tile = TM_FFN // BLK
        window = lambda b, k: ((b * N_BLK + 1 + k * blk_per_tile) * BLK, 0)
        row_tile = pl.BlockSpec((pl.Element(TM_FFN), pl.Element(D_MODEL)), window)
        operands = [h, attn, w_o, final_gain, meta_up]
        in_specs = [row_tile, row_tile,
                    pl.BlockSpec((D_MODEL, D_MODEL), const, pipeline_mode=pl.Buffered(1)),
                    pl.BlockSpec((1, D_MODEL), const),
                    pl.BlockSpec((None, N_META, 2 * D_FF), lambda b, k: (b, 0, 0))]
        out_shape = jax.ShapeDtypeStruct((BATCH, SEQ, D_MODEL), F32)
        out_specs = pl.BlockSpec((None, TM_FFN, D_MODEL), lambda b, k: (b, k, 0))
    else:
        grid = (ROWS // TM_FFN,)
        row_tile = pl.BlockSpec((TM_FFN, D_MODEL), lambda i: (i, 0))
        operands = [h] + list(qk_gains)
        in_specs = [row_tile] + [pl.BlockSpec((1, D_MODEL), const)] * 2
        out_shape = (jax.ShapeDtypeStruct((ROWS, D_MODEL), F32),) + (
            jax.ShapeDtypeStruct((ROWS, D_MODEL), BF16),) * 2
        out_specs = (row_tile,) * 3
    operands += [gain, w_up, conv_w, conv_b, w_down]
    in_specs += [
        pl.BlockSpec((None, 1, D_MODEL), of_layer),
        pl.BlockSpec((None, D_MODEL, 2 * D_FF), of_layer, pipeline_mode=pl.Buffered(1)),
        pl.BlockSpec((None, CONV_WIDTH, 2 * D_FF), of_layer),
        pl.BlockSpec((None, 1, 2 * D_FF), of_layer),
        pl.BlockSpec((None, D_FF, D_MODEL), of_layer, pipeline_mode=pl.Buffered(1)),
    ]
    return pl.pallas_call(
        functools.partial(_ffn_kernel, attn_tail=attn_tail is not None),
        out_shape=out_shape,
        grid=grid,
        in_specs=in_specs,
        out_specs=out_specs,
        scratch_shapes=[
            pltpu.VMEM((N_FFN_CHUNKS, SUBLANES, 2 * FFN_CHUNK), F32),
            pltpu.VMEM((TM_FFN, D_MODEL), BF16),
            pltpu.VMEM((TM_FFN, D_FF), BF16),
            pltpu.VMEM((D_FF, D_MODEL), BF16),
        ],
        compiler_params=pltpu.CompilerParams(dimension_semantics=("arbitrary",) * len(grid),
                                             vmem_limit_bytes=VMEM_LIMIT),
        name="conv_ffn" if attn_tail is None else "attn_out_conv_ffn_norm",
    )(*operands)


_NT_DIMS = (((1,), (1,)), ((), ()))


def _suffix_sum_matrix():
    s = np.arange(BLK)[:, None]
    j = np.arange(BLK)[None, :]
    return jnp.asarray(j > s, BF16)


def _attn_kernel(xk0_ref, xq0_ref, xkn_ref, xqn_ref, wk32_ref, wv32_ref, wq32_ref, u_ref, o_ref,
                 wk_all_ref, wvt_all_ref, wqt_all_ref, k_ref, vt_ref, qt_ref, qm_ref, acc_ref,
                 carry_ref, go_ref):
    i = pl.program_id(2)
    group = pl.program_id(1)

    @pl.when((pl.program_id(0) == 0) & (i == 0))
    def _():
        wk_all_ref[group] = wk32_ref[...].astype(BF16)
        wvt_all_ref[group] = wv32_ref[...].T.astype(BF16)
        wqt_all_ref[group] = (wq32_ref[...].T * (HEAD_DIM ** -0.5)).astype(BF16)

    wk_ref, wvt_ref, wqt_ref = wk_all_ref.at[group], wvt_all_ref.at[group], wqt_all_ref.at[group]

    def project(xk_ref, xq_ref, block, slot):
        rows = pl.ds(pl.multiple_of(block * BLK, BLK), BLK)
        xk = xk_ref[0]
        k_ref[rows, :] = jnp.dot(xk, wk_ref[...], preferred_element_type=F32).astype(BF16)
        vt_ref[:, rows] = lax.dot_general(wvt_ref[...], xk, _NT_DIMS,
                                          preferred_element_type=F32).astype(BF16)
        qt_ref[slot] = lax.dot_general(wqt_ref[...], xq_ref[0], _NT_DIMS,
                                       preferred_element_type=F32).astype(BF16)

    @pl.when(i == 0)
    def _():
        project(xk0_ref, xq0_ref, 0, 0)

    pair_row = lax.broadcasted_iota(jnp.int32, (PAIR, 1), 0)
    for h in range(HEAD_GROUP):
        first = (h % 2) * HEAD_DIM
        mine = (pair_row >= first) & (pair_row < first + HEAD_DIM)
        qp = qt_ref[lax.rem(i, 2), (h // 2) * PAIR:(h // 2 + 1) * PAIR, :]
        qm_ref[h] = jnp.where(mine, qp, jnp.zeros((), BF16))
    acc_ref[...] = jnp.zeros(acc_ref.shape, F32)
    carry_ref[...] = jnp.zeros(carry_ref.shape, F32)

    def sweep(blocks, project_ahead=False):
        row = lax.broadcasted_iota(jnp.int32, (BLK, BLK), 0)
        col = lax.broadcasted_iota(jnp.int32, (BLK, BLK), 1)
        valid = {None: None}
        for j, kind in blocks:
            if kind == "causal":
                valid[kind] = row < col
            elif kind == "edge":
                key = j * BLK + row
                valid[kind] = (key < i * BLK + col) & (key >= ROW_PAD)
        chains = [(pl.multiple_of(j * BLK, BLK), kind, h) for j, kind in blocks for h in range(HEAD_GROUP)]
        zs = [jnp.dot(k_ref[pl.ds(start, BLK), (h // 2) * PAIR:(h // 2 + 1) * PAIR], qm_ref[h],
                      preferred_element_type=F32) for start, _, h in chains]
        if project_ahead:
            project(xkn_ref, xqn_ref, jnp.minimum(i + 1, N_BLK - 1), lax.rem(i + 1, 2))

        log_betas, sums, totals = [], [], []
        for (_, kind, h), z in zip(chains, zs):
            soft = jnp.log(1.0 + jnp.exp2(jnp.abs(z) * -LOG2_E))
            log_beta = jnp.minimum(z, 0.0) - soft
            log_1m = log_beta - z
            if kind is not None:
                log_1m = jnp.where(valid[kind], log_1m, 0.0)
            log_betas.append(log_beta)
            rounded = log_1m.astype(BF16)
            s = jnp.dot(u_ref[...], rounded, preferred_element_type=F32)
            sums.append(s)
            totals.append(s[0:1, :] + rounded[0:1, :].astype(F32))
        carry = [carry_ref[h] for h in range(HEAD_GROUP)]
        carry_in = []
        for (_, _, h), total in zip(chains, totals):
            carry_in.append(carry[h])
            carry[h] = carry[h] + total
        most = carry[0]
        for h in range(HEAD_GROUP):
            carry_ref[h] = carry[h]
            most = jnp.maximum(most, carry[h])
        go_ref[0] = (jnp.max(most) >= UNDERFLOW_LOG).astype(jnp.int32)
        for (start, kind, h), log_beta, s, c_in in zip(chains, log_betas, sums, carry_in):
            a = jnp.exp(log_beta + s[0:BLK] + c_in[0:1, :])
            if kind is not None:
                a = jnp.where(valid[kind], a, 0.0)
            vb = vt_ref[h * HEAD_DIM:(h + 1) * HEAD_DIM, pl.ds(start, BLK)]
            rows = slice((h % 2) * HEAD_DIM, (h % 2 + 1) * HEAD_DIM)
            acc_ref[h // 2, rows, :] += jnp.dot(vb, a.astype(BF16), preferred_element_type=F32)

    @pl.when(i >= 2)
    def _():
        sweep([(i, "causal"), (i - 1, None)], project_ahead=True)

    @pl.when(i < 2)
    def _():
        sweep([(i, "edge")], project_ahead=True)

    def interior(j):
        sweep([(j, None)])
        return j - 1

    j_end = lax.while_loop(lambda j: (j >= 1) & (go_ref[0] != 0), interior,
                           jnp.where(i >= 2, i - 2, i - 1))

    @pl.when((j_end == 0) & (go_ref[0] != 0))
    def _():
        sweep([(0, "edge")])

    for p in range(HEAD_GROUP // 2):
        o_ref[0, :, p * PAIR:(p + 1) * PAIR] = acc_ref[p].T.astype(BF16)


def _attention(xk, xq, w_kv, w_q):
    gw = HEAD_GROUP * HEAD_DIM
    n_groups = N_HEADS // HEAD_GROUP
    first_block = pl.BlockSpec((1, BLK, D_MODEL), lambda b, g, i: (b, 0, 0))
    next_block = pl.BlockSpec((1, BLK, D_MODEL), lambda b, g, i: (b, jnp.minimum(i + 1, N_BLK - 1), 0))
    return pl.pallas_call(
        _attn_kernel,
        out_shape=jax.ShapeDtypeStruct((BATCH, LP, D_MODEL), BF16),
        grid=(BATCH, n_groups, N_BLK),
        in_specs=[
            first_block, first_block, next_block, next_block,
            pl.BlockSpec((D_MODEL, gw), lambda b, g, i: (0, g)),
            pl.BlockSpec((D_MODEL, gw), lambda b, g, i: (0, n_groups + g)),
            pl.BlockSpec((D_MODEL, gw), lambda b, g, i: (0, g)),
            pl.BlockSpec((BLK, BLK), lambda b, g, i: (0, 0)),
        ],
        out_specs=pl.BlockSpec((1, BLK, gw), lambda b, g, i: (b, i, g)),
        scratch_shapes=[pltpu.VMEM((n_groups, D_MODEL, gw), BF16),
                        pltpu.VMEM((n_groups, gw, D_MODEL), BF16),
                        pltpu.VMEM((n_groups, gw, D_MODEL), BF16),
                        pltpu.VMEM((LP, gw), BF16),
                        pltpu.VMEM((gw, LP), BF16),
                        pltpu.VMEM((2, gw, BLK), BF16),
                        pltpu.VMEM((HEAD_GROUP, PAIR, BLK), BF16),
                        pltpu.VMEM((HEAD_GROUP // 2, PAIR, BLK), F32),
                        pltpu.VMEM((HEAD_GROUP, SUBLANES, BLK), F32),
                        pltpu.SMEM((1,), jnp.int32)],
        compiler_params=pltpu.CompilerParams(
            dimension_semantics=("arbitrary", "arbitrary", "arbitrary"), vmem_limit_bytes=VMEM_LIMIT),
        name="stickbreak_attn",
    )(xk, xq, xk, xq, w_kv, w_kv, w_q, _suffix_sum_matrix())


def kernel(x, meta_tokens, mix_norm, ffn_norm, pool_w, pool_scale, kv_norm, w_kv, w_q, w_o,
           ffn_w_up, ffn_conv_w, ffn_conv_b, ffn_w_down, final_norm):
    row = lambda v: v.reshape(1, -1)

    w_up = ffn_w_up.astype(BF16)

    def ffn(h, layer, **mode):
        return _ffn_layer(h.reshape(ROWS, D_MODEL), layer, ffn_norm[:, None, :], w_up, ffn_conv_w,
                          ffn_conv_b[:, None, :], ffn_w_down, **mode)

    h = _pool_layer(x, meta_tokens.astype(x.dtype), row(mix_norm[0]), pool_w[0].astype(BF16),
                    row(pool_scale[0]))
    h, xk, xq = ffn(h, 0, qk_gains=(row(kv_norm), row(mix_norm[1])))

    by_batch = lambda a: a.reshape(BATCH, LP, D_MODEL)
    attn = _attention(by_batch(xk), by_batch(xq), w_kv, w_q[0])
    return ffn(h, 1, attn_tail=(attn.reshape(ROWS, D_MODEL), w_o[0].astype(BF16), row(final_norm)))
```

```python
import functools

import numpy as np
import jax
import jax.numpy as jnp
from jax import lax
from jax.experimental import pallas as pl
from jax.experimental.pallas import tpu as pltpu

D_MODEL = 1024
BATCH = 4
SEQ = 4096
N_META = 16
POOL_WINDOWS = (2, 4, 8, 16)
POOL_GROUP_DIM = D_MODEL // len(POOL_WINDOWS)
N_HEADS = 16
HEAD_DIM = D_MODEL // N_HEADS
D_FF = 2816
CONV_WIDTH = 3
RMS_EPS = 1e-6

SUBLANES = 8
MXU_DIM = 256

BLK = MXU_DIM
ROW_PAD = BLK - N_META
LP = SEQ + BLK
N_BLK = LP // BLK
ROWS = BATCH * LP
MAX_WINDOW = max(POOL_WINDOWS)
HEAD_GROUP = 8
PAIR = 2 * HEAD_DIM
UNDERFLOW_LOG = -104.0
LOG2_E = 1.4426950408889634

TM_POOL = BLK
TM_FFN = 512
FFN_CHUNK = MXU_DIM
N_FFN_CHUNKS = D_FF // FFN_CHUNK
VMEM_LIMIT = 56 * 1024 * 1024

F32 = jnp.float32
BF16 = jnp.bfloat16


def _rms_scale(x):
    return lax.rsqrt(jnp.mean(x * x, axis=-1, keepdims=True) + RMS_EPS)


def _pool_kernel(x_ref, meta_ref, g_ref, w_ref, sc_ref, o_ref, buf_ref):
    j = pl.program_id(0)
    tm = TM_POOL

    @pl.when(j == 0)
    def _():
        buf_ref[:, 0:MAX_WINDOW, :] = jnp.zeros((BATCH, MAX_WINDOW, D_MODEL), F32)

    head_tile = jnp.concatenate([jnp.zeros((ROW_PAD, D_MODEL), F32), meta_ref[...]], axis=0)
    pos = j * tm + lax.broadcasted_iota(jnp.int32, (tm, 1), 0) - ROW_PAD
    xs, diffs = [], [[] for _ in POOL_WINDOWS]
    for b in range(BATCH):
        x = jnp.where(j == 0, head_tile, x_ref[b])
        xn = (x * _rms_scale(x)) * g_ref[...]
        buf_ref[b, MAX_WINDOW:MAX_WINDOW + tm, :] = xn
        win = buf_ref[b]
        shift = 1
        for g, w in enumerate(POOL_WINDOWS):
            assert w == 2 * shift
            win = win[:, (POOL_GROUP_DIM if g else 0):]
            win = win + pltpu.roll(win, shift, axis=0)
            count = jnp.clip(pos + 1, 1, w).astype(F32)
            mean = win[MAX_WINDOW:, 0:POOL_GROUP_DIM] / count
            diffs[g].append((mean - xn[:, g * POOL_GROUP_DIM:(g + 1) * POOL_GROUP_DIM]).astype(BF16))
            shift = w
        buf_ref[b, 0:MAX_WINDOW, :] = buf_ref[b, tm:tm + MAX_WINDOW, :]
        xs.append(x)
    y = jnp.concatenate([jnp.dot(jnp.concatenate(diffs[g], axis=0), w_ref[g], preferred_element_type=F32)
                         for g in range(len(POOL_WINDOWS))], axis=1) * sc_ref[...]
    for b in range(BATCH):
        o_ref[b] = jnp.where(pos >= 0, xs[b] + y[b * tm:(b + 1) * tm], 0.0)


def _pool_layer(x, meta, gain, w, scale):
    const2 = lambda j: (0, 0)
    return pl.pallas_call(
        _pool_kernel,
        out_shape=jax.ShapeDtypeStruct((BATCH, LP, D_MODEL), F32),
        grid=(LP // TM_POOL,),
        in_specs=[
            pl.BlockSpec((BATCH, TM_POOL, D_MODEL), lambda j: (0, jnp.maximum(j - 1, 0), 0)),
            pl.BlockSpec((N_META, D_MODEL), const2),
            pl.BlockSpec((1, D_MODEL), const2),
            pl.BlockSpec((len(POOL_WINDOWS), POOL_GROUP_DIM, POOL_GROUP_DIM), lambda j: (0, 0, 0)),
            pl.BlockSpec((1, D_MODEL), const2),
        ],
        out_specs=pl.BlockSpec((BATCH, TM_POOL, D_MODEL), lambda j: (0, j, 0)),
        scratch_shapes=[pltpu.VMEM((BATCH, MAX_WINDOW + TM_POOL, D_MODEL), F32)],
        compiler_params=pltpu.CompilerParams(dimension_semantics=("arbitrary",),
                                             vmem_limit_bytes=VMEM_LIMIT),
        name="pool_mixer",
    )(x, meta, gain, w, scale)


def _pad_row_mask(row0, tm):
    row = row0 + lax.broadcasted_iota(jnp.int32, (tm, 1), 0)
    is_pad = jnp.zeros((tm, 1), jnp.bool_)
    for b in range(BATCH):
        is_pad = is_pad | ((row >= b * LP) & (row < b * LP + ROW_PAD))
    return jnp.logical_not(is_pad)


def _ffn_kernel(*refs, attn_tail):
    if attn_tail:
        (h_ref, attn_ref, wo_ref, fg_ref, meta_up_ref, g_ref, wup_ref, cw_ref, cb_ref, wdn_ref,
         o_ref, *scratch) = refs
    else:
        (h_ref, gkv_ref, gq_ref, g_ref, wup_ref, cw_ref, cb_ref, wdn_ref,
         o_ref, xk_ref, xq_ref, *scratch) = refs
    carry_ref, xn_ref, act_ref, wdn_bf_ref = scratch
    tm = TM_FFN
    cw = 2 * FFN_CHUNK
    first_step = (pl.program_id(0) == 0) & (pl.program_id(1) == 0) if attn_tail else pl.program_id(0) == 0

    @pl.when(first_step)
    def _():
        wdn_bf_ref[...] = wdn_ref[...].astype(BF16)

    def chunk_cols(ref, c, rows=slice(None)):
        g0 = c * FFN_CHUNK
        return jnp.concatenate([ref[rows, g0:g0 + FFN_CHUNK],
                                ref[rows, D_FF + g0:D_FF + g0 + FFN_CHUNK]], axis=1)

    if attn_tail:
        @pl.when(pl.program_id(1) == 0)
        def _():
            for c in range(N_FFN_CHUNKS):
                carry_ref[c] = chunk_cols(meta_up_ref, c, slice(N_META - SUBLANES, N_META))
    else:
        i = pl.program_id(0)

        @pl.when(i == 0)
        def _():
            carry_ref[...] = jnp.zeros(carry_ref.shape, F32)

    x = h_ref[...]
    if attn_tail:
        x = x + jnp.dot(attn_ref[...], wo_ref[...], preferred_element_type=F32)
    xn_ref[...] = ((x * _rms_scale(x)) * g_ref[...]).astype(BF16)

    for c in range(N_FFN_CHUNKS):
        u = jnp.dot(xn_ref[...], chunk_cols(wup_ref, c), preferred_element_type=F32)
        ext = jnp.concatenate([carry_ref[c], u], axis=0)
        carry_ref[c] = u[tm - SUBLANES:tm, :]
        w = chunk_cols(cw_ref, c)
        cv = chunk_cols(cb_ref, c) + w[CONV_WIDTH - 1:CONV_WIDTH] * u
        for k in range(CONV_WIDTH - 1):
            first = SUBLANES - (CONV_WIDTH - 1) + k
            cv = cv + w[k:k + 1] * ext[first:first + tm, :]
        half_gate = 0.5 * cv[:, 0:FFN_CHUNK]
        act = (half_gate + half_gate * jnp.tanh(half_gate)) * cv[:, FFN_CHUNK:cw]
        act_ref[:, c * FFN_CHUNK:(c + 1) * FFN_CHUNK] = act.astype(BF16)

    out = x + jnp.dot(act_ref[...], wdn_bf_ref[...], preferred_element_type=F32)
    if attn_tail:
        o_ref[...] = (out * _rms_scale(out)) * fg_ref[...]
    else:
        out = jnp.where(_pad_row_mask(i * tm, tm), out, 0.0)
        o_ref[...] = out
        normed = out * _rms_scale(out)
        xk_ref[...] = (normed * gkv_ref[...]).astype(BF16)
        xq_ref[...] = (normed * gq_ref[...]).astype(BF16)


def _meta_up_kernel(*refs):
    h_refs, attn_refs = refs[:BATCH], refs[BATCH:2 * BATCH]
    wo_ref, g_ref, wup_ref, o_ref = refs[2 * BATCH:]
    h = jnp.concatenate([r[...] for r in h_refs], axis=0)
    attn = jnp.concatenate([r[...] for r in attn_refs], axis=0)
    x = h + jnp.dot(attn, wo_ref[...], preferred_element_type=F32)
    xn = ((x * _rms_scale(x)) * g_ref[...]).astype(BF16)
    o_ref[...] = jnp.dot(xn, wup_ref[...], preferred_element_type=F32)


def _meta_up_projection(h, attn, w_o, layer, gain, w_up):
    const = lambda s: (0, 0)
    of_layer = lambda s: (layer, 0, 0)
    meta_rows = [pl.BlockSpec((pl.Element(N_META), pl.Element(D_MODEL)),
                              functools.partial(lambda b, s: (b * LP + ROW_PAD, 0), b))
                 for b in range(BATCH)]
    out = pl.pallas_call(
        _meta_up_kernel,
        out_shape=jax.ShapeDtypeStruct((BATCH * N_META, 2 * D_FF), F32),
        grid=(1,),
        in_specs=meta_rows + meta_rows + [
            pl.BlockSpec((D_MODEL, D_MODEL), const, pipeline_mode=pl.Buffered(1)),
            pl.BlockSpec((None, 1, D_MODEL), of_layer),
            pl.BlockSpec((None, D_MODEL, 2 * D_FF), of_layer, pipeline_mode=pl.Buffered(1))],
        out_specs=pl.BlockSpec((BATCH * N_META, 2 * D_FF), const),
        compiler_params=pltpu.CompilerParams(dimension_semantics=("arbitrary",),
                                             vmem_limit_bytes=VMEM_LIMIT),
        name="meta_up_proj",
    )(*([h] * BATCH), *([attn] * BATCH), w_o, gain, w_up)
    return out.reshape(BATCH, N_META, 2 * D_FF)


def _ffn_layer(h, layer, gain, w_up, conv_w, conv_b, w_down, attn_tail=None, qk_gains=None):
    const = lambda *_: (0, 0)
    of_layer = lambda *_: (layer, 0, 0)
    if attn_tail is not None:
        attn, w_o, final_gain = attn_tail
        meta_up = _meta_up_projection(h, attn, w_o, layer, gain, w_up)
        grid = (BATCH, SEQ // TM_FFN)
        blk_per_tile = TM_FFN // BLK
        window = lambda b, k: ((b * N_BLK + 1 + k * blk_per_tile) * BLK, 0)
        row_tile = pl.BlockSpec((pl.Element(TM_FFN), pl.Element(D_MODEL)), window)
        operands = [h, attn, w_o, final_gain, meta_up]
        in_specs = [row_tile, row_tile,
                    pl.BlockSpec((D_MODEL, D_MODEL), const, pipeline_mode=pl.Buffered(1)),
                    pl.BlockSpec((1, D_MODEL), const),
                    pl.BlockSpec((None, N_META, 2 * D_FF), lambda b, k: (b, 0, 0))]
        out_shape = jax.ShapeDtypeStruct((BATCH, SEQ, D_MODEL), F32)
        out_specs = pl.BlockSpec((None, TM_FFN, D_MODEL), lambda b, k: (b, k, 0))
    else:
        grid = (ROWS // TM_FFN,)
        row_tile = pl.BlockSpec((TM_FFN, D_MODEL), lambda i: (i, 0))
        operands = [h] + list(qk_gains)
        in_specs = [row_tile] + [pl.BlockSpec((1, D_MODEL), const)] * 2
        out_shape = (jax.ShapeDtypeStruct((ROWS, D_MODEL), F32),) + (
            jax.ShapeDtypeStruct((ROWS, D_MODEL), BF16),) * 2
        out_specs = (row_tile,) * 3
    operands += [gain, w_up, conv_w, conv_b, w_down]
    in_specs += [
        pl.BlockSpec((None, 1, D_MODEL), of_layer),
        pl.BlockSpec((None, D_MODEL, 2 * D_FF), of_layer, pipeline_mode=pl.Buffered(1)),
        pl.BlockSpec((None, CONV_WIDTH, 2 * D_FF), of_layer),
        pl.BlockSpec((None, 1, 2 * D_FF), of_layer),
        pl.BlockSpec((None, D_FF, D_MODEL), of_layer, pipeline_mode=pl.Buffered(1)),
    ]
    return pl.pallas_call(
        functools.partial(_ffn_kernel, attn_tail=attn_tail is not None),
        out_shape=out_shape,
        grid=grid,
        in_specs=in_specs,
        out_specs=out_specs,
        scratch_shapes=[
            pltpu.VMEM((N_FFN_CHUNKS, SUBLANES, 2 * FFN_CHUNK), F32),
            pltpu.VMEM((TM_FFN, D_MODEL), BF16),
            pltpu.VMEM((TM_FFN, D_FF), BF16),
            pltpu.VMEM((D_FF, D_MODEL), BF16),
        ],
        compiler_params=pltpu.CompilerParams(dimension_semantics=("arbitrary",) * len(grid),
                                             vmem_limit_bytes=VMEM_LIMIT),
        name="conv_ffn" if attn_tail is None else "attn_out_conv_ffn_norm",
    )(*operands)


_NT_DIMS = (((1,), (1,)), ((), ()))


def _suffix_sum_matrix():
    s = np.arange(BLK)[:, None]
    j = np.arange(BLK)[None, :]
    return jnp.asarray(j > s, BF16)


def _attn_kernel(xk0_ref, xq0_ref, xkn_ref, xqn_ref, wk32_ref, wv32_ref, wq32_ref, u_ref, o_ref,
                 wk_all_ref, wvt_all_ref, wqt_all_ref, k_ref, vt_ref, qt_ref, qm_ref, acc_ref,
                 carry_ref, go_ref):
    i = pl.program_id(2)
    group = pl.program_id(1)

    @pl.when((pl.program_id(0) == 0) & (i == 0))
    def _():
        wk_all_ref[group] = wk32_ref[...].astype(BF16)
        wvt_all_ref[group] = wv32_ref[...].T.astype(BF16)
        wqt_all_ref[group] = (wq32_ref[...].T * (HEAD_DIM ** -0.5)).astype(BF16)

    wk_ref, wvt_ref, wqt_ref = wk_all_ref.at[group], wvt_all_ref.at[group], wqt_all_ref.at[group]

    def project(xk_ref, xq_ref, block, slot):
        rows = pl.ds(pl.multiple_of(block * BLK, BLK), BLK)
        xk = xk_ref[0]
        k_ref[rows, :] = jnp.dot(xk, wk_ref[...], preferred_element_type=F32).astype(BF16)
        vt_ref[:, rows] = lax.dot_general(wvt_ref[...], xk, _NT_DIMS,
                                          preferred_element_type=F32).astype(BF16)
        qt_ref[slot] = lax.dot_general(wqt_ref[...], xq_ref[0], _NT_DIMS,
                                       preferred_element_type=F32).astype(BF16)

    @pl.when(i == 0)
    def _():
        project(xk0_ref, xq0_ref, 0, 0)

    pair_row = lax.broadcasted_iota(jnp.int32, (PAIR, 1), 0)
    for h in range(HEAD_GROUP):
        first = (h % 2) * HEAD_DIM
        mine = (pair_row >= first) & (pair_row < first + HEAD_DIM)
        qp = qt_ref[lax.rem(i, 2), (h // 2) * PAIR:(h // 2 + 1) * PAIR, :]
        qm_ref[h] = jnp.where(mine, qp, jnp.zeros((), BF16))
    acc_ref[...] = jnp.zeros(acc_ref.shape, F32)
    carry_ref[...] = jnp.zeros(carry_ref.shape, F32)

    def sweep(blocks, project_ahead=False):
        row = lax.broadcasted_iota(jnp.int32, (BLK, BLK), 0)
        col = lax.broadcasted_iota(jnp.int32, (BLK, BLK), 1)
        valid = {None: None}
        for j, kind in blocks:
            if kind == "causal":
                valid[kind] = row < col
            elif kind == "edge":
                key = j * BLK + row
                valid[kind] = (key < i * BLK + col) & (key >= ROW_PAD)
        chains = [(pl.multiple_of(j * BLK, BLK), kind, h) for j, kind in blocks for h in range(HEAD_GROUP)]
        zs = [jnp.dot(k_ref[pl.ds(start, BLK), (h // 2) * PAIR:(h // 2 + 1) * PAIR], qm_ref[h],
                      preferred_element_type=F32) for start, _, h in chains]
        if project_ahead:
            project(xkn_ref, xqn_ref, jnp.minimum(i + 1, N_BLK - 1), lax.rem(i + 1, 2))

        log_betas, sums, totals = [], [], []
        for (_, kind, h), z in zip(chains, zs):
            soft = jnp.log(1.0 + jnp.exp2(jnp.abs(z) * -LOG2_E))
            log_beta = jnp.minimum(z, 0.0) - soft
            log_1m = log_beta - z
            if kind is not None:
                log_1m = jnp.where(valid[kind], log_1m, 0.0)
            log_betas.append(log_beta)
            rounded = log_1m.astype(BF16)
            s = jnp.dot(u_ref[...], rounded, preferred_element_type=F32)
            sums.append(s)
            totals.append(s[0:1, :] + rounded[0:1, :].astype(F32))
        carry = [carry_ref[h] for h in range(HEAD_GROUP)]
        carry_in = []
        for (_, _, h), total in zip(chains, totals):
            carry_in.append(carry[h])
            carry[h] = carry[h] + total
        most = carry[0]
        for h in range(HEAD_GROUP):
            carry_ref[h] = carry[h]
            most = jnp.maximum(most, carry[h])
        go_ref[0] = (jnp.max(most) >= UNDERFLOW_LOG).astype(jnp.int32)
        for (start, kind, h), log_beta, s, c_in in zip(chains, log_betas, sums, carry_in):
            a = jnp.exp(log_beta + s[0:BLK] + c_in[0:1, :])
            if kind is not None:
                a = jnp.where(valid[kind], a, 0.0)
            vb = vt_ref[h * HEAD_DIM:(h + 1) * HEAD_DIM, pl.ds(start, BLK)]
            rows = slice((h % 2) * HEAD_DIM, (h % 2 + 1) * HEAD_DIM)
            acc_ref[h // 2, rows, :] += jnp.dot(vb, a.astype(BF16), preferred_element_type=F32)

    @pl.when(i >= 2)
    def _():
        sweep([(i, "causal"), (i - 1, None)], project_ahead=True)

    @pl.when(i < 2)
    def _():
        sweep([(i, "edge")], project_ahead=True)

    def interior(j):
        sweep([(j, None)])
        return j - 1

    j_end = lax.while_loop(lambda j: (j >= 1) & (go_ref[0] != 0), interior,
                           jnp.where(i >= 2, i - 2, i - 1))

    @pl.when((j_end == 0) & (go_ref[0] != 0))
    def _():
        sweep([(0, "edge")])

    for p in range(HEAD_GROUP // 2):
        o_ref[0, :, p * PAIR:(p + 1) * PAIR] = acc_ref[p].T.astype(BF16)


def _attention(xk, xq, w_kv, w_q):
    gw = HEAD_GROUP * HEAD_DIM
    n_groups = N_HEADS // HEAD_GROUP
    f32_group = lambda b, g: jnp.where(b == 0, g, n_groups - 1)
    first_block = pl.BlockSpec((1, BLK, D_MODEL), lambda b, g, i: (b, 0, 0))
    next_block = pl.BlockSpec((1, BLK, D_MODEL), lambda b, g, i: (b, jnp.minimum(i + 1, N_BLK - 1), 0))
    return pl.pallas_call(
        _attn_kernel,
        out_shape=jax.ShapeDtypeStruct((BATCH, LP, D_MODEL), BF16),
        grid=(BATCH, n_groups, N_BLK),
        in_specs=[
            first_block, first_block, next_block, next_block,
            pl.BlockSpec((D_MODEL, gw), lambda b, g, i: (0, f32_group(b, g))),
            pl.BlockSpec((D_MODEL, gw), lambda b, g, i: (0, n_groups + f32_group(b, g))),
            pl.BlockSpec((D_MODEL, gw), lambda b, g, i: (0, f32_group(b, g))),
            pl.BlockSpec((BLK, BLK), lambda b, g, i: (0, 0)),
        ],
        out_specs=pl.BlockSpec((1, BLK, gw), lambda b, g, i: (b, i, g)),
        scratch_shapes=[pltpu.VMEM((n_groups, D_MODEL, gw), BF16),
                        pltpu.VMEM((n_groups, gw, D_MODEL), BF16),
                        pltpu.VMEM((n_groups, gw, D_MODEL), BF16),
                        pltpu.VMEM((LP, gw), BF16),
                        pltpu.VMEM((gw, LP), BF16),
                        pltpu.VMEM((2, gw, BLK), BF16),
                        pltpu.VMEM((HEAD_GROUP, PAIR, BLK), BF16),
                        pltpu.VMEM((HEAD_GROUP // 2, PAIR, BLK), F32),
                        pltpu.VMEM((HEAD_GROUP, SUBLANES, BLK), F32),
                        pltpu.SMEM((1,), jnp.int32)],
        compiler_params=pltpu.CompilerParams(
            dimension_semantics=("arbitrary", "arbitrary", "arbitrary"), vmem_limit_bytes=VMEM_LIMIT),
        name="stickbreak_attn",
    )(xk, xq, xk, xq, w_kv, w_kv, w_q, _suffix_sum_matrix())


def kernel(x, meta_tokens, mix_norm, ffn_norm, pool_w, pool_scale, kv_norm, w_kv, w_q, w_o,
           ffn_w_up, ffn_conv_w, ffn_conv_b, ffn_w_down, final_norm):
    row = lambda v: v.reshape(1, -1)

    w_up = ffn_w_up.astype(BF16)

    def ffn(h, layer, **mode):
        return _ffn_layer(h.reshape(ROWS, D_MODEL), layer, ffn_norm[:, None, :], w_up, ffn_conv_w,
                          ffn_conv_b[:, None, :], ffn_w_down, **mode)

    h = _pool_layer(x, meta_tokens.astype(x.dtype), row(mix_norm[0]), pool_w[0].astype(BF16),
                    row(pool_scale[0]))
    h, xk, xq = ffn(h, 0, qk_gains=(row(kv_norm), row(mix_norm[1])))

    by_batch = lambda a: a.reshape(BATCH, LP, D_MODEL)
    attn = _attention(by_batch(xk), by_batch(xq), w_kv, w_q[0])
    return ffn(h, 1, attn_tail=(attn.reshape(ROWS, D_MODEL), w_o[0].astype(BF16), row(final_norm)))
```

```python
import functools

import numpy as np
import jax
import jax.numpy as jnp
from jax import lax
from jax.experimental import pallas as pl
from jax.experimental.pallas import tpu as pltpu

D_MODEL = 1024
BATCH = 4
SEQ = 4096
N_META = 16
POOL_WINDOWS = (2, 4, 8, 16)
POOL_GROUP_DIM = D_MODEL // len(POOL_WINDOWS)
N_HEADS = 16
HEAD_DIM = D_MODEL // N_HEADS
D_FF = 2816
CONV_WIDTH = 3
RMS_EPS = 1e-6

SUBLANES = 8
MXU_DIM = 256

BLK = MXU_DIM
ROW_PAD = BLK - N_META
LP = SEQ + BLK
N_BLK = LP // BLK
ROWS = BATCH * LP
MAX_WINDOW = max(POOL_WINDOWS)
HEAD_GROUP = 8
PAIR = 2 * HEAD_DIM
UNDERFLOW_LOG = -104.0
LOG2_E = 1.4426950408889634

TM_POOL = BLK
TM_FFN = 512
FFN_CHUNK = MXU_DIM
N_FFN_CHUNKS = D_FF // FFN_CHUNK
WUP_CAST_COLS = 2 * FFN_CHUNK
WUP_CAST_STEPS = 2 * D_FF // WUP_CAST_COLS
VMEM_LIMIT = 56 * 1024 * 1024

F32 = jnp.float32
BF16 = jnp.bfloat16


def _rms_scale(x):
    return lax.rsqrt(jnp.mean(x * x, axis=-1, keepdims=True) + RMS_EPS)


def _pool_kernel(x_ref, meta_ref, g_ref, w_ref, sc_ref, wup32_ref, o_ref, wup_bf_ref, buf_ref):
    j = pl.program_id(0)
    tm = TM_POOL

    @pl.when(j == 0)
    def _():
        buf_ref[:, 0:MAX_WINDOW, :] = jnp.zeros((BATCH, MAX_WINDOW, D_MODEL), F32)

    @pl.when(j < WUP_CAST_STEPS)
    def _():
        wup_bf_ref[...] = wup32_ref[...].astype(BF16)

    head_tile = jnp.concatenate([jnp.zeros((ROW_PAD, D_MODEL), F32), meta_ref[...]], axis=0)
    pos = j * tm + lax.broadcasted_iota(jnp.int32, (tm, 1), 0) - ROW_PAD
    xs, diffs = [], [[] for _ in POOL_WINDOWS]
    for b in range(BATCH):
        x = jnp.where(j == 0, head_tile, x_ref[b])
        xn = (x * _rms_scale(x)) * g_ref[...]
        buf_ref[b, MAX_WINDOW:MAX_WINDOW + tm, :] = xn
        win = buf_ref[b]
        shift = 1
        for g, w in enumerate(POOL_WINDOWS):
            assert w == 2 * shift
            win = win[:, (POOL_GROUP_DIM if g else 0):]
            win = win + pltpu.roll(win, shift, axis=0)
            count = jnp.clip(pos + 1, 1, w).astype(F32)
            mean = win[MAX_WINDOW:, 0:POOL_GROUP_DIM] / count
            diffs[g].append((mean - xn[:, g * POOL_GROUP_DIM:(g + 1) * POOL_GROUP_DIM]).astype(BF16))
            shift = w
        buf_ref[b, 0:MAX_WINDOW, :] = buf_ref[b, tm:tm + MAX_WINDOW, :]
        xs.append(x)
    y = jnp.concatenate([jnp.dot(jnp.concatenate(diffs[g], axis=0), w_ref[g], preferred_element_type=F32)
                         for g in range(len(POOL_WINDOWS))], axis=1) * sc_ref[...]
    for b in range(BATCH):
        o_ref[b] = jnp.where(pos >= 0, xs[b] + y[b * tm:(b + 1) * tm], 0.0)


def _pool_layer(x, meta, gain, w, scale, w_up):
    const2 = lambda j: (0, 0)
    n_layers = w_up.shape[0]
    cast_block = pl.BlockSpec((n_layers, D_MODEL, WUP_CAST_COLS),
                              lambda j: (0, 0, jnp.minimum(j, WUP_CAST_STEPS - 1)))
    return pl.pallas_call(
        _pool_kernel,
        out_shape=(jax.ShapeDtypeStruct((BATCH, LP, D_MODEL), F32),
                   jax.ShapeDtypeStruct(w_up.shape, BF16)),
        grid=(LP // TM_POOL,),
        in_specs=[
            pl.BlockSpec((BATCH, TM_POOL, D_MODEL), lambda j: (0, jnp.maximum(j - 1, 0), 0)),
            pl.BlockSpec((N_META, D_MODEL), const2),
            pl.BlockSpec((1, D_MODEL), const2),
            pl.BlockSpec((len(POOL_WINDOWS), POOL_GROUP_DIM, POOL_GROUP_DIM), lambda j: (0, 0, 0)),
            pl.BlockSpec((1, D_MODEL), const2),
            cast_block,
        ],
        out_specs=(pl.BlockSpec((BATCH, TM_POOL, D_MODEL), lambda j: (0, j, 0)), cast_block),
        scratch_shapes=[pltpu.VMEM((BATCH, MAX_WINDOW + TM_POOL, D_MODEL), F32)],
        compiler_params=pltpu.CompilerParams(dimension_semantics=("arbitrary",),
                                             vmem_limit_bytes=VMEM_LIMIT),
        name="pool_mixer",
    )(x, meta, gain, w, scale, w_up)


def _pad_row_mask(row0, tm):
    row = row0 + lax.broadcasted_iota(jnp.int32, (tm, 1), 0)
    is_pad = jnp.zeros((tm, 1), jnp.bool_)
    for b in range(BATCH):
        is_pad = is_pad | ((row >= b * LP) & (row < b * LP + ROW_PAD))
    return jnp.logical_not(is_pad)


def _ffn_kernel(*refs, attn_tail):
    if attn_tail:
        (h_ref, attn_ref, wo_ref, fg_ref, meta_up_ref, g_ref, wup_ref, cw_ref, cb_ref, wdn_ref,
         o_ref, *scratch) = refs
    else:
        (h_ref, gkv_ref, gq_ref, g_ref, wup_ref, cw_ref, cb_ref, wdn_ref,
         o_ref, xk_ref, xq_ref, *scratch) = refs
    carry_ref, xn_ref, act_ref, wdn_bf_ref = scratch
    tm = TM_FFN
    cw = 2 * FFN_CHUNK
    first_step = (pl.program_id(0) == 0) & (pl.program_id(1) == 0) if attn_tail else pl.program_id(0) == 0

    @pl.when(first_step)
    def _():
        wdn_bf_ref[...] = wdn_ref[...].astype(BF16)

    def chunk_cols(ref, c, rows=slice(None)):
        g0 = c * FFN_CHUNK
        return jnp.concatenate([ref[rows, g0:g0 + FFN_CHUNK],
                                ref[rows, D_FF + g0:D_FF + g0 + FFN_CHUNK]], axis=1)

    if attn_tail:
        @pl.when(pl.program_id(1) == 0)
        def _():
            for c in range(N_FFN_CHUNKS):
                carry_ref[c] = chunk_cols(meta_up_ref, c, slice(N_META - SUBLANES, N_META))
    else:
        i = pl.program_id(0)

        @pl.when(i == 0)
        def _():
            carry_ref[...] = jnp.zeros(carry_ref.shape, F32)

    x = h_ref[...]
    if attn_tail:
        x = x + jnp.dot(attn_ref[...], wo_ref[...], preferred_element_type=F32)
    xn_ref[...] = ((x * _rms_scale(x)) * g_ref[...]).astype(BF16)

    for c in range(N_FFN_CHUNKS):
        u = jnp.dot(xn_ref[...], chunk_cols(wup_ref, c), preferred_element_type=F32)
        ext = jnp.concatenate([carry_ref[c], u], axis=0)
        carry_ref[c] = u[tm - SUBLANES:tm, :]
        w = chunk_cols(cw_ref, c)
        cv = chunk_cols(cb_ref, c) + w[CONV_WIDTH - 1:CONV_WIDTH] * u
        for k in range(CONV_WIDTH - 1):
            first = SUBLANES - (CONV_WIDTH - 1) + k
            cv = cv + w[k:k + 1] * ext[first:first + tm, :]
        half_gate = 0.5 * cv[:, 0:FFN_CHUNK]
        act = (half_gate + half_gate * jnp.tanh(half_gate)) * cv[:, FFN_CHUNK:cw]
        act_ref[:, c * FFN_CHUNK:(c + 1) * FFN_CHUNK] = act.astype(BF16)

    out = x + jnp.dot(act_ref[...], wdn_bf_ref[...], preferred_element_type=F32)
    if attn_tail:
        o_ref[...] = (out * _rms_scale(out)) * fg_ref[...]
    else:
        out = jnp.where(_pad_row_mask(i * tm, tm), out, 0.0)
        o_ref[...] = out
        normed = out * _rms_scale(out)
        xk_ref[...] = (normed * gkv_ref[...]).astype(BF16)
        xq_ref[...] = (normed * gq_ref[...]).astype(BF16)


def _meta_up_kernel(*refs):
    h_refs, attn_refs = refs[:BATCH], refs[BATCH:2 * BATCH]
    wo_ref, g_ref, wup_ref, o_ref = refs[2 * BATCH:]
    h = jnp.concatenate([r[...] for r in h_refs], axis=0)
    attn = jnp.concatenate([r[...] for r in attn_refs], axis=0)
    x = h + jnp.dot(attn, wo_ref[...], preferred_element_type=F32)
    xn = ((x * _rms_scale(x)) * g_ref[...]).astype(BF16)
    o_ref[...] = jnp.dot(xn, wup_ref[...], preferred_element_type=F32)


def _meta_up_projection(h, attn, w_o, layer, gain, w_up):
    const = lambda s: (0, 0)
    of_layer = lambda s: (layer, 0, 0)
    meta_rows = [pl.BlockSpec((pl.Element(N_META), pl.Element(D_MODEL)),
                              functools.partial(lambda b, s: (b * LP + ROW_PAD, 0), b))
                 for b in range(BATCH)]
    out = pl.pallas_call(
        _meta_up_kernel,
        out_shape=jax.ShapeDtypeStruct((BATCH * N_META, 2 * D_FF), F32),
        grid=(1,),
        in_specs=meta_rows + meta_rows + [
            pl.BlockSpec((D_MODEL, D_MODEL), const, pipeline_mode=pl.Buffered(1)),
            pl.BlockSpec((None, 1, D_MODEL), of_layer),
            pl.BlockSpec((None, D_MODEL, 2 * D_FF), of_layer, pipeline_mode=pl.Buffered(1))],
        out_specs=pl.BlockSpec((BATCH * N_META, 2 * D_FF), const),
        compiler_params=pltpu.CompilerParams(dimension_semantics=("arbitrary",),
                                             vmem_limit_bytes=VMEM_LIMIT),
        name="meta_up_proj",
    )(*([h] * BATCH), *([attn] * BATCH), w_o, gain, w_up)
    return out.reshape(BATCH, N_META, 2 * D_FF)


def _ffn_layer(h, layer, gain, w_up, conv_w, conv_b, w_down, attn_tail=None, qk_gains=None):
    const = lambda *_: (0, 0)
    of_layer = lambda *_: (layer, 0, 0)
    if attn_tail is not None:
        attn, w_o, final_gain = attn_tail
        meta_up = _meta_up_projection(h, attn, w_o, layer, gain, w_up)
        grid = (BATCH, SEQ // TM_FFN)
        blk_per_tile = TM_FFN // BLK
        window = lambda b, k: ((b * N_BLK + 1 + k * blk_per_tile) * BLK, 0)
        row_tile = pl.BlockSpec((pl.Element(TM_FFN), pl.Element(D_MODEL)), window)
        operands = [h, attn, w_o, final_gain, meta_up]
        in_specs = [row_tile, row_tile,
                    pl.BlockSpec((D_MODEL, D_MODEL), const, pipeline_mode=pl.Buffered(1)),
                    pl.BlockSpec((1, D_MODEL), const),
                    pl.BlockSpec((None, N_META, 2 * D_FF), lambda b, k: (b, 0, 0))]
        out_shape = jax.ShapeDtypeStruct((BATCH, SEQ, D_MODEL), F32)
        out_specs = pl.BlockSpec((None, TM_FFN, D_MODEL), lambda b, k: (b, k, 0))
    else:
        grid = (ROWS // TM_FFN,)
        row_tile = pl.BlockSpec((TM_FFN, D_MODEL), lambda i: (i, 0))
        operands = [h] + list(qk_gains)
        in_specs = [row_tile] + [pl.BlockSpec((1, D_MODEL), const)] * 2
        out_shape = (jax.ShapeDtypeStruct((ROWS, D_MODEL), F32),) + (
            jax.ShapeDtypeStruct((ROWS, D_MODEL), BF16),) * 2
        out_specs = (row_tile,) * 3
    operands += [gain, w_up, conv_w, conv_b, w_down]
    in_specs += [
        pl.BlockSpec((None, 1, D_MODEL), of_layer),
        pl.BlockSpec((None, D_MODEL, 2 * D_FF), of_layer, pipeline_mode=pl.Buffered(1)),
        pl.BlockSpec((None, CONV_WIDTH, 2 * D_FF), of_layer),
        pl.BlockSpec((None, 1, 2 * D_FF), of_layer),
        pl.BlockSpec((None, D_FF, D_MODEL), of_layer, pipeline_mode=pl.Buffered(1)),
    ]
    return pl.pallas_call(
        functools.partial(_ffn_kernel, attn_tail=attn_tail is not None),
        out_shape=out_shape,
        grid=grid,
        in_specs=in_specs,
        out_specs=out_specs,
        scratch_shapes=[
            pltpu.VMEM((N_FFN_CHUNKS, SUBLANES, 2 * FFN_CHUNK), F32),
            pltpu.VMEM((TM_FFN, D_MODEL), BF16),
            pltpu.VMEM((TM_FFN, D_FF), BF16),
            pltpu.VMEM((D_FF, D_MODEL), BF16),
        ],
        compiler_params=pltpu.CompilerParams(dimension_semantics=("arbitrary",) * len(grid),
                                             vmem_limit_bytes=VMEM_LIMIT),
        name="conv_ffn" if attn_tail is None else "attn_out_conv_ffn_norm",
    )(*operands)


_NT_DIMS = (((1,), (1,)), ((), ()))


def _suffix_sum_matrix():
    s = np.arange(BLK)[:, None]
    j = np.arange(BLK)[None, :]
    return jnp.asarray(j > s, BF16)


def _attn_kernel(xk0_ref, xq0_ref, xkn_ref, xqn_ref, wk32_ref, wv32_ref, wq32_ref, u_ref, o_ref,
                 wk_all_ref, wvt_all_ref, wqt_all_ref, k_ref, vt_ref, qt_ref, qm_ref, acc_ref,
                 carry_ref, go_ref):
    i = pl.program_id(2)
    group = pl.program_id(1)

    @pl.when((pl.program_id(0) == 0) & (i == 0))
    def _():
        wk_all_ref[group] = wk32_ref[...].astype(BF16)
        wvt_all_ref[group] = wv32_ref[...].T.astype(BF16)
        wqt_all_ref[group] = (wq32_ref[...].T * (HEAD_DIM ** -0.5)).astype(BF16)

    wk_ref, wvt_ref, wqt_ref = wk_all_ref.at[group], wvt_all_ref.at[group], wqt_all_ref.at[group]

    def project(xk_ref, xq_ref, block, slot):
        rows = pl.ds(pl.multiple_of(block * BLK, BLK), BLK)
        xk = xk_ref[0]
        k_ref[rows, :] = jnp.dot(xk, wk_ref[...], preferred_element_type=F32).astype(BF16)
        vt_ref[:, rows] = lax.dot_general(wvt_ref[...], xk, _NT_DIMS,
                                          preferred_element_type=F32).astype(BF16)
        qt_ref[slot] = lax.dot_general(wqt_ref[...], xq_ref[0], _NT_DIMS,
                                       preferred_element_type=F32).astype(BF16)

    @pl.when(i == 0)
    def _():
        project(xk0_ref, xq0_ref, 0, 0)

    pair_row = lax.broadcasted_iota(jnp.int32, (PAIR, 1), 0)
    for h in range(HEAD_GROUP):
        first = (h % 2) * HEAD_DIM
        mine = (pair_row >= first) & (pair_row < first + HEAD_DIM)
        qp = qt_ref[lax.rem(i, 2), (h // 2) * PAIR:(h // 2 + 1) * PAIR, :]
        qm_ref[h] = jnp.where(mine, qp, jnp.zeros((), BF16))
    acc_ref[...] = jnp.zeros(acc_ref.shape, F32)
    carry_ref[...] = jnp.zeros(carry_ref.shape, F32)

    def sweep(blocks, project_ahead=False):
        row = lax.broadcasted_iota(jnp.int32, (BLK, BLK), 0)
        col = lax.broadcasted_iota(jnp.int32, (BLK, BLK), 1)
        valid = {None: None}
        for j, kind in blocks:
            if kind == "causal":
                valid[kind] = row < col
            elif kind == "edge":
                key = j * BLK + row
                valid[kind] = (key < i * BLK + col) & (key >= ROW_PAD)
        chains = [(pl.multiple_of(j * BLK, BLK), kind, h) for j, kind in blocks for h in range(HEAD_GROUP)]
        zs = [jnp.dot(k_ref[pl.ds(start, BLK), (h // 2) * PAIR:(h // 2 + 1) * PAIR], qm_ref[h],
                      preferred_element_type=F32) for start, _, h in chains]
        if project_ahead:
            project(xkn_ref, xqn_ref, jnp.minimum(i + 1, N_BLK - 1), lax.rem(i + 1, 2))

        log_betas, sums, totals = [], [], []
        for (_, kind, h), z in zip(chains, zs):
            soft = jnp.log(1.0 + jnp.exp2(jnp.abs(z) * -LOG2_E))
            log_beta = jnp.minimum(z, 0.0) - soft
            log_1m = log_beta - z
            if kind is not None:
                log_1m = jnp.where(valid[kind], log_1m, 0.0)
            log_betas.append(log_beta)
            rounded = log_1m.astype(BF16)
            s = jnp.dot(u_ref[...], rounded, preferred_element_type=F32)
            sums.append(s)
            totals.append(s[0:1, :] + rounded[0:1, :].astype(F32))
        carry = [carry_ref[h] for h in range(HEAD_GROUP)]
        carry_in = []
        for (_, _, h), total in zip(chains, totals):
            carry_in.append(carry[h])
            carry[h] = carry[h] + total
        most = carry[0]
        for h in range(HEAD_GROUP):
            carry_ref[h] = carry[h]
            most = jnp.maximum(most, carry[h])
        go_ref[0] = (jnp.max(most) >= UNDERFLOW_LOG).astype(jnp.int32)
        for (start, kind, h), log_beta, s, c_in in zip(chains, log_betas, sums, carry_in):
            a = jnp.exp(log_beta + s[0:BLK] + c_in[0:1, :])
            if kind is not None:
                a = jnp.where(valid[kind], a, 0.0)
            vb = vt_ref[h * HEAD_DIM:(h + 1) * HEAD_DIM, pl.ds(start, BLK)]
            rows = slice((h % 2) * HEAD_DIM, (h % 2 + 1) * HEAD_DIM)
            acc_ref[h // 2, rows, :] += jnp.dot(vb, a.astype(BF16), preferred_element_type=F32)

    @pl.when(i >= 2)
    def _():
        sweep([(i, "causal"), (i - 1, None)], project_ahead=True)

    @pl.when(i < 2)
    def _():
        sweep([(i, "edge")], project_ahead=True)

    def interior(j):
        sweep([(j, None)])
        return j - 1

    j_end = lax.while_loop(lambda j: (j >= 1) & (go_ref[0] != 0), interior,
                           jnp.where(i >= 2, i - 2, i - 1))

    @pl.when((j_end == 0) & (go_ref[0] != 0))
    def _():
        sweep([(0, "edge")])

    for p in range(HEAD_GROUP // 2):
        o_ref[0, :, p * PAIR:(p + 1) * PAIR] = acc_ref[p].T.astype(BF16)


def _attention(xk, xq, w_kv, w_q):
    gw = HEAD_GROUP * HEAD_DIM
    n_groups = N_HEADS // HEAD_GROUP
    first_block = pl.BlockSpec((1, BLK, D_MODEL), lambda b, g, i: (b, 0, 0))
    next_block = pl.BlockSpec((1, BLK, D_MODEL), lambda b, g, i: (b, jnp.minimum(i + 1, N_BLK - 1), 0))
    return pl.pallas_call(
        _attn_kernel,
        out_shape=jax.ShapeDtypeStruct((BATCH, LP, D_MODEL), BF16),
        grid=(BATCH, n_groups, N_BLK),
        in_specs=[
            first_block, first_block, next_block, next_block,
            pl.BlockSpec((D_MODEL, gw), lambda b, g, i: (0, g)),
            pl.BlockSpec((D_MODEL, gw), lambda b, g, i: (0, n_groups + g)),
            pl.BlockSpec((D_MODEL, gw), lambda b, g, i: (0, g)),
            pl.BlockSpec((BLK, BLK), lambda b, g, i: (0, 0)),
        ],
        out_specs=pl.BlockSpec((1, BLK, gw), lambda b, g, i: (b, i, g)),
        scratch_shapes=[pltpu.VMEM((n_groups, D_MODEL, gw), BF16),
                        pltpu.VMEM((n_groups, gw, D_MODEL), BF16),
                        pltpu.VMEM((n_groups, gw, D_MODEL), BF16),
                        pltpu.VMEM((LP, gw), BF16),
                        pltpu.VMEM((gw, LP), BF16),
                        pltpu.VMEM((2, gw, BLK), BF16),
                        pltpu.VMEM((HEAD_GROUP, PAIR, BLK), BF16),
                        pltpu.VMEM((HEAD_GROUP // 2, PAIR, BLK), F32),
                        pltpu.VMEM((HEAD_GROUP, SUBLANES, BLK), F32),
                        pltpu.SMEM((1,), jnp.int32)],
        compiler_params=pltpu.CompilerParams(
            dimension_semantics=("arbitrary", "arbitrary", "arbitrary"), vmem_limit_bytes=VMEM_LIMIT),
        name="stickbreak_attn",
    )(xk, xq, xk, xq, w_kv, w_kv, w_q, _suffix_sum_matrix())


def kernel(x, meta_tokens, mix_norm, ffn_norm, pool_w, pool_scale, kv_norm, w_kv, w_q, w_o,
           ffn_w_up, ffn_conv_w, ffn_conv_b, ffn_w_down, final_norm):
    row = lambda v: v.reshape(1, -1)

    h, w_up = _pool_layer(x, meta_tokens.astype(x.dtype), row(mix_norm[0]), pool_w[0].astype(BF16),
                          row(pool_scale[0]), ffn_w_up)

    def ffn(h, layer, **mode):
        return _ffn_layer(h.reshape(ROWS, D_MODEL), layer, ffn_norm[:, None, :], w_up, ffn_conv_w,
                          ffn_conv_b[:, None, :], ffn_w_down, **mode)

    h, xk, xq = ffn(h, 0, qk_gains=(row(kv_norm), row(mix_norm[1])))

    by_batch = lambda a: a.reshape(BATCH, LP, D_MODEL)
    attn = _attention(by_batch(xk), by_batch(xq), w_kv, w_q[0])
    return ffn(h, 1, attn_tail=(attn.reshape(ROWS, D_MODEL), w_o[0].astype(BF16), row(final_norm)))
```

```python
import functools

import numpy as np
import jax
import jax.numpy as jnp
from jax import lax
from jax.experimental import pallas as pl
from jax.experimental.pallas import tpu as pltpu

D_MODEL = 1024
BATCH = 4
SEQ = 4096
N_META = 16
POOL_WINDOWS = (2, 4, 8, 16)
POOL_GROUP_DIM = D_MODEL // len(POOL_WINDOWS)
N_HEADS = 16
HEAD_DIM = D_MODEL // N_HEADS
D_FF = 2816
CONV_WIDTH = 3
RMS_EPS = 1e-6

SUBLANES = 8
MXU_DIM = 256

BLK = MXU_DIM
ROW_PAD = BLK - N_META
LP = SEQ + BLK
N_BLK = LP // BLK
ROWS = BATCH * LP
MAX_WINDOW = max(POOL_WINDOWS)
HEAD_GROUP = 8
PAIR = 2 * HEAD_DIM
UNDERFLOW_LOG = -104.0
LOG2_E = 1.4426950408889634

TM_POOL = BLK
TM_FFN = 512
FFN_CHUNK = MXU_DIM
N_FFN_CHUNKS = D_FF // FFN_CHUNK
WUP_CAST_COLS = 2 * FFN_CHUNK
WUP_CAST_STEPS = 2 * D_FF // WUP_CAST_COLS
VMEM_LIMIT = 56 * 1024 * 1024

F32 = jnp.float32
BF16 = jnp.bfloat16


def _rms_scale(x):
    return lax.rsqrt(jnp.mean(x * x, axis=-1, keepdims=True) + RMS_EPS)


def _pool_kernel(x_ref, meta_ref, g_ref, w_ref, sc_ref, wup32_ref, o_ref, wup_bf_ref, buf_ref):
    j = pl.program_id(0)
    tm = TM_POOL

    @pl.when(j == 0)
    def _():
        buf_ref[:, 0:MAX_WINDOW, :] = jnp.zeros((BATCH, MAX_WINDOW, D_MODEL), F32)

    @pl.when(j < WUP_CAST_STEPS)
    def _():
        wup_bf_ref[...] = wup32_ref[...].astype(BF16)

    head_tile = jnp.concatenate([jnp.zeros((ROW_PAD, D_MODEL), F32), meta_ref[...]], axis=0)
    pos = j * tm + lax.broadcasted_iota(jnp.int32, (tm, 1), 0) - ROW_PAD
    xs, diffs = [], [[] for _ in POOL_WINDOWS]
    for b in range(BATCH):
        x = jnp.where(j == 0, head_tile, x_ref[b])
        xn = (x * _rms_scale(x)) * g_ref[...]
        buf_ref[b, MAX_WINDOW:MAX_WINDOW + tm, :] = xn
        win = buf_ref[b]
        shift = 1
        for g, w in enumerate(POOL_WINDOWS):
            assert w == 2 * shift
            win = win[:, (POOL_GROUP_DIM if g else 0):]
            win = win + pltpu.roll(win, shift, axis=0)
            count = jnp.clip(pos + 1, 1, w).astype(F32)
            mean = win[MAX_WINDOW:, 0:POOL_GROUP_DIM] / count
            diffs[g].append((mean - xn[:, g * POOL_GROUP_DIM:(g + 1) * POOL_GROUP_DIM]).astype(BF16))
            shift = w
        buf_ref[b, 0:MAX_WINDOW, :] = buf_ref[b, tm:tm + MAX_WINDOW, :]
        xs.append(x)
    y = jnp.concatenate([jnp.dot(jnp.concatenate(diffs[g], axis=0), w_ref[g], preferred_element_type=F32)
                         for g in range(len(POOL_WINDOWS))], axis=1) * sc_ref[...]
    for b in range(BATCH):
        o_ref[b] = jnp.where(pos >= 0, xs[b] + y[b * tm:(b + 1) * tm], 0.0)


def _pool_layer(x, meta, gain, w, scale, w_up):
    const2 = lambda j: (0, 0)
    n_layers = w_up.shape[0]
    cast_block = pl.BlockSpec((n_layers, D_MODEL, WUP_CAST_COLS),
                              lambda j: (0, 0, jnp.minimum(j, WUP_CAST_STEPS - 1)))
    return pl.pallas_call(
        _pool_kernel,
        out_shape=(jax.ShapeDtypeStruct((BATCH, LP, D_MODEL), F32),
                   jax.ShapeDtypeStruct(w_up.shape, BF16)),
        grid=(LP // TM_POOL,),
        in_specs=[
            pl.BlockSpec((BATCH, TM_POOL, D_MODEL), lambda j: (0, jnp.maximum(j - 1, 0), 0)),
            pl.BlockSpec((N_META, D_MODEL), const2),
            pl.BlockSpec((1, D_MODEL), const2),
            pl.BlockSpec((len(POOL_WINDOWS), POOL_GROUP_DIM, POOL_GROUP_DIM), lambda j: (0, 0, 0)),
            pl.BlockSpec((1, D_MODEL), const2),
            cast_block,
        ],
        out_specs=(pl.BlockSpec((BATCH, TM_POOL, D_MODEL), lambda j: (0, j, 0)), cast_block),
        scratch_shapes=[pltpu.VMEM((BATCH, MAX_WINDOW + TM_POOL, D_MODEL), F32)],
        compiler_params=pltpu.CompilerParams(dimension_semantics=("arbitrary",),
                                             vmem_limit_bytes=VMEM_LIMIT),
        name="pool_mixer",
    )(x, meta, gain, w, scale, w_up)


def _pad_row_mask(row0, tm):
    row = row0 + lax.broadcasted_iota(jnp.int32, (tm, 1), 0)
    is_pad = jnp.zeros((tm, 1), jnp.bool_)
    for b in range(BATCH):
        is_pad = is_pad | ((row >= b * LP) & (row < b * LP + ROW_PAD))
    return jnp.logical_not(is_pad)


def _ffn_kernel(*refs, attn_tail):
    if attn_tail:
        (h_ref, attn_ref, wo_ref, fg_ref, meta_up_ref, g_ref, wup_ref, cw_ref, cb_ref, wdn_ref,
         o_ref, *scratch) = refs
    else:
        (h_ref, gkv_ref, gq_ref, g_ref, wup_ref, cw_ref, cb_ref, wdn_ref,
         o_ref, xk_ref, xq_ref, *scratch) = refs
    carry_ref, xn_ref, act_ref, wdn_bf_ref = scratch
    tm = TM_FFN
    cw = 2 * FFN_CHUNK
    first_step = (pl.program_id(0) == 0) & (pl.program_id(1) == 0) if attn_tail else pl.program_id(0) == 0

    @pl.when(first_step)
    def _():
        wdn_bf_ref[...] = wdn_ref[...].astype(BF16)

    def chunk_cols(ref, c, rows=slice(None)):
        g0 = c * FFN_CHUNK
        return jnp.concatenate([ref[rows, g0:g0 + FFN_CHUNK],
                                ref[rows, D_FF + g0:D_FF + g0 + FFN_CHUNK]], axis=1)

    if attn_tail:
        @pl.when(pl.program_id(1) == 0)
        def _():
            for c in range(N_FFN_CHUNKS):
                carry_ref[c] = chunk_cols(meta_up_ref, c, slice(N_META - SUBLANES, N_META))
    else:
        i = pl.program_id(0)

        @pl.when(i == 0)
        def _():
            carry_ref[...] = jnp.zeros(carry_ref.shape, F32)

    x = h_ref[...]
    if attn_tail:
        x = x + jnp.dot(attn_ref[...], wo_ref[...], preferred_element_type=F32)
    xn_ref[...] = ((x * _rms_scale(x)) * g_ref[...]).astype(BF16)

    for c in range(N_FFN_CHUNKS):
        u = jnp.dot(xn_ref[...], chunk_cols(wup_ref, c), preferred_element_type=F32)
        ext = jnp.concatenate([carry_ref[c], u], axis=0)
        carry_ref[c] = u[tm - SUBLANES:tm, :]
        w = chunk_cols(cw_ref, c)
        cv = chunk_cols(cb_ref, c) + w[CONV_WIDTH - 1:CONV_WIDTH] * u
        for k in range(CONV_WIDTH - 1):
            first = SUBLANES - (CONV_WIDTH - 1) + k
            cv = cv + w[k:k + 1] * ext[first:first + tm, :]
        half_gate = 0.5 * cv[:, 0:FFN_CHUNK]
        act = (half_gate + half_gate * jnp.tanh(half_gate)) * cv[:, FFN_CHUNK:cw]
        act_ref[:, c * FFN_CHUNK:(c + 1) * FFN_CHUNK] = act.astype(BF16)

    out = x + jnp.dot(act_ref[...], wdn_bf_ref[...], preferred_element_type=F32)
    if attn_tail:
        o_ref[...] = (out * _rms_scale(out)) * fg_ref[...]
    else:
        out = jnp.where(_pad_row_mask(i * tm, tm), out, 0.0)
        o_ref[...] = out
        normed = out * _rms_scale(out)
        xk_ref[...] = (normed * gkv_ref[...]).astype(BF16)
        xq_ref[...] = (normed * gq_ref[...]).astype(BF16)


def _meta_up_kernel(*refs):
    h_refs, attn_refs = refs[:BATCH], refs[BATCH:2 * BATCH]
    wo_ref, g_ref, wup_ref, o_ref = refs[2 * BATCH:]
    h = jnp.concatenate([r[...] for r in h_refs], axis=0)
    attn = jnp.concatenate([r[...] for r in attn_refs], axis=0)
    x = h + jnp.dot(attn, wo_ref[...], preferred_element_type=F32)
    xn = ((x * _rms_scale(x)) * g_ref[...]).astype(BF16)
    o_ref[...] = jnp.dot(xn, wup_ref[...], preferred_element_type=F32)


def _meta_up_projection(h, attn, w_o, layer, gain, w_up):
    const = lambda s: (0, 0)
    of_layer = lambda s: (layer, 0, 0)
    meta_rows = [pl.BlockSpec((pl.Element(N_META), pl.Element(D_MODEL)),
                              functools.partial(lambda b, s: (b * LP + ROW_PAD, 0), b))
                 for b in range(BATCH)]
    out = pl.pallas_call(
        _meta_up_kernel,
        out_shape=jax.ShapeDtypeStruct((BATCH * N_META, 2 * D_FF), F32),
        grid=(1,),
        in_specs=meta_rows + meta_rows + [
            pl.BlockSpec((D_MODEL, D_MODEL), const, pipeline_mode=pl.Buffered(1)),
            pl.BlockSpec((None, 1, D_MODEL), of_layer),
            pl.BlockSpec((None, D_MODEL, 2 * D_FF), of_layer, pipeline_mode=pl.Buffered(1))],
        out_specs=pl.BlockSpec((BATCH * N_META, 2 * D_FF), const),
        compiler_params=pltpu.CompilerParams(dimension_semantics=("arbitrary",),
                                             vmem_limit_bytes=VMEM_LIMIT),
        name="meta_up_proj",
    )(*([h] * BATCH), *([attn] * BATCH), w_o, gain, w_up)
    return out.reshape(BATCH, N_META, 2 * D_FF)


def _ffn_layer(h, layer, gain, w_up, conv_w, conv_b, w_down, attn_tail=None, qk_gains=None):
    const = lambda *_: (0, 0)
    of_layer = lambda *_: (layer, 0, 0)
    if attn_tail is not None:
        attn, w_o, final_gain = attn_tail
        meta_up = _meta_up_projection(h, attn, w_o, layer, gain, w_up)
        grid = (BATCH, SEQ // TM_FFN)
        blk_per_tile = TM_FFN // BLK
        window = lambda b, k: ((b * N_BLK + 1 + k * blk_per_tile) * BLK, 0)
        row_tile = pl.BlockSpec((pl.Element(TM_FFN), pl.Element(D_MODEL)), window)
        operands = [h, attn, w_o, final_gain, meta_up]
        in_specs = [row_tile, row_tile,
                    pl.BlockSpec((D_MODEL, D_MODEL), const, pipeline_mode=pl.Buffered(1)),
                    pl.BlockSpec((1, D_MODEL), const),
                    pl.BlockSpec((None, N_META, 2 * D_FF), lambda b, k: (b, 0, 0))]
        out_shape = jax.ShapeDtypeStruct((BATCH, SEQ, D_MODEL), F32)
        out_specs = pl.BlockSpec((None, TM_FFN, D_MODEL), lambda b, k: (b, k, 0))
    else:
        grid = (ROWS // TM_FFN,)
        row_tile = pl.BlockSpec((TM_FFN, D_MODEL), lambda i: (i, 0))
        operands = [h] + list(qk_gains)
        in_specs = [row_tile] + [pl.BlockSpec((1, D_MODEL), const)] * 2
        out_shape = (jax.ShapeDtypeStruct((ROWS, D_MODEL), F32),) + (
            jax.ShapeDtypeStruct((ROWS, D_MODEL), BF16),) * 2
        out_specs = (row_tile,) * 3
    operands += [gain, w_up, conv_w, conv_b, w_down]
    in_specs += [
        pl.BlockSpec((None, 1, D_MODEL), of_layer),
        pl.BlockSpec((None, D_MODEL, 2 * D_FF), of_layer, pipeline_mode=pl.Buffered(1)),
        pl.BlockSpec((None, CONV_WIDTH, 2 * D_FF), of_layer),
        pl.BlockSpec((None, 1, 2 * D_FF), of_layer),
        pl.BlockSpec((None, D_FF, D_MODEL), of_layer, pipeline_mode=pl.Buffered(1)),
    ]
    return pl.pallas_call(
        functools.partial(_ffn_kernel, attn_tail=attn_tail is not None),
        out_shape=out_shape,
        grid=grid,
        in_specs=in_specs,
        out_specs=out_specs,
        scratch_shapes=[
            pltpu.VMEM((N_FFN_CHUNKS, SUBLANES, 2 * FFN_CHUNK), F32),
            pltpu.VMEM((TM_FFN, D_MODEL), BF16),
            pltpu.VMEM((TM_FFN, D_FF), BF16),
            pltpu.VMEM((D_FF, D_MODEL), BF16),
        ],
        compiler_params=pltpu.CompilerParams(dimension_semantics=("arbitrary",) * len(grid),
                                             vmem_limit_bytes=VMEM_LIMIT),
        name="conv_ffn" if attn_tail is None else "attn_out_conv_ffn_norm",
    )(*operands)


_NT_DIMS = (((1,), (1,)), ((), ()))


def _suffix_sum_matrix():
    s = np.arange(BLK)[:, None]
    j = np.arange(BLK)[None, :]
    return jnp.asarray(j > s, BF16)


def _attn_kernel(xk0_ref, xq0_ref, xkn_ref, xqn_ref, wk32_ref, wv32_ref, wq32_ref, u_ref, o_ref,
                 wk_all_ref, wvt_all_ref, wqt_all_ref, k_ref, vt_ref, qt_ref, qm_ref, acc_ref,
                 carry_ref, go_ref):
    i = pl.program_id(2)
    group = pl.program_id(1)

    @pl.when((pl.program_id(0) == 0) & (i == 0))
    def _():
        wk_all_ref[group] = wk32_ref[...].astype(BF16)
        wvt_all_ref[group] = wv32_ref[...].T.astype(BF16)
        wqt_all_ref[group] = (wq32_ref[...].T * (HEAD_DIM ** -0.5)).astype(BF16)

    wk_ref, wvt_ref, wqt_ref = wk_all_ref.at[group], wvt_all_ref.at[group], wqt_all_ref.at[group]

    def project(xk_ref, xq_ref, block, slot):
        rows = pl.ds(pl.multiple_of(block * BLK, BLK), BLK)
        xk = xk_ref[0]
        k_ref[rows, :] = jnp.dot(xk, wk_ref[...], preferred_element_type=F32).astype(BF16)
        vt_ref[:, rows] = lax.dot_general(wvt_ref[...], xk, _NT_DIMS,
                                          preferred_element_type=F32).astype(BF16)
        qt_ref[slot] = lax.dot_general(wqt_ref[...], xq_ref[0], _NT_DIMS,
                                       preferred_element_type=F32).astype(BF16)

    @pl.when(i == 0)
    def _():
        project(xk0_ref, xq0_ref, 0, 0)

    pair_row = lax.broadcasted_iota(jnp.int32, (PAIR, 1), 0)
    for h in range(HEAD_GROUP):
        first = (h % 2) * HEAD_DIM
        mine = (pair_row >= first) & (pair_row < first + HEAD_DIM)
        qp = qt_ref[lax.rem(i, 2), (h // 2) * PAIR:(h // 2 + 1) * PAIR, :]
        qm_ref[h] = jnp.where(mine, qp, jnp.zeros((), BF16))
    acc_ref[...] = jnp.zeros(acc_ref.shape, F32)
    carry_ref[...] = jnp.zeros(carry_ref.shape, F32)

    def sweep(blocks, project_ahead=False):
        row = lax.broadcasted_iota(jnp.int32, (BLK, BLK), 0)
        col = lax.broadcasted_iota(jnp.int32, (BLK, BLK), 1)
        valid = {None: None}
        for j, kind in blocks:
            if kind == "causal":
                valid[kind] = row < col
            elif kind == "edge":
                key = j * BLK + row
                valid[kind] = (key < i * BLK + col) & (key >= ROW_PAD)
        chains = [(pl.multiple_of(j * BLK, BLK), kind, h) for j, kind in blocks for h in range(HEAD_GROUP)]
        zs = [jnp.dot(k_ref[pl.ds(start, BLK), (h // 2) * PAIR:(h // 2 + 1) * PAIR], qm_ref[h],
                      preferred_element_type=F32) for start, _, h in chains]
        if project_ahead:
            project(xkn_ref, xqn_ref, jnp.minimum(i + 1, N_BLK - 1), lax.rem(i + 1, 2))

        log_betas, sums, totals = [], [], []
        for (_, kind, h), z in zip(chains, zs):
            soft = jnp.log(1.0 + jnp.exp2(jnp.abs(z) * -LOG2_E))
            log_beta = jnp.minimum(z, 0.0) - soft
            log_1m = log_beta - z
            if kind is not None:
                log_1m = jnp.where(valid[kind], log_1m, 0.0)
            log_betas.append(log_beta)
            rounded = log_1m.astype(BF16)
            s = jnp.dot(u_ref[...], rounded, preferred_element_type=F32)
            sums.append(s)
            totals.append(s[0:1, :] + rounded[0:1, :].astype(F32))
        carry = [carry_ref[h] for h in range(HEAD_GROUP)]
        carry_in = []
        for (_, _, h), total in zip(chains, totals):
            carry_in.append(carry[h])
            carry[h] = carry[h] + total
        most = carry[0]
        for h in range(HEAD_GROUP):
            carry_ref[h] = carry[h]
            most = jnp.maximum(most, carry[h])
        go_ref[0] = (jnp.max(most) >= UNDERFLOW_LOG).astype(jnp.int32)
        for (start, kind, h), log_beta, s, c_in in zip(chains, log_betas, sums, carry_in):
            a = jnp.exp(log_beta + s[0:BLK])
            if kind is not None:
                a = jnp.where(valid[kind], a, 0.0)
            vb = vt_ref[h * HEAD_DIM:(h + 1) * HEAD_DIM, pl.ds(start, BLK)]
            rows = slice((h % 2) * HEAD_DIM, (h % 2 + 1) * HEAD_DIM)
            acc_ref[h // 2, rows, :] += (jnp.dot(vb, a.astype(BF16), preferred_element_type=F32)
                                         * jnp.exp(c_in[0:1, :]))

    @pl.when(i >= 2)
    def _():
        sweep([(i, "causal"), (i - 1, None)], project_ahead=True)

    @pl.when(i < 2)
    def _():
        sweep([(i, "edge")], project_ahead=True)

    def interior(j):
        sweep([(j, None)])
        return j - 1

    j_end = lax.while_loop(lambda j: (j >= 1) & (go_ref[0] != 0), interior,
                           jnp.where(i >= 2, i - 2, i - 1))

    @pl.when((j_end == 0) & (go_ref[0] != 0))
    def _():
        sweep([(0, "edge")])

    for p in range(HEAD_GROUP // 2):
        o_ref[0, :, p * PAIR:(p + 1) * PAIR] = acc_ref[p].T.astype(BF16)


def _attention(xk, xq, w_kv, w_q):
    gw = HEAD_GROUP * HEAD_DIM
    n_groups = N_HEADS // HEAD_GROUP
    first_block = pl.BlockSpec((1, BLK, D_MODEL), lambda b, g, i: (b, 0, 0))
    next_block = pl.BlockSpec((1, BLK, D_MODEL), lambda b, g, i: (b, jnp.minimum(i + 1, N_BLK - 1), 0))
    return pl.pallas_call(
        _attn_kernel,
        out_shape=jax.ShapeDtypeStruct((BATCH, LP, D_MODEL), BF16),
        grid=(BATCH, n_groups, N_BLK),
        in_specs=[
            first_block, first_block, next_block, next_block,
            pl.BlockSpec((D_MODEL, gw), lambda b, g, i: (0, g)),
            pl.BlockSpec((D_MODEL, gw), lambda b, g, i: (0, n_groups + g)),
            pl.BlockSpec((D_MODEL, gw), lambda b, g, i: (0, g)),
            pl.BlockSpec((BLK, BLK), lambda b, g, i: (0, 0)),
        ],
        out_specs=pl.BlockSpec((1, BLK, gw), lambda b, g, i: (b, i, g)),
        scratch_shapes=[pltpu.VMEM((n_groups, D_MODEL, gw), BF16),
                        pltpu.VMEM((n_groups, gw, D_MODEL), BF16),
                        pltpu.VMEM((n_groups, gw, D_MODEL), BF16),
                        pltpu.VMEM((LP, gw), BF16),
                        pltpu.VMEM((gw, LP), BF16),
                        pltpu.VMEM((2, gw, BLK), BF16),
                        pltpu.VMEM((HEAD_GROUP, PAIR, BLK), BF16),
                        pltpu.VMEM((HEAD_GROUP // 2, PAIR, BLK), F32),
                        pltpu.VMEM((HEAD_GROUP, SUBLANES, BLK), F32),
                        pltpu.SMEM((1,), jnp.int32)],
        compiler_params=pltpu.CompilerParams(
            dimension_semantics=("arbitrary", "arbitrary", "arbitrary"), vmem_limit_bytes=VMEM_LIMIT),
        name="stickbreak_attn",
    )(xk, xq, xk, xq, w_kv, w_kv, w_q, _suffix_sum_matrix())


def kernel(x, meta_tokens, mix_norm, ffn_norm, pool_w, pool_scale, kv_norm, w_kv, w_q, w_o,
           ffn_w_up, ffn_conv_w, ffn_conv_b, ffn_w_down, final_norm):
    row = lambda v: v.reshape(1, -1)

    h, w_up = _pool_layer(x, meta_tokens.astype(x.dtype), row(mix_norm[0]), pool_w[0].astype(BF16),
                          row(pool_scale[0]), ffn_w_up)

    def ffn(h, layer, **mode):
        return _ffn_layer(h.reshape(ROWS, D_MODEL), layer, ffn_norm[:, None, :], w_up, ffn_conv_w,
                          ffn_conv_b[:, None, :], ffn_w_down, **mode)

    h, xk, xq = ffn(h, 0, qk_gains=(row(kv_norm), row(mix_norm[1])))

    by_batch = lambda a: a.reshape(BATCH, LP, D_MODEL)
    attn = _attention(by_batch(xk), by_batch(xq), w_kv, w_q[0])
    return ffn(h, 1, attn_tail=(attn.reshape(ROWS, D_MODEL), w_o[0].astype(BF16), row(final_norm)))
```
